```python
import jax, jax.numpy as jnp
from jax import lax
import numpy as np

D_MODEL = 2048
BATCH = 2
SEQ = 16384
DEPTH = 1

NORM_EPS = 1e-5
RWKV_WIDTH = D_MODEL // 2
RWKV_HEAD = 64
RWKV_HEADS = RWKV_WIDTH // RWKV_HEAD
DECAY_LORA = 64
ICLR_LORA = 64
GATE_LORA = 128
LNX_EPS = 64e-5
HEAD_DIM = 64
ATTN_HEADS = (D_MODEL // 2) // HEAD_DIM
KV_HEADS = 4
GROUP = ATTN_HEADS // KV_HEADS
WINDOW = 128
N_EXPERTS = 32
TOP_K = 4
D_FF = D_MODEL
SWIGLU_LIMIT = 7.0
SWIGLU_ALPHA = 1.702
MOE_BLOCK = 512
RWKV_SPLITS = (RWKV_WIDTH, RWKV_WIDTH, RWKV_WIDTH, DECAY_LORA, ICLR_LORA, GATE_LORA)
RWKV_COLS = sum(RWKV_SPLITS)
QKV_SPLITS = (ATTN_HEADS * HEAD_DIM, KV_HEADS * HEAD_DIM, KV_HEADS * HEAD_DIM)
QKV_COLS = sum(QKV_SPLITS)
IN_COLS = RWKV_COLS + QKV_COLS + 2 * D_MODEL

kernel_name = 'hybrid_rwkv7_swa_sink_moe'


def _split(t, sizes):
    idx = tuple(int(i) for i in np.cumsum(sizes)[:-1])
    return jnp.split(t, idx, axis=-1)


def rms_norm(x, g):
    xf = x.astype(jnp.float32)
    y = xf * lax.rsqrt(jnp.mean(xf * xf, axis=-1, keepdims=True) + NORM_EPS)
    return (y * g.astype(jnp.float32)).astype(x.dtype)


def token_shift(t):
    return jnp.pad(t[:, :-1], ((0, 0), (1, 0), (0, 0)))


def wkv7_scan(r, w, k, v, a, b):
    B, S, H, N = r.shape
    def step(state, inp):
        r_t, w_t, k_t, v_t, a_t, b_t = inp
        sa = jnp.einsum('bhij,bhj->bhi', state, a_t)
        state = state * w_t[:, :, None, :] + sa[..., None] * b_t[:, :, None, :] + v_t[..., None] * k_t[:, :, None, :]
        return state, jnp.einsum('bhij,bhj->bhi', state, r_t)
    s0 = jnp.zeros((B, H, N, N), jnp.float32)
    seq_major = tuple(jnp.moveaxis(t, 1, 0) for t in (r, w, k, v, a, b))
    _, y = lax.scan(step, s0, seq_major)
    return jnp.moveaxis(y, 0, 1)


def rwkv7_time_mix(r, k, v, zw, za, zg, w0, w_decay_up, a0, w_iclr_up, w_gate_up, k_k, k_a, r_k, lnx_g, lnx_b):
    B, S, C = r.shape
    f32 = jnp.float32
    heads = lambda t: t.astype(f32).reshape(B, S, RWKV_HEADS, RWKV_HEAD)
    w_log = -jax.nn.softplus(-(w0 + jnp.tanh(zw) @ w_decay_up).astype(f32)) - 0.5
    decay = jnp.exp(-jnp.exp(w_log))
    a = jax.nn.sigmoid((a0 + za @ w_iclr_up).astype(f32))
    g = (jax.nn.sigmoid(zg) @ w_gate_up).astype(f32)
    kk = heads(k * k_k)
    kk = kk / jnp.maximum(jnp.sqrt(jnp.sum(kk * kk, axis=-1, keepdims=True)), 1e-12)
    k_f = k.astype(f32) * (1.0 + (a - 1.0) * k_a.astype(f32))
    rh, kh, vh, ah, wh = heads(r), heads(k_f), heads(v), heads(a), heads(decay)
    y = wkv7_scan(rh, wh, kh, vh, -kk, kk * ah)
    mu = jnp.mean(y, axis=-1, keepdims=True)
    var = jnp.mean(jnp.square(y - mu), axis=-1, keepdims=True)
    y = ((y - mu) * lax.rsqrt(var + LNX_EPS)).reshape(B, S, C) * lnx_g.astype(f32) + lnx_b.astype(f32)
    bonus = jnp.sum(rh * kh * r_k.astype(f32), axis=-1, keepdims=True) * vh
    return ((y + bonus.reshape(B, S, C)) * g).astype(r.dtype)


def sliding_window_attention(q, k, v, sinks):
    B, S = q.shape[:2]
    nb = S // WINDOW
    qb = q.reshape(B, nb, WINDOW, KV_HEADS, GROUP, HEAD_DIM)
    def with_prev(t):
        t = t.reshape(B, nb, WINDOW, KV_HEADS, HEAD_DIM)
        prev = jnp.pad(t[:, :-1], ((0, 0), (1, 0), (0, 0), (0, 0), (0, 0)))
        return jnp.concatenate([prev, t], axis=2)
    kc, vc = with_prev(k), with_prev(v)
    s = jnp.einsum('bnqhgd,bnkhd->bnhgqk', qb, kc).astype(jnp.float32) * (HEAD_DIM ** -0.5)
    qi = jnp.arange(WINDOW)[:, None]
    kj = jnp.arange(2 * WINDOW)[None, :]
    band = (kj > qi) & (kj <= qi + WINDOW)
    first = (jnp.arange(nb) == 0)[:, None, None]
    valid = band[None] & ~(first & (kj < WINDOW)[None])
    s = jnp.where(valid[None, :, None, None], s, -jnp.inf)
    sink = sinks.astype(jnp.float32).reshape(KV_HEADS, GROUP)[None, None, :, :, None, None]
    m = jnp.maximum(jnp.max(s, axis=-1, keepdims=True), sink)
    p = jnp.exp(s - m)
    p = p / (jnp.sum(p, axis=-1, keepdims=True) + jnp.exp(sink - m))
    o = jnp.einsum('bnhgqk,bnkhd->bnqhgd', p.astype(v.dtype), vc)
    return o.reshape(B, S, ATTN_HEADS * HEAD_DIM)


def mixer_block(h, w_in, mix_mu, w0, w_decay_up, a0, w_iclr_up, w_gate_up, k_k, k_a, r_k, lnx_g, lnx_b, b_qkv, sinks, w_up_rwkv, w_up_attn, w_out):
    B, S, _ = h.shape
    z = h @ w_in
    z_rwkv = z[..., :RWKV_COLS]
    z_rwkv = z_rwkv + (token_shift(z_rwkv) - z_rwkv) * mix_mu
    r, k, v, zw, za, zg = _split(z_rwkv, RWKV_SPLITS)
    qkv = z[..., RWKV_COLS:RWKV_COLS + QKV_COLS] + b_qkv
    q, ka, va = _split(qkv, QKV_SPLITS)
    gate_rwkv, gate_attn = _split(z[..., RWKV_COLS + QKV_COLS:], (D_MODEL, D_MODEL))
    y_rwkv = rwkv7_time_mix(r, k, v, zw, za, zg, w0, w_decay_up, a0, w_iclr_up, w_gate_up, k_k, k_a, r_k, lnx_g, lnx_b)
    y_attn = sliding_window_attention(q.reshape(B, S, ATTN_HEADS, HEAD_DIM), ka.reshape(B, S, KV_HEADS, HEAD_DIM), va.reshape(B, S, KV_HEADS, HEAD_DIM), sinks)
    merged = jax.nn.sigmoid(gate_rwkv) * (y_rwkv @ w_up_rwkv) + jax.nn.sigmoid(gate_attn) * (y_attn @ w_up_attn)
    return merged @ w_out


def moe_ffn(h, w_router, b_router, w1, b1, w2, b2):
    B, S, D = h.shape
    f32 = jnp.float32
    n_tok = B * S
    hf = h.reshape(n_tok, D)
    logits = (hf @ w_router).astype(f32) + b_router.astype(f32)
    top_val, top_idx = lax.top_k(logits, TOP_K)
    gate = jax.nn.softmax(top_val, axis=-1)
    n_assign = n_tok * TOP_K
    flat_e = top_idx.reshape(-1).astype(jnp.int32)
    order = jnp.argsort(flat_e)
    sorted_e = flat_e[order]
    sorted_tok = (order // TOP_K).astype(jnp.int32)
    sorted_gate = gate.reshape(-1)[order]
    counts = jnp.bincount(flat_e, length=N_EXPERTS).astype(jnp.int32)
    padded = (counts + MOE_BLOCK - 1) // MOE_BLOCK * MOE_BLOCK
    pad_end = jnp.cumsum(padded)
    pad_start = pad_end - padded
    grp_start = jnp.cumsum(counts) - counts
    dest = pad_start[sorted_e] + jnp.arange(n_assign, dtype=jnp.int32) - grp_start[sorted_e]
    n_blocks = -(-n_assign // MOE_BLOCK) + N_EXPERTS
    n_rows = n_blocks * MOE_BLOCK
    row_tok = jnp.zeros((n_rows,), jnp.int32).at[dest].set(sorted_tok)
    row_gate = jnp.zeros((n_rows,), f32).at[dest].set(sorted_gate)
    block_start = jnp.arange(n_blocks, dtype=jnp.int32) * MOE_BLOCK
    block_expert = jnp.minimum(jnp.searchsorted(pad_end, block_start, side='right'), N_EXPERTS - 1).astype(jnp.int32)
    def block_step(acc, inp):
        tok, gw, e = inp
        hid = hf[tok] @ w1[e] + b1[e]
        glu = jnp.minimum(hid[:, :D_FF], SWIGLU_LIMIT)
        lin = jnp.clip(hid[:, D_FF:], -SWIGLU_LIMIT, SWIGLU_LIMIT)
        act = glu * jax.nn.sigmoid(SWIGLU_ALPHA * glu) * (lin + 1.0)
        yb = act @ w2[e] + b2[e]
        return acc.at[tok].add(yb.astype(f32) * gw[:, None]), None
    acc, _ = lax.scan(block_step, jnp.zeros((n_tok, D), f32), (row_tok.reshape(n_blocks, MOE_BLOCK), row_gate.reshape(n_blocks, MOE_BLOCK), block_expert))
    return acc.reshape(B, S, D).astype(h.dtype)


def setup_inputs(seed: int = 0) -> dict:
    key = jax.random.key(seed)
    ks = iter(jax.random.split(key, 32))
    f32 = jnp.float32
    L = DEPTH
    C = RWKV_WIDTH
    def nrm(shape, scale):
        return jax.random.normal(next(ks), shape, f32) * scale
    def gain(shape):
        return 1.0 + nrm(shape, 0.02)
    return {
        'x': nrm((BATCH, SEQ, D_MODEL), 1.0),
        'norm1_g': gain((L, D_MODEL)),
        'w_in': nrm((L, D_MODEL, IN_COLS), D_MODEL ** -0.5),
        'mix_mu': jax.random.uniform(next(ks), (L, RWKV_COLS), f32),
        'w0': jax.random.uniform(next(ks), (L, C), f32, -5.0, -0.5),
        'w_decay_up': nrm((L, DECAY_LORA, C), 0.5 * DECAY_LORA ** -0.5),
        'a0': nrm((L, C), 0.5),
        'w_iclr_up': nrm((L, ICLR_LORA, C), ICLR_LORA ** -0.5),
        'w_gate_up': nrm((L, GATE_LORA, C), GATE_LORA ** -0.5),
        'k_k': 0.85 + nrm((L, C), 0.05),
        'k_a': 1.0 + nrm((L, C), 0.05),
        'r_k': nrm((L, RWKV_HEADS, RWKV_HEAD), 0.1),
        'lnx_g': gain((L, C)),
        'lnx_b': nrm((L, C), 0.02),
        'b_qkv': nrm((L, QKV_COLS), 0.02),
        'sinks': nrm((L, ATTN_HEADS), 0.5),
        'w_up_rwkv': nrm((L, C, D_MODEL), C ** -0.5),
        'w_up_attn': nrm((L, ATTN_HEADS * HEAD_DIM, D_MODEL), (ATTN_HEADS * HEAD_DIM) ** -0.5),
        'w_out': nrm((L, D_MODEL, D_MODEL), D_MODEL ** -0.5),
        'norm2_g': gain((L, D_MODEL)),
        'w_router': nrm((L, D_MODEL, N_EXPERTS), D_MODEL ** -0.5),
        'b_router': nrm((L, N_EXPERTS), 0.01),
        'w1': nrm((L, N_EXPERTS, D_MODEL, 2 * D_FF), D_MODEL ** -0.5),
        'b1': nrm((L, N_EXPERTS, 2 * D_FF), 0.02),
        'w2': nrm((L, N_EXPERTS, D_FF, D_MODEL), D_FF ** -0.5),
        'b2': nrm((L, N_EXPERTS, D_MODEL), 0.02),
        'normf_g': gain((D_MODEL,)),
    }


def reference(x, norm1_g, w_in, mix_mu, w0, w_decay_up, a0, w_iclr_up, w_gate_up, k_k, k_a, r_k, lnx_g, lnx_b, b_qkv, sinks, w_up_rwkv, w_up_attn, w_out, norm2_g, w_router, b_router, w1, b1, w2, b2, normf_g):
    for l in range(DEPTH):
        h = rms_norm(x, norm1_g[l])
        x = x + mixer_block(h, w_in[l], mix_mu[l], w0[l], w_decay_up[l], a0[l], w_iclr_up[l], w_gate_up[l], k_k[l], k_a[l], r_k[l], lnx_g[l], lnx_b[l], b_qkv[l], sinks[l], w_up_rwkv[l], w_up_attn[l], w_out[l])
        h = rms_norm(x, norm2_g[l])
        x = x + moe_ffn(h, w_router[l], b_router[l], w1[l], b1[l], w2[l], b2[l])
    return rms_norm(x, normf_g)
```

```python
import functools

import jax
import jax.numpy as jnp
import numpy as np
from jax import lax
from jax.experimental import pallas as pl
from jax.experimental.pallas import tpu as pltpu

F32 = jnp.float32
BF16 = jnp.bfloat16

NORM_EPS = 1e-5
LNX_EPS = 64e-5
HEAD = 64
WINDOW = 128
KV_HEADS = 4
TOP_K = 4
SWIGLU_LIMIT = 7.0
SWIGLU_ALPHA = 1.702
DECAY_SCALE = float(np.exp(-0.5))

CHUNK = 64
ROW_BLOCK = 1024
FF_TILE = 256
VMEM_LIMIT = 56 * 1024 * 1024
HI = lax.Precision.HIGHEST


def _cparams(sem):
    return pltpu.CompilerParams(dimension_semantics=sem, vmem_limit_bytes=VMEM_LIMIT)


def _resident(shape):
    nd = len(shape)
    return pl.BlockSpec(shape, lambda *_: (0,) * nd, pipeline_mode=pl.Buffered(1))


def _rmsnorm(x, g):
    return x * lax.rsqrt(jnp.mean(x * x, axis=-1, keepdims=True) + NORM_EPS) * g


def _sigmoid(x):
    return 1.0 / (1.0 + jnp.exp(-x))


def _norm_proj_kernel(x_ref, g_ref, w_ref, b_ref, o_ref):
    h = _rmsnorm(x_ref[...], g_ref[...]).astype(BF16)
    z = jnp.dot(h, w_ref[...], preferred_element_type=F32) + b_ref[...]
    o_ref[...] = z.astype(o_ref.dtype)


def _norm_proj(x, g, w, b, out_dtype, tm):
    n, d = x.shape
    cols = w.shape[1]
    return pl.pallas_call(
        _norm_proj_kernel,
        grid=(n // tm,),
        in_specs=[pl.BlockSpec((tm, d), lambda i: (i, 0)), _resident((1, d)), _resident((d, cols)),
                  _resident((1, cols))],
        out_specs=pl.BlockSpec((tm, cols), lambda i: (i, 0)),
        out_shape=jax.ShapeDtypeStruct((n, cols), out_dtype),
        compiler_params=_cparams(("parallel",)),
        name="norm_proj",
    )(x, g, w, b)


def _head_sums(x):
    rows, c = x.shape
    lane = lax.broadcasted_iota(jnp.int32, (rows, 128), 1)
    low = lane < HEAD
    parts = []
    for gi in range(c // 128):
        xg = x[:, gi * 128:(gi + 1) * 128]
        s_lo = jnp.sum(jnp.where(low, xg, 0.0), axis=-1, keepdims=True)
        s_hi = jnp.sum(jnp.where(low, 0.0, xg), axis=-1, keepdims=True)
        parts.append(jnp.where(low, s_lo, s_hi))
    return jnp.concatenate(parts, axis=-1)


def _prep_kernel(seq_blocks, c, z_ref, zp_ref, mu_ref, wlora_ref, wgate_ref, w0_ref, a0_ref, kk_ref, ka_ref,
                 r_ref, lw_ref, kf_ref, v_ref, kn_ref, ba_ref, g_ref):
    i = pl.program_id(0)
    z = z_ref[...]
    tm = z.shape[0]
    prev = jnp.where(i % seq_blocks == 0, 0.0, zp_ref[7:8, :])
    row = lax.broadcasted_iota(jnp.int32, z.shape, 0)
    shifted = jnp.where(row == 0, prev, pltpu.roll(z, 1, 0))
    zs = z + (shifted - z) * mu_ref[...]
    r = zs[:, 0:c]
    k = zs[:, c:2 * c]
    v = zs[:, 2 * c:3 * c]
    zwa = zs[:, 3 * c:3 * c + 128]
    zg = zs[:, 3 * c + 128:3 * c + 256]
    lane = lax.broadcasted_iota(jnp.int32, (tm, 128), 1)
    lora_in = jnp.where(lane < 64, jnp.tanh(zwa), zwa)
    up = jnp.dot(lora_in, wlora_ref[...], preferred_element_type=F32, precision=HI)
    u = w0_ref[...] + up[:, 0:c]
    a = _sigmoid(a0_ref[...] + up[:, c:2 * c])
    g = jnp.dot(_sigmoid(zg), wgate_ref[...], preferred_element_type=F32, precision=HI)
    lw = -DECAY_SCALE * _sigmoid(u)
    kk = k * kk_ref[...]
    kn = kk / jnp.maximum(jnp.sqrt(_head_sums(kk * kk)), 1e-12)
    kf = k * (1.0 + (a - 1.0) * ka_ref[...])
    r_ref[...] = r
    lw_ref[...] = lw
    kf_ref[...] = kf
    v_ref[...] = v
    kn_ref[...] = kn
    ba_ref[...] = kn * a
    g_ref[...] = g


def _rwkv_prep(z, seq, c, mix_mu, w_lora, w_gate, w0, a0, k_k, k_a, tm):
    n, zc = z.shape
    row_spec = pl.BlockSpec((tm, c), lambda i: (i, 0))
    out = jax.ShapeDtypeStruct((n, c), F32)
    return pl.pallas_call(
        functools.partial(_prep_kernel, seq // tm, c),
        grid=(n // tm,),
        in_specs=[pl.BlockSpec((tm, zc), lambda i: (i, 0)),
                  pl.BlockSpec((8, zc), lambda i: (jnp.maximum(i * (tm // 8) - 1, 0), 0)),
                  _resident((1, zc)), _resident((128, 2 * c)), _resident((128, c)),
                  _resident((1, c)), _resident((1, c)), _resident((1, c)), _resident((1, c))],
        out_specs=[row_spec] * 7,
        out_shape=[out] * 7,
        compiler_params=_cparams(("parallel",)),
        name="rwkv_prep",
    )(z, z, mix_mu, w_lora, w_gate, w0, a0, k_k, k_a)


def _split3(x):
    h1 = x.astype(BF16)
    r1 = x - h1.astype(F32)
    h2 = r1.astype(BF16)
    h3 = (r1 - h2.astype(F32)).astype(BF16)
    return h1, h2, h3


def _mm(a, b):
    return jnp.dot(a, b, preferred_element_type=F32)


def _mm_nt(a, b):
    return lax.dot_general(a, b, (((1,), (1,)), ((), ())), preferred_element_type=F32)


def _mm_tn(a, b):
    return lax.dot_general(a, b, (((0,), (0,)), ((), ())), preferred_element_type=F32)


def _unit_lower_inverse(a):
    t = a.shape[0]
    eye = (lax.broadcasted_iota(jnp.int32, (t, t), 0) == lax.broadcasted_iota(jnp.int32, (t, t), 1)).astype(F32)
    m = eye + a
    p = a
    span = 2
    while span < t:
        p = _mm(p, p)
        m = m + _mm(m, p)
        span *= 2
    return m


def _scan_kernel(heads, r_ref, lw_ref, kf_ref, v_ref, kn_ref, ba_ref, g_ref, rk_ref, lng_ref, lnb_ref, y_ref, s_ref):
    @pl.when(pl.program_id(1) == 0)
    def _():
        s_ref[...] = jnp.zeros_like(s_ref)

    t = CHUNK
    lw = lw_ref[...]
    ti = lax.broadcasted_iota(jnp.int32, (t, t), 0)
    si = lax.broadcasted_iota(jnp.int32, (t, t), 1)
    tri = jnp.where(si <= ti, 1.0, 0.0).astype(BF16)
    cum = sum(jnp.dot(tri, part, preferred_element_type=F32) for part in _split3(lw))
    cum_end = cum[t - 1:t, :]
    e_pos = jnp.exp(cum)
    e_neg = jnp.exp(-cum)
    r = r_ref[...]
    kf = kf_ref[...]
    v = v_ref[...]
    ba = ba_ref[...]
    rt = r * e_pos
    at = -kn_ref[...] * jnp.exp(cum - lw)
    bt = ba * e_neg
    kt = kf * e_neg
    e_end = jnp.exp(cum_end - cum)
    bh = ba * e_end
    kh = kf * e_end
    w_end = jnp.exp(cum_end)
    rkf = r * kf * rk_ref[...]
    strict = si < ti
    incl = si <= ti
    outs = []
    for h in range(heads):
        sl = slice(h * HEAD, (h + 1) * HEAD)
        s0 = s_ref[h]
        vh = v[:, sl]
        lhs = jnp.concatenate([at[:, sl], rt[:, sl]], axis=0)
        rhs = jnp.concatenate([bt[:, sl], kt[:, sl]], axis=0)
        amat = _mm_nt(lhs, rhs)
        a_ab = jnp.where(strict, amat[0:t, 0:t], 0.0)
        a_ak = jnp.where(strict, amat[0:t, t:2 * t], 0.0)
        a_rb = jnp.where(incl, amat[t:2 * t, 0:t], 0.0)
        a_rk = jnp.where(incl, amat[t:2 * t, t:2 * t], 0.0)
        minv = _unit_lower_inverse(a_ab)
        u = _mm(minv, _mm_nt(at[:, sl], s0) + _mm(a_ak, vh))
        y = _mm_nt(rt[:, sl], s0) + _mm(a_rb, u) + _mm(a_rk, vh)
        upd = _mm_tn(jnp.concatenate([u, vh], axis=0), jnp.concatenate([bh[:, sl], kh[:, sl]], axis=0))
        s_ref[h] = s0 * w_end[:, sl] + upd
        mu = jnp.mean(y, axis=-1, keepdims=True)
        yc = y - mu
        var = jnp.mean(yc * yc, axis=-1, keepdims=True)
        yn = yc * lax.rsqrt(var + LNX_EPS)
        bonus = jnp.sum(rkf[:, sl], axis=-1, keepdims=True) * vh
        outs.append((yn, bonus))
    yn = jnp.concatenate([o[0] for o in outs], axis=-1)
    bonus = jnp.concatenate([o[1] for o in outs], axis=-1)
    y_ref[...] = ((yn * lng_ref[...] + lnb_ref[...] + bonus) * g_ref[...]).astype(y_ref.dtype)


def _rwkv_scan(r, lw, kf, v, kn, ba, g, r_k, lnx_g, lnx_b):
    b, s, c = r.shape
    heads = c // HEAD
    blk = pl.BlockSpec((None, CHUNK, c), lambda bi, ci: (bi, ci, 0))
    return pl.pallas_call(
        functools.partial(_scan_kernel, heads),
        grid=(b, s // CHUNK),
        in_specs=[blk] * 7 + [_resident((1, c))] * 3,
        out_specs=blk,
        out_shape=jax.ShapeDtypeStruct((b, s, c), BF16),
        scratch_shapes=[pltpu.VMEM((heads, HEAD, HEAD), F32)],
        compiler_params=_cparams(("parallel", "arbitrary")),
        name="rwkv_scan",
    )(r, lw, kf, v, kn, ba, g, r_k, lnx_g, lnx_b)


def _attn_kernel(q_ref, kc_ref, kp_ref, vc_ref, vp_ref, sink_ref, o_ref):
    first = pl.program_id(1) == 0
    w = WINDOW
    group = q_ref.shape[-1] // HEAD // KV_HEADS
    row = lax.broadcasted_iota(jnp.int32, (group * w, 2 * w), 0)
    qi = row & (w - 1)
    kj = lax.broadcasted_iota(jnp.int32, (group * w, 2 * w), 1)
    lo = jnp.where(first, jnp.maximum(qi, w - 1), qi)
    valid = (kj > lo) & (kj <= qi + w)
    grow = lax.broadcasted_iota(jnp.int32, (group * w, 1), 0) // w
    q = q_ref[...]
    outs = []
    for hk in range(KV_HEADS):
        ksl = slice(hk * HEAD, (hk + 1) * HEAD)
        kcat = jnp.concatenate([kp_ref[:, ksl], kc_ref[:, ksl]], axis=0)
        vcat = jnp.concatenate([vp_ref[:, ksl], vc_ref[:, ksl]], axis=0)
        qg = jnp.concatenate([q[:, (hk * group + gi) * HEAD:(hk * group + gi + 1) * HEAD] for gi in range(group)],
                             axis=0)
        sink = jnp.zeros((group * w, 1), F32)
        for gi in range(group):
            sink = jnp.where(grow == gi, sink_ref[hk * group + gi:hk * group + gi + 1, 0:1], sink)
        s = _mm_nt(qg, kcat) * (HEAD ** -0.5)
        s = jnp.where(valid, s, -1e30)
        m = jnp.maximum(jnp.max(s, axis=-1, keepdims=True), sink)
        p = jnp.exp(s - m)
        denom = jnp.sum(p, axis=-1, keepdims=True) + jnp.exp(sink - m)
        o = _mm(p.astype(BF16), vcat) / denom
        for gi in range(group):
            outs.append(o[gi * w:(gi + 1) * w, :])
    o_ref[...] = jnp.concatenate(outs, axis=-1).astype(o_ref.dtype)


def _attention(q, k, v, sinks_b):
    b, s, qc = q.shape
    kc = k.shape[-1]
    nb = s // WINDOW
    cur = lambda bi, i: (bi, i, 0)
    prev = lambda bi, i: (bi, jnp.maximum(i - 1, 0), 0)
    return pl.pallas_call(
        _attn_kernel,
        grid=(b, nb),
        in_specs=[pl.BlockSpec((None, WINDOW, qc), cur),
                  pl.BlockSpec((None, WINDOW, kc), cur), pl.BlockSpec((None, WINDOW, kc), prev),
                  pl.BlockSpec((None, WINDOW, kc), cur), pl.BlockSpec((None, WINDOW, kc), prev),
                  _resident(sinks_b.shape)],
        out_specs=pl.BlockSpec((None, WINDOW, qc), cur),
        out_shape=jax.ShapeDtypeStruct((b, s, qc), BF16),
        compiler_params=_cparams(("parallel", "parallel")),
        name="swa_attention",
    )(q, k, k, v, v, sinks_b)


def _merge_kernel(n_exp, x_ref, yr_ref, ya_ref, gr_ref, ga_ref, wur_ref, wua_ref, wo_ref, n2_ref, wr_ref, br_ref,
                  x1_ref, h2_ref, gate_ref, idx_ref):
    up_r = _mm(yr_ref[...], wur_ref[...])
    up_a = _mm(ya_ref[...], wua_ref[...])
    merged = _sigmoid(gr_ref[...].astype(F32)) * up_r + _sigmoid(ga_ref[...].astype(F32)) * up_a
    x1 = x_ref[...] + _mm(merged.astype(BF16), wo_ref[...])
    x1_ref[...] = x1
    h2 = _rmsnorm(x1, n2_ref[...])
    h2_ref[...] = h2
    logits = jnp.dot(h2, wr_ref[...], preferred_element_type=F32, precision=HI) + br_ref[...]
    tm = logits.shape[0]
    lane_e = lax.broadcasted_iota(jnp.int32, logits.shape, 1).astype(F32)
    lane_o = lax.broadcasted_iota(jnp.int32, (tm, 128), 1)
    vals, gate_o, idx_o = [], jnp.zeros((tm, 128), F32), jnp.zeros((tm, 128), F32)
    idxs = []
    for _ in range(TOP_K):
        mx = jnp.max(logits, axis=-1, keepdims=True)
        ix = jnp.min(jnp.where(logits == mx, lane_e, float(n_exp)), axis=-1, keepdims=True)
        vals.append(mx)
        idxs.append(ix)
        logits = jnp.where(lane_e == ix, -jnp.inf, logits)
    exps = [jnp.exp(vk - vals[0]) for vk in vals]
    tot = exps[0] + exps[1] + exps[2] + exps[3]
    for kk in range(TOP_K):
        gate_o = jnp.where(lane_o == kk, exps[kk] / tot, gate_o)
        idx_o = jnp.where(lane_o == kk, idxs[kk], idx_o)
    gate_ref[...] = gate_o
    idx_ref[...] = idx_o.astype(jnp.int32)


def _merge(x, y_rwkv, y_attn, gates, w_up_r, w_up_a, w_out, norm2_g, w_router, b_router, tm):
    n, d = x.shape
    c = y_rwkv.shape[1]
    n_exp = w_router.shape[1]
    row = lambda cols, j=0: pl.BlockSpec((tm, cols), lambda i: (i, j))
    return pl.pallas_call(
        functools.partial(_merge_kernel, n_exp),
        grid=(n // tm,),
        in_specs=[row(d), row(c), row(c), row(d, 0), row(d, 1),
                  _resident((c, d)), _resident((c, d)), _resident((d, d)), _resident((1, d)),
                  _resident((d, n_exp)), _resident((1, n_exp))],
        out_specs=[row(d), row(d), row(128), row(128)],
        out_shape=[jax.ShapeDtypeStruct((n, d), F32), jax.ShapeDtypeStruct((n, d), F32),
                   jax.ShapeDtypeStruct((n, 128), F32), jax.ShapeDtypeStruct((n, 128), jnp.int32)],
        compiler_params=_cparams(("parallel",)),
        name="merge_router",
    )(x, y_rwkv, y_attn, gates, gates, w_up_r, w_up_a, w_out, norm2_g, w_router, b_router)


def _row_copy(src_hbm, dst, sem, src_row, dst_row):
    return pltpu.make_async_copy(src_hbm.at[pl.ds(src_row, 1)], dst.at[pl.ds(dst_row, 1)], sem)


def _dispatch_kernel(nvalid_ref, tok_ref, h_hbm, o_ref, buf, sem):
    i = pl.program_id(0)
    rows = buf.shape[0]

    @pl.when(i < nvalid_ref[0])
    def _():
        def start(r, carry):
            _row_copy(h_hbm, buf, sem, tok_ref[0, 0, r], r).start()
            return carry

        def wait(r, carry):
            _row_copy(h_hbm, buf, sem, 0, r).wait()
            return carry

        lax.fori_loop(0, rows, start, 0)
        lax.fori_loop(0, rows, wait, 0)
        o_ref[...] = buf[...].astype(o_ref.dtype)

    @pl.when(i >= nvalid_ref[0])
    def _():
        o_ref[...] = jnp.zeros_like(o_ref)


def _dispatch(h2, row_tok, n_valid, n_blocks):
    d = h2.shape[1]
    grid_spec = pltpu.PrefetchScalarGridSpec(
        num_scalar_prefetch=1,
        grid=(n_blocks,),
        in_specs=[pl.BlockSpec((1, 1, ROW_BLOCK), lambda i, nv: (i, 0, 0), memory_space=pltpu.SMEM),
                  pl.BlockSpec(memory_space=pl.ANY)],
        out_specs=pl.BlockSpec((ROW_BLOCK, d), lambda i, nv: (i, 0)),
        scratch_shapes=[pltpu.VMEM((ROW_BLOCK, d), F32), pltpu.SemaphoreType.DMA(())],
    )
    return pl.pallas_call(
        _dispatch_kernel,
        grid_spec=grid_spec,
        out_shape=jax.ShapeDtypeStruct((n_blocks * ROW_BLOCK, d), BF16),
        compiler_params=_cparams(("arbitrary",)),
        name="moe_dispatch",
    )(n_valid, row_tok.reshape(n_blocks, 1, ROW_BLOCK), h2)


def _ffn_kernel(nvalid_ref, bexp_ref, xs_ref, w1g_ref, w1l_ref, b1g_ref, b1l_ref, w2_ref, b2_ref, o_ref):
    i = pl.program_id(0)
    j = pl.program_id(1)

    @pl.when(i < nvalid_ref[0])
    def _():
        x = xs_ref[...]
        hg = _mm(x, w1g_ref[...].astype(BF16)) + b1g_ref[...]
        hl = _mm(x, w1l_ref[...].astype(BF16)) + b1l_ref[...]
        glu = jnp.minimum(hg, SWIGLU_LIMIT)
        lin = jnp.clip(hl, -SWIGLU_LIMIT, SWIGLU_LIMIT)
        act = glu * _sigmoid(SWIGLU_ALPHA * glu) * (lin + 1.0)
        part = _mm(act.astype(BF16), w2_ref[...].astype(BF16))

        @pl.when(j == 0)
        def _():
            o_ref[...] = part + b2_ref[...]

        @pl.when(j > 0)
        def _():
            o_ref[...] += part

    @pl.when((i >= nvalid_ref[0]) & (j == 0))
    def _():
        o_ref[...] = jnp.zeros_like(o_ref)


def _expert_ffn(xs, w1, b1, w2, b2, block_expert, n_valid, n_blocks):
    d = xs.shape[1]
    n_exp, _, ff2 = w1.shape
    ff = ff2 // 2
    n_ff = ff // FF_TILE

    def jj(i, j, nv):
        return jnp.where(i < nv[0], j, n_ff - 1)

    grid_spec = pltpu.PrefetchScalarGridSpec(
        num_scalar_prefetch=2,
        grid=(n_blocks, n_ff),
        in_specs=[pl.BlockSpec((ROW_BLOCK, d), lambda i, j, nv, be: (i, 0)),
                  pl.BlockSpec((None, d, FF_TILE), lambda i, j, nv, be: (be[i], 0, jj(i, j, nv))),
                  pl.BlockSpec((None, d, FF_TILE), lambda i, j, nv, be: (be[i], 0, n_ff + jj(i, j, nv))),
                  pl.BlockSpec((None, 1, FF_TILE), lambda i, j, nv, be: (be[i], 0, jj(i, j, nv))),
                  pl.BlockSpec((None, 1, FF_TILE), lambda i, j, nv, be: (be[i], 0, n_ff + jj(i, j, nv))),
                  pl.BlockSpec((None, FF_TILE, d), lambda i, j, nv, be: (be[i], jj(i, j, nv), 0)),
                  pl.BlockSpec((None, 1, d), lambda i, j, nv, be: (be[i], 0, 0))],
        out_specs=pl.BlockSpec((ROW_BLOCK, d), lambda i, j, nv, be: (i, 0)),
    )
    return pl.pallas_call(
        _ffn_kernel,
        grid_spec=grid_spec,
        out_shape=jax.ShapeDtypeStruct((n_blocks * ROW_BLOCK, d), F32),
        compiler_params=_cparams(("arbitrary", "arbitrary")),
        name="moe_ffn",
    )(n_valid, block_expert, xs, w1, w1, b1.reshape(n_exp, 1, ff2), b1.reshape(n_exp, 1, ff2), w2,
      b2.reshape(n_exp, 1, d))


def _combine_kernel(pos_ref, x1_ref, gate_ref, nf_ref, ys_hbm, o_ref, buf, sem):
    tc = x1_ref.shape[0]

    def start(a, carry):
        _row_copy(ys_hbm, buf, sem, pos_ref[0, 0, a], a).start()
        return carry

    def wait(a, carry):
        _row_copy(ys_hbm, buf, sem, 0, a).wait()
        return carry

    lax.fori_loop(0, TOP_K * tc, start, 0)
    lax.fori_loop(0, TOP_K * tc, wait, 0)
    acc = x1_ref[...]
    gate = gate_ref[...]
    for k in range(TOP_K):
        acc = acc + gate[:, k:k + 1] * buf[k * tc:(k + 1) * tc, :]
    o_ref[...] = _rmsnorm(acc, nf_ref[...])


def _combine(x1, gate, pos, ys, normf_g, tc):
    n, d = x1.shape
    nt = n // tc
    return pl.pallas_call(
        _combine_kernel,
        grid=(nt,),
        in_specs=[pl.BlockSpec((1, 1, TOP_K * tc), lambda i: (i, 0, 0), memory_space=pltpu.SMEM),
                  pl.BlockSpec((tc, d), lambda i: (i, 0)), pl.BlockSpec((tc, 128), lambda i: (i, 0)),
                  _resident((1, d)), pl.BlockSpec(memory_space=pl.ANY)],
        out_specs=pl.BlockSpec((tc, d), lambda i: (i, 0)),
        out_shape=jax.ShapeDtypeStruct((n, d), F32),
        scratch_shapes=[pltpu.VMEM((TOP_K * tc, d), F32), pltpu.SemaphoreType.DMA(())],
        compiler_params=_cparams(("arbitrary",)),
        name="moe_combine",
    )(pos, x1, gate, normf_g, ys)


def _route(top_idx, n_exp, tc):
    n = top_idx.shape[0]
    n_assign = n * TOP_K
    flat_e = top_idx.reshape(-1)
    order = jnp.argsort(flat_e)
    sorted_e = flat_e[order]
    counts = jnp.bincount(flat_e, length=n_exp).astype(jnp.int32)
    padded = (counts + ROW_BLOCK - 1) // ROW_BLOCK * ROW_BLOCK
    pad_end = jnp.cumsum(padded)
    pad_start = pad_end - padded
    grp_start = jnp.cumsum(counts) - counts
    dest = pad_start[sorted_e] + jnp.arange(n_assign, dtype=jnp.int32) - grp_start[sorted_e]
    n_blocks = n_assign // ROW_BLOCK + n_exp
    row_tok = jnp.zeros((n_blocks * ROW_BLOCK,), jnp.int32).at[dest].set((order // TOP_K).astype(jnp.int32))
    slot = jnp.zeros((n_assign,), jnp.int32).at[order].set(dest.astype(jnp.int32))
    block_start = jnp.arange(n_blocks, dtype=jnp.int32) * ROW_BLOCK
    block_expert = jnp.minimum(jnp.searchsorted(pad_end, block_start, side="right"), n_exp - 1).astype(jnp.int32)
    n_valid = (pad_end[-1] // ROW_BLOCK).astype(jnp.int32).reshape(1)
    pos = slot.reshape(n // tc, tc, TOP_K).transpose(0, 2, 1).reshape(n // tc, 1, TOP_K * tc)
    return row_tok, block_expert, n_valid, pos, n_blocks


def _layer(x, norm1_g, w_in, mix_mu, w0, w_decay_up, a0, w_iclr_up, w_gate_up, k_k, k_a, r_k, lnx_g, lnx_b, b_qkv,
           sinks, w_up_rwkv, w_up_attn, w_out, norm2_g, w_router, b_router, w1, b1, w2, b2, normf_g):
    b, s, d = x.shape
    n = b * s
    c = w_up_rwkv.shape[0]
    lora = w_decay_up.shape[0]
    qc = w_up_attn.shape[0]
    kvc = KV_HEADS * HEAD
    rwkv_cols = 3 * c + 2 * lora + w_gate_up.shape[0]
    qkv_cols = qc + 2 * kvc
    row = lambda t: t.reshape(1, -1).astype(F32)
    xf = x.reshape(n, d)
    g1 = row(norm1_g)
    w_in_b = w_in.astype(BF16)

    z_rwkv = _norm_proj(xf, g1, w_in_b[:, :rwkv_cols], jnp.zeros((1, rwkv_cols), F32), F32, 512)
    qkv = _norm_proj(xf, g1, w_in_b[:, rwkv_cols:rwkv_cols + qkv_cols], row(b_qkv), BF16, 512)
    gates = _norm_proj(xf, g1, w_in_b[:, rwkv_cols + qkv_cols:], jnp.zeros((1, 2 * d), F32), BF16, 512)

    zl = jnp.zeros((lora, c), F32)
    w_lora = jnp.concatenate([jnp.concatenate([w_decay_up, zl], axis=1), jnp.concatenate([zl, w_iclr_up], axis=1)],
                             axis=0)
    r, lw, kf, v, kn, ba, g = _rwkv_prep(z_rwkv, s, c, row(mix_mu), w_lora, w_gate_up, row(w0), row(a0), row(k_k),
                                         row(k_a), 256)
    as3 = lambda t: t.reshape(b, s, c)
    y_rwkv = _rwkv_scan(as3(r), as3(lw), as3(kf), as3(v), as3(kn), as3(ba), as3(g), row(r_k), row(lnx_g),
                        row(lnx_b)).reshape(n, c)

    q = qkv[:, :qc].reshape(b, s, qc)
    ka = qkv[:, qc:qc + kvc].reshape(b, s, kvc)
    va = qkv[:, qc + kvc:].reshape(b, s, kvc)
    sinks_b = jnp.broadcast_to(sinks.astype(F32).reshape(-1, 1), (sinks.shape[0], 128))
    y_attn = _attention(q, ka, va, sinks_b).reshape(n, qc)

    x1, h2, gate, top_idx = _merge(xf, y_rwkv, y_attn, gates, w_up_rwkv.astype(BF16), w_up_attn.astype(BF16),
                                   w_out.astype(BF16), row(norm2_g), w_router, row(b_router), 256)

    tc = 256
    row_tok, block_expert, n_valid, pos, n_blocks = _route(top_idx[:, :TOP_K], w_router.shape[1], tc)
    xs = _dispatch(h2, row_tok, n_valid, n_blocks)
    ys = _expert_ffn(xs, w1, b1, w2, b2, block_expert, n_valid, n_blocks)
    out = _combine(x1, gate, pos, ys, row(normf_g), tc)
    return out.reshape(b, s, d)


def kernel(x, norm1_g, w_in, mix_mu, w0, w_decay_up, a0, w_iclr_up, w_gate_up, k_k, k_a, r_k, lnx_g, lnx_b, b_qkv,
           sinks, w_up_rwkv, w_up_attn, w_out, norm2_g, w_router, b_router, w1, b1, w2, b2, normf_g):
    assert w_in.shape[0] == 1, "single-layer block"
    return _layer(x, norm1_g[0], w_in[0], mix_mu[0], w0[0], w_decay_up[0], a0[0], w_iclr_up[0], w_gate_up[0],
                  k_k[0], k_a[0], r_k[0], lnx_g[0], lnx_b[0], b_qkv[0], sinks[0], w_up_rwkv[0], w_up_attn[0],
                  w_out[0], norm2_g[0], w_router[0], b_router[0], w1[0], b1[0], w2[0], b2[0], normf_g)
```

```python
import functools

import jax
import jax.numpy as jnp
import numpy as np
from jax import lax
from jax.experimental import pallas as pl
from jax.experimental.pallas import tpu as pltpu

F32 = jnp.float32
BF16 = jnp.bfloat16

NORM_EPS = 1e-5
LNX_EPS = 64e-5
HEAD = 64
WINDOW = 128
KV_HEADS = 4
TOP_K = 4
SWIGLU_LIMIT = 7.0
SWIGLU_ALPHA = 1.702
DECAY_SCALE = float(np.exp(-0.5))

CHUNK = 64
ROW_BLOCK = 1024
FF_TILE = 256
VMEM_LIMIT = 56 * 1024 * 1024
HI = lax.Precision.HIGHEST


def _cparams(sem):
    return pltpu.CompilerParams(dimension_semantics=sem, vmem_limit_bytes=VMEM_LIMIT)


def _resident(shape):
    nd = len(shape)
    return pl.BlockSpec(shape, lambda *_: (0,) * nd, pipeline_mode=pl.Buffered(1))


def _rmsnorm(x, g):
    return x * lax.rsqrt(jnp.mean(x * x, axis=-1, keepdims=True) + NORM_EPS) * g


def _sigmoid(x):
    return 1.0 / (1.0 + jnp.exp(-x))


def _norm_proj_kernel(x_ref, g_ref, w_ref, b_ref, o_ref):
    h = _rmsnorm(x_ref[...], g_ref[...]).astype(BF16)
    z = jnp.dot(h, w_ref[...], preferred_element_type=F32) + b_ref[...]
    o_ref[...] = z.astype(o_ref.dtype)


def _norm_proj(x, g, w, b, out_dtype, tm):
    n, d = x.shape
    cols = w.shape[1]
    return pl.pallas_call(
        _norm_proj_kernel,
        grid=(n // tm,),
        in_specs=[pl.BlockSpec((tm, d), lambda i: (i, 0)), _resident((1, d)), _resident((d, cols)),
                  _resident((1, cols))],
        out_specs=pl.BlockSpec((tm, cols), lambda i: (i, 0)),
        out_shape=jax.ShapeDtypeStruct((n, cols), out_dtype),
        compiler_params=_cparams(("parallel",)),
        name="norm_proj",
    )(x, g, w, b)


def _head_sums(x):
    rows, c = x.shape
    lane = lax.broadcasted_iota(jnp.int32, (rows, 128), 1)
    low = lane < HEAD
    parts = []
    for gi in range(c // 128):
        xg = x[:, gi * 128:(gi + 1) * 128]
        s_lo = jnp.sum(jnp.where(low, xg, 0.0), axis=-1, keepdims=True)
        s_hi = jnp.sum(jnp.where(low, 0.0, xg), axis=-1, keepdims=True)
        parts.append(jnp.where(low, s_lo, s_hi))
    return jnp.concatenate(parts, axis=-1)


def _prep_kernel(seq_blocks, c, z_ref, zp_ref, mu_ref, wlora_ref, wgate_ref, w0_ref, a0_ref, kk_ref, ka_ref,
                 r_ref, lw_ref, kf_ref, v_ref, kn_ref, ba_ref, g_ref):
    i = pl.program_id(0)
    z = z_ref[...]
    tm = z.shape[0]
    prev = jnp.where(i % seq_blocks == 0, 0.0, zp_ref[7:8, :])
    row = lax.broadcasted_iota(jnp.int32, z.shape, 0)
    shifted = jnp.where(row == 0, prev, pltpu.roll(z, 1, 0))
    zs = z + (shifted - z) * mu_ref[...]
    r = zs[:, 0:c]
    k = zs[:, c:2 * c]
    v = zs[:, 2 * c:3 * c]
    zwa = zs[:, 3 * c:3 * c + 128]
    zg = zs[:, 3 * c + 128:3 * c + 256]
    lane = lax.broadcasted_iota(jnp.int32, (tm, 128), 1)
    lora_in = jnp.where(lane < 64, jnp.tanh(zwa), zwa)
    up = jnp.dot(lora_in, wlora_ref[...], preferred_element_type=F32, precision=HI)
    u = w0_ref[...] + up[:, 0:c]
    a = _sigmoid(a0_ref[...] + up[:, c:2 * c])
    g = jnp.dot(_sigmoid(zg), wgate_ref[...], preferred_element_type=F32, precision=HI)
    lw = -DECAY_SCALE * _sigmoid(u)
    kk = k * kk_ref[...]
    kn = kk / jnp.maximum(jnp.sqrt(_head_sums(kk * kk)), 1e-12)
    kf = k * (1.0 + (a - 1.0) * ka_ref[...])
    r_ref[...] = r
    lw_ref[...] = lw
    kf_ref[...] = kf
    v_ref[...] = v
    kn_ref[...] = kn
    ba_ref[...] = kn * a
    g_ref[...] = g


def _rwkv_prep(z, seq, c, mix_mu, w_lora, w_gate, w0, a0, k_k, k_a, tm):
    n, zc = z.shape
    row_spec = pl.BlockSpec((tm, c), lambda i: (i, 0))
    out = jax.ShapeDtypeStruct((n, c), F32)
    return pl.pallas_call(
        functools.partial(_prep_kernel, seq // tm, c),
        grid=(n // tm,),
        in_specs=[pl.BlockSpec((tm, zc), lambda i: (i, 0)),
                  pl.BlockSpec((8, zc), lambda i: (jnp.maximum(i * (tm // 8) - 1, 0), 0)),
                  _resident((1, zc)), _resident((128, 2 * c)), _resident((128, c)),
                  _resident((1, c)), _resident((1, c)), _resident((1, c)), _resident((1, c))],
        out_specs=[row_spec] * 7,
        out_shape=[out] * 7,
        compiler_params=_cparams(("parallel",)),
        name="rwkv_prep",
    )(z, z, mix_mu, w_lora, w_gate, w0, a0, k_k, k_a)


def _split3(x):
    h1 = x.astype(BF16)
    r1 = x - h1.astype(F32)
    h2 = r1.astype(BF16)
    h3 = (r1 - h2.astype(F32)).astype(BF16)
    return h1, h2, h3


def _mm(a, b):
    return jnp.dot(a, b, preferred_element_type=F32)


def _mm_nt(a, b):
    return lax.dot_general(a, b, (((1,), (1,)), ((), ())), preferred_element_type=F32)


def _mm_tn(a, b):
    return lax.dot_general(a, b, (((0,), (0,)), ((), ())), preferred_element_type=F32)


GROUP_HEADS = 4
GROUP_LANES = GROUP_HEADS * HEAD


def _block_diag(x, mask):
    return jnp.concatenate([x.astype(BF16)] * GROUP_HEADS, axis=0) * mask


def _scan_kernel(r_ref, lw_ref, kf_ref, v_ref, kn_ref, ba_ref, g_ref, rk_ref, lng_ref, lnb_ref, mask_ref, y_ref,
                 s_ref):
    @pl.when(pl.program_id(0) == 0)
    def _():
        s_ref[...] = jnp.zeros_like(s_ref)

    t = CHUNK
    nb = r_ref.shape[0]
    c = r_ref.shape[2]
    gl = GROUP_LANES
    n_groups = c // gl
    mask = mask_ref[...]
    ti = lax.broadcasted_iota(jnp.int32, (t, t), 0)
    si = lax.broadcasted_iota(jnp.int32, (t, t), 1)
    tri = jnp.where(si <= ti, 1.0, 0.0).astype(BF16)
    row = lax.broadcasted_iota(jnp.int32, (t, gl), 0)
    col = lax.broadcasted_iota(jnp.int32, (t, gl), 1) & (HEAD - 1)
    strict = col < row
    incl = col <= row
    eye = jnp.where(col == row, 1.0, 0.0)

    prep = []
    for b in range(nb):
        lw = lw_ref[b]
        cum = sum(jnp.dot(tri, part, preferred_element_type=F32) for part in _split3(lw))
        cum_end = cum[t - 1:t, :]
        e_pos = jnp.exp(cum)
        e_neg = jnp.exp(-cum)
        e_end = jnp.exp(cum_end - cum)
        r = r_ref[b]
        kf = kf_ref[b]
        ba = ba_ref[b]
        prep.append(dict(
            rt=r * e_pos,
            at=-kn_ref[b] * jnp.exp(cum - lw),
            bt=ba * e_neg,
            kt=kf * e_neg,
            bh=ba * e_end,
            kh=kf * e_end,
            v=v_ref[b],
            w_end=jnp.exp(cum_end),
            rkf=r * kf * rk_ref[...]))

    probs = [(b, gi) for b in range(nb) for gi in range(n_groups)]

    def part(b, gi, name):
        return prep[b][name][:, gi * gl:(gi + 1) * gl]

    a_ab, a_ak, a_rb, a_rk = [], [], [], []
    for b, gi in probs:
        lhs = jnp.concatenate([part(b, gi, "at"), part(b, gi, "rt")], axis=0).astype(BF16)
        rhs = jnp.concatenate([_block_diag(part(b, gi, "bt"), mask), _block_diag(part(b, gi, "kt"), mask)], axis=0)
        amat = _mm_nt(lhs, rhs)
        a_ab.append(jnp.where(strict, amat[0:t, 0:gl], 0.0))
        a_ak.append(jnp.where(strict, amat[0:t, gl:2 * gl], 0.0))
        a_rb.append(jnp.where(incl, amat[t:2 * t, 0:gl], 0.0))
        a_rk.append(jnp.where(incl, amat[t:2 * t, gl:2 * gl], 0.0))

    vbd = [_block_diag(part(b, gi, "v"), mask) for b, gi in probs]
    av = [_mm(a_ak[i].astype(BF16), vbd[i]) for i in range(len(probs))]
    minv = [eye + a for a in a_ab]
    power = [_mm(a.astype(BF16), _block_diag(a, mask)) for a in a_ab]
    span = 2
    while span < t:
        last = span * 2 >= t
        for i in range(len(probs)):
            pbd = _block_diag(power[i], mask)
            if last:
                minv[i] = minv[i] + _mm(minv[i].astype(BF16), pbd)
            else:
                both = _mm(jnp.concatenate([power[i], minv[i]], axis=0).astype(BF16), pbd)
                power[i] = both[0:t]
                minv[i] = minv[i] + both[t:2 * t]
        span *= 2

    s0 = [s_ref[b, gi] for b, gi in probs]
    sbd = [_block_diag(s, mask) for s in s0]
    ps = []
    for i, (b, gi) in enumerate(probs):
        lhs = jnp.concatenate([part(b, gi, "at"), part(b, gi, "rt")], axis=0).astype(BF16)
        ps.append(_mm_nt(lhs, sbd[i]))
    u = [_mm(minv[i].astype(BF16), _block_diag(ps[i][0:t] + av[i], mask)) for i in range(len(probs))]
    ys = []
    for i, (b, gi) in enumerate(probs):
        lhs = jnp.concatenate([a_rb[i], a_rk[i]], axis=1).astype(BF16)
        rhs = jnp.concatenate([_block_diag(u[i], mask), vbd[i]], axis=0)
        ys.append(ps[i][t:2 * t] + _mm(lhs, rhs))
        uv = jnp.concatenate([u[i], part(b, gi, "v")], axis=0).astype(BF16)
        bk = jnp.concatenate([part(b, gi, "bh"), part(b, gi, "kh")], axis=0).astype(BF16)
        full = _mm_tn(uv, bk) * mask.astype(F32)
        upd = full[0:HEAD]
        for hh in range(1, GROUP_HEADS):
            upd = upd + full[hh * HEAD:(hh + 1) * HEAD]
        s_ref[b, gi] = s0[i] * part(b, gi, "w_end") + upd

    for b in range(nb):
        y = jnp.concatenate([ys[b * n_groups + gi] for gi in range(n_groups)], axis=1)
        mu = _head_sums(y) * (1.0 / HEAD)
        yc = y - mu
        var = _head_sums(yc * yc) * (1.0 / HEAD)
        yn = yc * lax.rsqrt(var + LNX_EPS)
        bonus = _head_sums(prep[b]["rkf"]) * prep[b]["v"]
        y_ref[b] = ((yn * lng_ref[...] + lnb_ref[...] + bonus) * g_ref[b]).astype(y_ref.dtype)


def _rwkv_scan(r, lw, kf, v, kn, ba, g, r_k, lnx_g, lnx_b):
    b, s, c = r.shape
    blk = pl.BlockSpec((b, CHUNK, c), lambda ci: (0, ci, 0))
    hid = np.arange(GROUP_LANES) // HEAD
    mask = jnp.asarray(hid[:, None] == hid[None, :], BF16)
    return pl.pallas_call(
        _scan_kernel,
        grid=(s // CHUNK,),
        in_specs=[blk] * 7 + [_resident((1, c))] * 3 + [_resident((GROUP_LANES, GROUP_LANES))],
        out_specs=blk,
        out_shape=jax.ShapeDtypeStruct((b, s, c), BF16),
        scratch_shapes=[pltpu.VMEM((b, c // GROUP_LANES, HEAD, GROUP_LANES), F32)],
        compiler_params=_cparams(("arbitrary",)),
        name="rwkv_scan",
    )(r, lw, kf, v, kn, ba, g, r_k, lnx_g, lnx_b, mask)


def _attn_kernel(q_ref, kc_ref, kp_ref, vc_ref, vp_ref, sink_ref, o_ref):
    first = pl.program_id(1) == 0
    w = WINDOW
    group = q_ref.shape[-1] // HEAD // KV_HEADS
    row = lax.broadcasted_iota(jnp.int32, (group * w, 2 * w), 0)
    qi = row & (w - 1)
    kj = lax.broadcasted_iota(jnp.int32, (group * w, 2 * w), 1)
    lo = jnp.where(first, jnp.maximum(qi, w - 1), qi)
    valid = (kj > lo) & (kj <= qi + w)
    grow = lax.broadcasted_iota(jnp.int32, (group * w, 1), 0) // w
    q = q_ref[...]
    outs = []
    for hk in range(KV_HEADS):
        ksl = slice(hk * HEAD, (hk + 1) * HEAD)
        kcat = jnp.concatenate([kp_ref[:, ksl], kc_ref[:, ksl]], axis=0)
        vcat = jnp.concatenate([vp_ref[:, ksl], vc_ref[:, ksl]], axis=0)
        qg = jnp.concatenate([q[:, (hk * group + gi) * HEAD:(hk * group + gi + 1) * HEAD] for gi in range(group)],
                             axis=0)
        sink = jnp.zeros((group * w, 1), F32)
        for gi in range(group):
            sink = jnp.where(grow == gi, sink_ref[hk * group + gi:hk * group + gi + 1, 0:1], sink)
        s = _mm_nt(qg, kcat) * (HEAD ** -0.5)
        s = jnp.where(valid, s, -1e30)
        m = jnp.maximum(jnp.max(s, axis=-1, keepdims=True), sink)
        p = jnp.exp(s - m)
        denom = jnp.sum(p, axis=-1, keepdims=True) + jnp.exp(sink - m)
        o = _mm(p.astype(BF16), vcat) / denom
        for gi in range(group):
            outs.append(o[gi * w:(gi + 1) * w, :])
    o_ref[...] = jnp.concatenate(outs, axis=-1).astype(o_ref.dtype)


def _attention(q, k, v, sinks_b):
    b, s, qc = q.shape
    kc = k.shape[-1]
    nb = s // WINDOW
    cur = lambda bi, i: (bi, i, 0)
    prev = lambda bi, i: (bi, jnp.maximum(i - 1, 0), 0)
    return pl.pallas_call(
        _attn_kernel,
        grid=(b, nb),
        in_specs=[pl.BlockSpec((None, WINDOW, qc), cur),
                  pl.BlockSpec((None, WINDOW, kc), cur), pl.BlockSpec((None, WINDOW, kc), prev),
                  pl.BlockSpec((None, WINDOW, kc), cur), pl.BlockSpec((None, WINDOW, kc), prev),
                  _resident(sinks_b.shape)],
        out_specs=pl.BlockSpec((None, WINDOW, qc), cur),
        out_shape=jax.ShapeDtypeStruct((b, s, qc), BF16),
        compiler_params=_cparams(("parallel", "parallel")),
        name="swa_attention",
    )(q, k, k, v, v, sinks_b)


def _merge_kernel(n_exp, x_ref, yr_ref, ya_ref, gr_ref, ga_ref, wur_ref, wua_ref, wo_ref, n2_ref, wr_ref, br_ref,
                  x1_ref, h2_ref, gate_ref, idx_ref):
    up_r = _mm(yr_ref[...], wur_ref[...])
    up_a = _mm(ya_ref[...], wua_ref[...])
    merged = _sigmoid(gr_ref[...].astype(F32)) * up_r + _sigmoid(ga_ref[...].astype(F32)) * up_a
    x1 = x_ref[...] + _mm(merged.astype(BF16), wo_ref[...])
    x1_ref[...] = x1
    h2 = _rmsnorm(x1, n2_ref[...])
    h2_ref[...] = h2
    logits = jnp.dot(h2, wr_ref[...], preferred_element_type=F32, precision=HI) + br_ref[...]
    tm = logits.shape[0]
    lane_e = lax.broadcasted_iota(jnp.int32, logits.shape, 1).astype(F32)
    lane_o = lax.broadcasted_iota(jnp.int32, (tm, 128), 1)
    vals, gate_o, idx_o = [], jnp.zeros((tm, 128), F32), jnp.zeros((tm, 128), F32)
    idxs = []
    for _ in range(TOP_K):
        mx = jnp.max(logits, axis=-1, keepdims=True)
        ix = jnp.min(jnp.where(logits == mx, lane_e, float(n_exp)), axis=-1, keepdims=True)
        vals.append(mx)
        idxs.append(ix)
        logits = jnp.where(lane_e == ix, -jnp.inf, logits)
    exps = [jnp.exp(vk - vals[0]) for vk in vals]
    tot = exps[0] + exps[1] + exps[2] + exps[3]
    for kk in range(TOP_K):
        gate_o = jnp.where(lane_o == kk, exps[kk] / tot, gate_o)
        idx_o = jnp.where(lane_o == kk, idxs[kk], idx_o)
    gate_ref[...] = gate_o
    idx_ref[...] = idx_o.astype(jnp.int32)


def _merge(x, y_rwkv, y_attn, gates, w_up_r, w_up_a, w_out, norm2_g, w_router, b_router, tm):
    n, d = x.shape
    c = y_rwkv.shape[1]
    n_exp = w_router.shape[1]
    row = lambda cols, j=0: pl.BlockSpec((tm, cols), lambda i: (i, j))
    return pl.pallas_call(
        functools.partial(_merge_kernel, n_exp),
        grid=(n // tm,),
        in_specs=[row(d), row(c), row(c), row(d, 0), row(d, 1),
                  _resident((c, d)), _resident((c, d)), _resident((d, d)), _resident((1, d)),
                  _resident((d, n_exp)), _resident((1, n_exp))],
        out_specs=[row(d), row(d), row(128), row(128)],
        out_shape=[jax.ShapeDtypeStruct((n, d), F32), jax.ShapeDtypeStruct((n, d), F32),
                   jax.ShapeDtypeStruct((n, 128), F32), jax.ShapeDtypeStruct((n, 128), jnp.int32)],
        compiler_params=_cparams(("parallel",)),
        name="merge_router",
    )(x, y_rwkv, y_attn, gates, gates, w_up_r, w_up_a, w_out, norm2_g, w_router, b_router)


GATHER_UNROLL = 8


def _gather_rows(src_hbm, idx_ref, buf, sem):
    rows = buf.shape[0]

    def start(blk, carry):
        for u in range(GATHER_UNROLL):
            r = blk * GATHER_UNROLL + u
            pltpu.make_async_copy(src_hbm.at[pl.ds(idx_ref[0, 0, r], 1)], buf.at[pl.ds(r, 1)], sem).start()
        return carry

    lax.fori_loop(0, rows // GATHER_UNROLL, start, 0)
    pltpu.make_async_copy(buf, buf, sem).wait()


def _dispatch_kernel(nvalid_ref, tok_ref, h_hbm, o_ref, buf, sem):
    i = pl.program_id(0)

    @pl.when(i < nvalid_ref[0])
    def _():
        _gather_rows(h_hbm, tok_ref, buf, sem)
        o_ref[...] = buf[...].astype(o_ref.dtype)

    @pl.when(i >= nvalid_ref[0])
    def _():
        o_ref[...] = jnp.zeros_like(o_ref)


def _dispatch(h2, row_tok, n_valid, n_blocks):
    d = h2.shape[1]
    grid_spec = pltpu.PrefetchScalarGridSpec(
        num_scalar_prefetch=1,
        grid=(n_blocks,),
        in_specs=[pl.BlockSpec((1, 1, ROW_BLOCK), lambda i, nv: (i, 0, 0), memory_space=pltpu.SMEM),
                  pl.BlockSpec(memory_space=pl.ANY)],
        out_specs=pl.BlockSpec((ROW_BLOCK, d), lambda i, nv: (i, 0)),
        scratch_shapes=[pltpu.VMEM((ROW_BLOCK, d), F32), pltpu.SemaphoreType.DMA(())],
    )
    return pl.pallas_call(
        _dispatch_kernel,
        grid_spec=grid_spec,
        out_shape=jax.ShapeDtypeStruct((n_blocks * ROW_BLOCK, d), BF16),
        compiler_params=_cparams(("arbitrary",)),
        name="moe_dispatch",
    )(n_valid, row_tok.reshape(n_blocks, 1, ROW_BLOCK), h2)


def _ffn_kernel(nvalid_ref, bexp_ref, xs_ref, w1g_ref, w1l_ref, b1g_ref, b1l_ref, w2_ref, b2_ref, o_ref):
    i = pl.program_id(0)
    j = pl.program_id(1)

    @pl.when(i < nvalid_ref[0])
    def _():
        x = xs_ref[...]
        hg = _mm(x, w1g_ref[...].astype(BF16)) + b1g_ref[...]
        hl = _mm(x, w1l_ref[...].astype(BF16)) + b1l_ref[...]
        glu = jnp.minimum(hg, SWIGLU_LIMIT)
        lin = jnp.clip(hl, -SWIGLU_LIMIT, SWIGLU_LIMIT)
        act = glu * _sigmoid(SWIGLU_ALPHA * glu) * (lin + 1.0)
        part = _mm(act.astype(BF16), w2_ref[...].astype(BF16))

        @pl.when(j == 0)
        def _():
            o_ref[...] = part + b2_ref[...]

        @pl.when(j > 0)
        def _():
            o_ref[...] += part

    @pl.when((i >= nvalid_ref[0]) & (j == 0))
    def _():
        o_ref[...] = jnp.zeros_like(o_ref)


def _expert_ffn(xs, w1, b1, w2, b2, block_expert, n_valid, n_blocks):
    d = xs.shape[1]
    n_exp, _, ff2 = w1.shape
    ff = ff2 // 2
    n_ff = ff // FF_TILE

    def jj(i, j, nv):
        return jnp.where(i < nv[0], j, n_ff - 1)

    grid_spec = pltpu.PrefetchScalarGridSpec(
        num_scalar_prefetch=2,
        grid=(n_blocks, n_ff),
        in_specs=[pl.BlockSpec((ROW_BLOCK, d), lambda i, j, nv, be: (i, 0)),
                  pl.BlockSpec((None, d, FF_TILE), lambda i, j, nv, be: (be[i], 0, jj(i, j, nv))),
                  pl.BlockSpec((None, d, FF_TILE), lambda i, j, nv, be: (be[i], 0, n_ff + jj(i, j, nv))),
                  pl.BlockSpec((None, 1, FF_TILE), lambda i, j, nv, be: (be[i], 0, jj(i, j, nv))),
                  pl.BlockSpec((None, 1, FF_TILE), lambda i, j, nv, be: (be[i], 0, n_ff + jj(i, j, nv))),
                  pl.BlockSpec((None, FF_TILE, d), lambda i, j, nv, be: (be[i], jj(i, j, nv), 0)),
                  pl.BlockSpec((None, 1, d), lambda i, j, nv, be: (be[i], 0, 0))],
        out_specs=pl.BlockSpec((ROW_BLOCK, d), lambda i, j, nv, be: (i, 0)),
    )
    return pl.pallas_call(
        _ffn_kernel,
        grid_spec=grid_spec,
        out_shape=jax.ShapeDtypeStruct((n_blocks * ROW_BLOCK, d), F32),
        compiler_params=_cparams(("arbitrary", "arbitrary")),
        name="moe_ffn",
    )(n_valid, block_expert, xs, w1, w1, b1.reshape(n_exp, 1, ff2), b1.reshape(n_exp, 1, ff2), w2,
      b2.reshape(n_exp, 1, d))


def _combine_kernel(pos_ref, x1_ref, gate_ref, nf_ref, ys_hbm, o_ref, buf, sem):
    tc = x1_ref.shape[0]
    _gather_rows(ys_hbm, pos_ref, buf, sem)
    acc = x1_ref[...]
    gate = gate_ref[...]
    for k in range(TOP_K):
        acc = acc + gate[:, k:k + 1] * buf[k * tc:(k + 1) * tc, :]
    o_ref[...] = _rmsnorm(acc, nf_ref[...])


def _combine(x1, gate, pos, ys, normf_g, tc):
    n, d = x1.shape
    nt = n // tc
    return pl.pallas_call(
        _combine_kernel,
        grid=(nt,),
        in_specs=[pl.BlockSpec((1, 1, TOP_K * tc), lambda i: (i, 0, 0), memory_space=pltpu.SMEM),
                  pl.BlockSpec((tc, d), lambda i: (i, 0)), pl.BlockSpec((tc, 128), lambda i: (i, 0)),
                  _resident((1, d)), pl.BlockSpec(memory_space=pl.ANY)],
        out_specs=pl.BlockSpec((tc, d), lambda i: (i, 0)),
        out_shape=jax.ShapeDtypeStruct((n, d), F32),
        scratch_shapes=[pltpu.VMEM((TOP_K * tc, d), F32), pltpu.SemaphoreType.DMA(())],
        compiler_params=_cparams(("arbitrary",)),
        name="moe_combine",
    )(pos, x1, gate, normf_g, ys)


def _route(top_idx, n_exp, tc):
    n = top_idx.shape[0]
    n_assign = n * TOP_K
    n_blocks = n_assign // ROW_BLOCK + n_exp
    tile = min(1024, n)
    hot = (top_idx[:, :, None] == jnp.arange(n_exp, dtype=jnp.int32)).any(axis=1).reshape(n // tile, tile, n_exp)
    earlier = jnp.asarray(np.tril(np.ones((tile, tile), np.float32), -1), BF16)
    within = jnp.einsum("ts,bse->bte", earlier, hot.astype(BF16), preferred_element_type=F32)
    tile_tot = jnp.sum(hot, axis=1, dtype=jnp.int32)
    tile_off = jnp.cumsum(tile_tot, axis=0) - tile_tot
    rank = (within.astype(jnp.int32) + tile_off[:, None, :]).reshape(n, n_exp)
    counts = jnp.sum(tile_tot, axis=0)
    padded = (counts + ROW_BLOCK - 1) // ROW_BLOCK * ROW_BLOCK
    pad_end = jnp.cumsum(padded)
    pad_start = pad_end - padded
    grp_start = jnp.cumsum(counts) - counts
    slot = jnp.take_along_axis(rank + pad_start[None, :], top_idx, axis=1).reshape(-1)
    block_start = jnp.arange(n_blocks, dtype=jnp.int32) * ROW_BLOCK
    block_expert = jnp.minimum(jnp.searchsorted(pad_end, block_start, side="right"), n_exp - 1).astype(jnp.int32)
    n_valid = (pad_end[-1] // ROW_BLOCK).astype(jnp.int32).reshape(1)
    order = jnp.argsort(top_idx.reshape(-1), stable=True)
    row_e = jnp.repeat(block_expert, ROW_BLOCK)
    row_r = jnp.arange(n_blocks * ROW_BLOCK, dtype=jnp.int32) - pad_start[row_e]
    src = jnp.clip(grp_start[row_e] + row_r, 0, n_assign - 1)
    row_tok = jnp.where((row_r >= 0) & (row_r < counts[row_e]), order[src] // TOP_K, 0).astype(jnp.int32)
    pos = slot.reshape(n // tc, tc, TOP_K).transpose(0, 2, 1).reshape(n // tc, 1, TOP_K * tc)
    return row_tok, block_expert, n_valid, pos, n_blocks


def _layer(x, norm1_g, w_in, mix_mu, w0, w_decay_up, a0, w_iclr_up, w_gate_up, k_k, k_a, r_k, lnx_g, lnx_b, b_qkv,
           sinks, w_up_rwkv, w_up_attn, w_out, norm2_g, w_router, b_router, w1, b1, w2, b2, normf_g):
    b, s, d = x.shape
    n = b * s
    c = w_up_rwkv.shape[0]
    lora = w_decay_up.shape[0]
    qc = w_up_attn.shape[0]
    kvc = KV_HEADS * HEAD
    rwkv_cols = 3 * c + 2 * lora + w_gate_up.shape[0]
    qkv_cols = qc + 2 * kvc
    row = lambda t: t.reshape(1, -1).astype(F32)
    xf = x.reshape(n, d)
    g1 = row(norm1_g)
    w_in_b = w_in.astype(BF16)

    z_rwkv = _norm_proj(xf, g1, w_in_b[:, :rwkv_cols], jnp.zeros((1, rwkv_cols), F32), F32, 512)
    qkv = _norm_proj(xf, g1, w_in_b[:, rwkv_cols:rwkv_cols + qkv_cols], row(b_qkv), BF16, 512)
    gates = _norm_proj(xf, g1, w_in_b[:, rwkv_cols + qkv_cols:], jnp.zeros((1, 2 * d), F32), BF16, 512)

    zl = jnp.zeros((lora, c), F32)
    w_lora = jnp.concatenate([jnp.concatenate([w_decay_up, zl], axis=1), jnp.concatenate([zl, w_iclr_up], axis=1)],
                             axis=0)
    r, lw, kf, v, kn, ba, g = _rwkv_prep(z_rwkv, s, c, row(mix_mu), w_lora, w_gate_up, row(w0), row(a0), row(k_k),
                                         row(k_a), 256)
    as3 = lambda t: t.reshape(b, s, c)
    y_rwkv = _rwkv_scan(as3(r), as3(lw), as3(kf), as3(v), as3(kn), as3(ba), as3(g), row(r_k), row(lnx_g),
                        row(lnx_b)).reshape(n, c)

    q = qkv[:, :qc].reshape(b, s, qc)
    ka = qkv[:, qc:qc + kvc].reshape(b, s, kvc)
    va = qkv[:, qc + kvc:].reshape(b, s, kvc)
    sinks_b = jnp.broadcast_to(sinks.astype(F32).reshape(-1, 1), (sinks.shape[0], 128))
    y_attn = _attention(q, ka, va, sinks_b).reshape(n, qc)

    x1, h2, gate, top_idx = _merge(xf, y_rwkv, y_attn, gates, w_up_rwkv.astype(BF16), w_up_attn.astype(BF16),
                                   w_out.astype(BF16), row(norm2_g), w_router, row(b_router), 256)

    tc = 256
    row_tok, block_expert, n_valid, pos, n_blocks = _route(top_idx[:, :TOP_K], w_router.shape[1], tc)
    xs = _dispatch(h2, row_tok, n_valid, n_blocks)
    ys = _expert_ffn(xs, w1, b1, w2, b2, block_expert, n_valid, n_blocks)
    out = _combine(x1, gate, pos, ys, row(normf_g), tc)
    return out.reshape(b, s, d)


def kernel(x, norm1_g, w_in, mix_mu, w0, w_decay_up, a0, w_iclr_up, w_gate_up, k_k, k_a, r_k, lnx_g, lnx_b, b_qkv,
           sinks, w_up_rwkv, w_up_attn, w_out, norm2_g, w_router, b_router, w1, b1, w2, b2, normf_g):
    assert w_in.shape[0] == 1, "single-layer block"
    return _layer(x, norm1_g[0], w_in[0], mix_mu[0], w0[0], w_decay_up[0], a0[0], w_iclr_up[0], w_gate_up[0],
                  k_k[0], k_a[0], r_k[0], lnx_g[0], lnx_b[0], b_qkv[0], sinks[0], w_up_rwkv[0], w_up_attn[0],
                  w_out[0], norm2_g[0], w_router[0], b_router[0], w1[0], b1[0], w2[0], b2[0], normf_g)
```

```python
import functools

import jax
import jax.numpy as jnp
import numpy as np
from jax import lax
from jax.experimental import pallas as pl
from jax.experimental.pallas import tpu as pltpu

F32 = jnp.float32
BF16 = jnp.bfloat16

NORM_EPS = 1e-5
LNX_EPS = 64e-5
HEAD = 64
WINDOW = 128
KV_HEADS = 4
TOP_K = 4
SWIGLU_LIMIT = 7.0
SWIGLU_ALPHA = 1.702
DECAY_SCALE = float(np.exp(-0.5))

CHUNK = 64
ROW_BLOCK = 1024
ROW_SUB = 256
FF_TILE = 256
OUT_TILE = 256
VMEM_LIMIT = 56 * 1024 * 1024
HI = lax.Precision.HIGHEST


def _cparams(sem):
    return pltpu.CompilerParams(dimension_semantics=sem, vmem_limit_bytes=VMEM_LIMIT)


def _resident(shape):
    nd = len(shape)
    return pl.BlockSpec(shape, lambda *_: (0,) * nd, pipeline_mode=pl.Buffered(1))


def _rmsnorm(x, g):
    return x * lax.rsqrt(jnp.mean(x * x, axis=-1, keepdims=True) + NORM_EPS) * g


def _sigmoid(x):
    return 1.0 / (1.0 + jnp.exp(-x))


def _norm_proj_kernel(x_ref, g_ref, w_ref, b_ref, o_ref):
    h = _rmsnorm(x_ref[...], g_ref[...]).astype(BF16)
    z = jnp.dot(h, w_ref[...], preferred_element_type=F32) + b_ref[...]
    o_ref[...] = z.astype(o_ref.dtype)


def _norm_proj(x, g, w, b, out_dtype, tm):
    n, d = x.shape
    cols = w.shape[1]
    return pl.pallas_call(
        _norm_proj_kernel,
        grid=(n // tm,),
        in_specs=[pl.BlockSpec((tm, d), lambda i: (i, 0)), _resident((1, d)), _resident((d, cols)),
                  _resident((1, cols))],
        out_specs=pl.BlockSpec((tm, cols), lambda i: (i, 0)),
        out_shape=jax.ShapeDtypeStruct((n, cols), out_dtype),
        compiler_params=_cparams(("parallel",)),
        name="norm_proj",
    )(x, g, w, b)


def _head_sums(x):
    rows, c = x.shape
    lane = lax.broadcasted_iota(jnp.int32, (rows, 128), 1)
    low = lane < HEAD
    parts = []
    for gi in range(c // 128):
        xg = x[:, gi * 128:(gi + 1) * 128]
        s_lo = jnp.sum(jnp.where(low, xg, 0.0), axis=-1, keepdims=True)
        s_hi = jnp.sum(jnp.where(low, 0.0, xg), axis=-1, keepdims=True)
        parts.append(jnp.where(low, s_lo, s_hi))
    return jnp.concatenate(parts, axis=-1)


def _prep_kernel(seq_blocks, c, z_ref, zp_ref, mu_ref, wlora_ref, wgate_ref, w0_ref, a0_ref, kk_ref, ka_ref,
                 r_ref, lw_ref, kf_ref, v_ref, kn_ref, ba_ref, g_ref):
    i = pl.program_id(0)
    z = z_ref[...]
    tm = z.shape[0]
    prev = jnp.where(i % seq_blocks == 0, 0.0, zp_ref[7:8, :])
    row = lax.broadcasted_iota(jnp.int32, z.shape, 0)
    shifted = jnp.where(row == 0, prev, pltpu.roll(z, 1, 0))
    zs = z + (shifted - z) * mu_ref[...]
    r = zs[:, 0:c]
    k = zs[:, c:2 * c]
    v = zs[:, 2 * c:3 * c]
    zwa = zs[:, 3 * c:3 * c + 128]
    zg = zs[:, 3 * c + 128:3 * c + 256]
    lane = lax.broadcasted_iota(jnp.int32, (tm, 128), 1)
    lora_in = jnp.where(lane < 64, jnp.tanh(zwa), zwa)
    up = jnp.dot(lora_in, wlora_ref[...], preferred_element_type=F32, precision=HI)
    u = w0_ref[...] + up[:, 0:c]
    a = _sigmoid(a0_ref[...] + up[:, c:2 * c])
    g = jnp.dot(_sigmoid(zg), wgate_ref[...], preferred_element_type=F32, precision=HI)
    lw = -DECAY_SCALE * _sigmoid(u)
    kk = k * kk_ref[...]
    kn = kk / jnp.maximum(jnp.sqrt(_head_sums(kk * kk)), 1e-12)
    kf = k * (1.0 + (a - 1.0) * ka_ref[...])
    r_ref[...] = r
    lw_ref[...] = lw
    kf_ref[...] = kf
    v_ref[...] = v
    kn_ref[...] = kn
    ba_ref[...] = kn * a
    g_ref[...] = g


def _rwkv_prep(z, seq, c, mix_mu, w_lora, w_gate, w0, a0, k_k, k_a, tm):
    n, zc = z.shape
    row_spec = pl.BlockSpec((tm, c), lambda i: (i, 0))
    out = jax.ShapeDtypeStruct((n, c), F32)
    return pl.pallas_call(
        functools.partial(_prep_kernel, seq // tm, c),
        grid=(n // tm,),
        in_specs=[pl.BlockSpec((tm, zc), lambda i: (i, 0)),
                  pl.BlockSpec((8, zc), lambda i: (jnp.maximum(i * (tm // 8) - 1, 0), 0)),
                  _resident((1, zc)), _resident((128, 2 * c)), _resident((128, c)),
                  _resident((1, c)), _resident((1, c)), _resident((1, c)), _resident((1, c))],
        out_specs=[row_spec] * 7,
        out_shape=[out] * 7,
        compiler_params=_cparams(("parallel",)),
        name="rwkv_prep",
    )(z, z, mix_mu, w_lora, w_gate, w0, a0, k_k, k_a)


def _split3(x):
    h1 = x.astype(BF16)
    r1 = x - h1.astype(F32)
    h2 = r1.astype(BF16)
    h3 = (r1 - h2.astype(F32)).astype(BF16)
    return h1, h2, h3


def _mm(a, b):
    return jnp.dot(a, b, preferred_element_type=F32)


def _mm_nt(a, b):
    return lax.dot_general(a, b, (((1,), (1,)), ((), ())), preferred_element_type=F32)


def _mm_tn(a, b):
    return lax.dot_general(a, b, (((0,), (0,)), ((), ())), preferred_element_type=F32)


GROUP_HEADS = 4
GROUP_LANES = GROUP_HEADS * HEAD


def _block_diag(x, mask):
    return jnp.concatenate([x.astype(BF16)] * GROUP_HEADS, axis=0) * mask


def _scan_kernel(r_ref, lw_ref, kf_ref, v_ref, kn_ref, ba_ref, g_ref, rk_ref, lng_ref, lnb_ref, mask_ref, y_ref,
                 s_ref):
    @pl.when(pl.program_id(0) == 0)
    def _():
        s_ref[...] = jnp.zeros_like(s_ref)

    t = CHUNK
    nb = r_ref.shape[0]
    c = r_ref.shape[2]
    gl = GROUP_LANES
    n_groups = c // gl
    mask = mask_ref[...]
    ti = lax.broadcasted_iota(jnp.int32, (t, t), 0)
    si = lax.broadcasted_iota(jnp.int32, (t, t), 1)
    tri = jnp.where(si <= ti, 1.0, 0.0).astype(BF16)
    row = lax.broadcasted_iota(jnp.int32, (t, gl), 0)
    col = lax.broadcasted_iota(jnp.int32, (t, gl), 1) & (HEAD - 1)
    strict = col < row
    incl = col <= row
    eye = jnp.where(col == row, 1.0, 0.0)

    prep = []
    for b in range(nb):
        lw = lw_ref[b]
        cum = sum(jnp.dot(tri, part, preferred_element_type=F32) for part in _split3(lw))
        cum_end = cum[t - 1:t, :]
        e_pos = jnp.exp(cum)
        e_neg = jnp.exp(-cum)
        e_end = jnp.exp(cum_end - cum)
        r = r_ref[b]
        kf = kf_ref[b]
        ba = ba_ref[b]
        prep.append(dict(
            rt=r * e_pos,
            at=-kn_ref[b] * jnp.exp(cum - lw),
            bt=ba * e_neg,
            kt=kf * e_neg,
            bh=ba * e_end,
            kh=kf * e_end,
            v=v_ref[b],
            w_end=jnp.exp(cum_end),
            rkf=r * kf * rk_ref[...]))

    probs = [(b, gi) for b in range(nb) for gi in range(n_groups)]

    def part(b, gi, name):
        return prep[b][name][:, gi * gl:(gi + 1) * gl]

    a_ab, a_ak, a_rb, a_rk = [], [], [], []
    for b, gi in probs:
        lhs = jnp.concatenate([part(b, gi, "at"), part(b, gi, "rt")], axis=0).astype(BF16)
        rhs = jnp.concatenate([_block_diag(part(b, gi, "bt"), mask), _block_diag(part(b, gi, "kt"), mask)], axis=0)
        amat = _mm_nt(lhs, rhs)
        a_ab.append(jnp.where(strict, amat[0:t, 0:gl], 0.0))
        a_ak.append(jnp.where(strict, amat[0:t, gl:2 * gl], 0.0))
        a_rb.append(jnp.where(incl, amat[t:2 * t, 0:gl], 0.0))
        a_rk.append(jnp.where(incl, amat[t:2 * t, gl:2 * gl], 0.0))

    vbd = [_block_diag(part(b, gi, "v"), mask) for b, gi in probs]
    av = [_mm(a_ak[i].astype(BF16), vbd[i]) for i in range(len(probs))]
    minv = [eye + a for a in a_ab]
    power = [_mm(a.astype(BF16), _block_diag(a, mask)) for a in a_ab]
    span = 2
    while span < t:
        last = span * 2 >= t
        for i in range(len(probs)):
            pbd = _block_diag(power[i], mask)
            if last:
                minv[i] = minv[i] + _mm(minv[i].astype(BF16), pbd)
            else:
                both = _mm(jnp.concatenate([power[i], minv[i]], axis=0).astype(BF16), pbd)
                power[i] = both[0:t]
                minv[i] = minv[i] + both[t:2 * t]
        span *= 2

    s0 = [s_ref[b, gi] for b, gi in probs]
    sbd = [_block_diag(s, mask) for s in s0]
    ps = []
    for i, (b, gi) in enumerate(probs):
        lhs = jnp.concatenate([part(b, gi, "at"), part(b, gi, "rt")], axis=0).astype(BF16)
        ps.append(_mm_nt(lhs, sbd[i]))
    u = [_mm(minv[i].astype(BF16), _block_diag(ps[i][0:t] + av[i], mask)) for i in range(len(probs))]
    ys = []
    for i, (b, gi) in enumerate(probs):
        lhs = jnp.concatenate([a_rb[i], a_rk[i]], axis=1).astype(BF16)
        rhs = jnp.concatenate([_block_diag(u[i], mask), vbd[i]], axis=0)
        ys.append(ps[i][t:2 * t] + _mm(lhs, rhs))
        uv = jnp.concatenate([u[i], part(b, gi, "v")], axis=0).astype(BF16)
        bk = jnp.concatenate([part(b, gi, "bh"), part(b, gi, "kh")], axis=0).astype(BF16)
        full = _mm_tn(uv, bk) * mask.astype(F32)
        upd = full[0:HEAD]
        for hh in range(1, GROUP_HEADS):
            upd = upd + full[hh * HEAD:(hh + 1) * HEAD]
        s_ref[b, gi] = s0[i] * part(b, gi, "w_end") + upd

    for b in range(nb):
        y = jnp.concatenate([ys[b * n_groups + gi] for gi in range(n_groups)], axis=1)
        mu = _head_sums(y) * (1.0 / HEAD)
        yc = y - mu
        var = _head_sums(yc * yc) * (1.0 / HEAD)
        yn = yc * lax.rsqrt(var + LNX_EPS)
        bonus = _head_sums(prep[b]["rkf"]) * prep[b]["v"]
        y_ref[b] = ((yn * lng_ref[...] + lnb_ref[...] + bonus) * g_ref[b]).astype(y_ref.dtype)


def _rwkv_scan(r, lw, kf, v, kn, ba, g, r_k, lnx_g, lnx_b):
    b, s, c = r.shape
    blk = pl.BlockSpec((b, CHUNK, c), lambda ci: (0, ci, 0))
    hid = np.arange(GROUP_LANES) // HEAD
    mask = jnp.asarray(hid[:, None] == hid[None, :], BF16)
    return pl.pallas_call(
        _scan_kernel,
        grid=(s // CHUNK,),
        in_specs=[blk] * 7 + [_resident((1, c))] * 3 + [_resident((GROUP_LANES, GROUP_LANES))],
        out_specs=blk,
        out_shape=jax.ShapeDtypeStruct((b, s, c), BF16),
        scratch_shapes=[pltpu.VMEM((b, c // GROUP_LANES, HEAD, GROUP_LANES), F32)],
        compiler_params=_cparams(("arbitrary",)),
        name="rwkv_scan",
    )(r, lw, kf, v, kn, ba, g, r_k, lnx_g, lnx_b, mask)


def _attn_kernel(q_ref, kc_ref, kp_ref, vc_ref, vp_ref, sink_ref, o_ref):
    first = pl.program_id(1) == 0
    w = WINDOW
    group = q_ref.shape[-1] // HEAD // KV_HEADS
    row = lax.broadcasted_iota(jnp.int32, (group * w, 2 * w), 0)
    qi = row & (w - 1)
    kj = lax.broadcasted_iota(jnp.int32, (group * w, 2 * w), 1)
    lo = jnp.where(first, jnp.maximum(qi, w - 1), qi)
    valid = (kj > lo) & (kj <= qi + w)
    grow = lax.broadcasted_iota(jnp.int32, (group * w, 1), 0) // w
    q = q_ref[...]
    outs = []
    for hk in range(KV_HEADS):
        ksl = slice(hk * HEAD, (hk + 1) * HEAD)
        kcat = jnp.concatenate([kp_ref[:, ksl], kc_ref[:, ksl]], axis=0)
        vcat = jnp.concatenate([vp_ref[:, ksl], vc_ref[:, ksl]], axis=0)
        qg = jnp.concatenate([q[:, (hk * group + gi) * HEAD:(hk * group + gi + 1) * HEAD] for gi in range(group)],
                             axis=0)
        sink = jnp.zeros((group * w, 1), F32)
        for gi in range(group):
            sink = jnp.where(grow == gi, sink_ref[hk * group + gi:hk * group + gi + 1, 0:1], sink)
        s = _mm_nt(qg, kcat) * (HEAD ** -0.5)
        s = jnp.where(valid, s, -1e30)
        m = jnp.maximum(jnp.max(s, axis=-1, keepdims=True), sink)
        p = jnp.exp(s - m)
        pb = p.astype(BF16)
        psum = _mm(pb, jnp.ones((2 * w, HEAD), BF16))
        o = _mm(pb, vcat) / (psum + jnp.exp(sink - m))
        for gi in range(group):
            outs.append(o[gi * w:(gi + 1) * w, :])
    o_ref[...] = jnp.concatenate(outs, axis=-1).astype(o_ref.dtype)


def _attention(q, k, v, sinks_b):
    b, s, qc = q.shape
    kc = k.shape[-1]
    nb = s // WINDOW
    cur = lambda bi, i: (bi, i, 0)
    prev = lambda bi, i: (bi, jnp.maximum(i - 1, 0), 0)
    return pl.pallas_call(
        _attn_kernel,
        grid=(b, nb),
        in_specs=[pl.BlockSpec((None, WINDOW, qc), cur),
                  pl.BlockSpec((None, WINDOW, kc), cur), pl.BlockSpec((None, WINDOW, kc), prev),
                  pl.BlockSpec((None, WINDOW, kc), cur), pl.BlockSpec((None, WINDOW, kc), prev),
                  _resident(sinks_b.shape)],
        out_specs=pl.BlockSpec((None, WINDOW, qc), cur),
        out_shape=jax.ShapeDtypeStruct((b, s, qc), BF16),
        compiler_params=_cparams(("parallel", "parallel")),
        name="swa_attention",
    )(q, k, k, v, v, sinks_b)


MERGE_SPLIT = 2


def _merge_kernel(n_exp, x_ref, yr_ref, ya_ref, gr_ref, ga_ref, wur_ref, wua_ref, wo_ref, n2_ref, wr_ref, br_ref,
                  x1_ref, h2_ref, gate_ref, idx_ref):
    tm = x_ref.shape[0] // MERGE_SPLIT
    parts = [slice(p * tm, (p + 1) * tm) for p in range(MERGE_SPLIT)]
    ups = [(_mm(yr_ref[rs, :], wur_ref[...]), _mm(ya_ref[rs, :], wua_ref[...])) for rs in parts]
    merged = [_sigmoid(gr_ref[rs, :].astype(F32)) * ur + _sigmoid(ga_ref[rs, :].astype(F32)) * ua
              for rs, (ur, ua) in zip(parts, ups)]
    x1s = [x_ref[rs, :] + _mm(m.astype(BF16), wo_ref[...]) for rs, m in zip(parts, merged)]
    h2s = []
    for rs, x1 in zip(parts, x1s):
        x1_ref[rs, :] = x1
        h2 = _rmsnorm(x1, n2_ref[...])
        h2_ref[rs, :] = h2
        h2s.append(h2)
    all_logits = [jnp.dot(h2, wr_ref[...], preferred_element_type=F32, precision=HI) + br_ref[...] for h2 in h2s]
    lane_e = lax.broadcasted_iota(jnp.int32, (tm, n_exp), 1).astype(F32)
    lane_o = lax.broadcasted_iota(jnp.int32, (tm, 128), 1)
    for rs, logits in zip(parts, all_logits):
        vals, idxs = [], []
        for _ in range(TOP_K):
            mx = jnp.max(logits, axis=-1, keepdims=True)
            ix = jnp.min(jnp.where(logits == mx, lane_e, float(n_exp)), axis=-1, keepdims=True)
            vals.append(mx)
            idxs.append(ix)
            logits = jnp.where(lane_e == ix, -jnp.inf, logits)
        exps = [jnp.exp(vk - vals[0]) for vk in vals]
        tot = exps[0] + exps[1] + exps[2] + exps[3]
        gate_o, idx_o = jnp.zeros((tm, 128), F32), jnp.zeros((tm, 128), F32)
        for kk in range(TOP_K):
            gate_o = jnp.where(lane_o == kk, exps[kk] / tot, gate_o)
            idx_o = jnp.where(lane_o == kk, idxs[kk], idx_o)
        gate_ref[rs, :] = gate_o
        idx_ref[rs, :] = idx_o.astype(jnp.int32)


def _merge(x, y_rwkv, y_attn, gates, w_up_r, w_up_a, w_out, norm2_g, w_router, b_router, tm):
    n, d = x.shape
    c = y_rwkv.shape[1]
    n_exp = w_router.shape[1]
    row = lambda cols, j=0: pl.BlockSpec((tm, cols), lambda i: (i, j))
    return pl.pallas_call(
        functools.partial(_merge_kernel, n_exp),
        grid=(n // tm,),
        in_specs=[row(d), row(c), row(c), row(d, 0), row(d, 1),
                  _resident((c, d)), _resident((c, d)), _resident((d, d)), _resident((1, d)),
                  _resident((d, n_exp)), _resident((1, n_exp))],
        out_specs=[row(d), row(d), row(128), row(128)],
        out_shape=[jax.ShapeDtypeStruct((n, d), F32), jax.ShapeDtypeStruct((n, d), F32),
                   jax.ShapeDtypeStruct((n, 128), F32), jax.ShapeDtypeStruct((n, 128), jnp.int32)],
        compiler_params=_cparams(("parallel",)),
        name="merge_router",
    )(x, y_rwkv, y_attn, gates, gates, w_up_r, w_up_a, w_out, norm2_g, w_router, b_router)


GATHER_UNROLL = 8


def _gather_rows(src_hbm, idx_ref, buf, sem):
    rows = buf.shape[0]

    def start(blk, carry):
        for u in range(GATHER_UNROLL):
            r = blk * GATHER_UNROLL + u
            pltpu.make_async_copy(src_hbm.at[pl.ds(idx_ref[0, 0, r], 1)], buf.at[pl.ds(r, 1)], sem).start()
        return carry

    lax.fori_loop(0, rows // GATHER_UNROLL, start, 0)
    pltpu.make_async_copy(buf, buf, sem).wait()


def _dispatch_kernel(nvalid_ref, tok_ref, h_hbm, o_ref, buf, sem):
    i = pl.program_id(0)

    @pl.when(i < nvalid_ref[0])
    def _():
        _gather_rows(h_hbm, tok_ref, buf, sem)
        o_ref[...] = buf[...].astype(o_ref.dtype)

    @pl.when(i >= nvalid_ref[0])
    def _():
        o_ref[...] = jnp.zeros_like(o_ref)


def _dispatch(h2, row_tok, n_valid, n_blocks):
    d = h2.shape[1]
    grid_spec = pltpu.PrefetchScalarGridSpec(
        num_scalar_prefetch=1,
        grid=(n_blocks,),
        in_specs=[pl.BlockSpec((1, 1, ROW_BLOCK), lambda i, nv: (i, 0, 0), memory_space=pltpu.SMEM),
                  pl.BlockSpec(memory_space=pl.ANY)],
        out_specs=pl.BlockSpec((ROW_BLOCK, d), lambda i, nv: (i, 0)),
        scratch_shapes=[pltpu.VMEM((ROW_BLOCK, d), F32), pltpu.SemaphoreType.DMA(())],
    )
    return pl.pallas_call(
        _dispatch_kernel,
        grid_spec=grid_spec,
        out_shape=jax.ShapeDtypeStruct((n_blocks * ROW_BLOCK, d), BF16),
        compiler_params=_cparams(("arbitrary",)),
        name="moe_dispatch",
    )(n_valid, row_tok.reshape(n_blocks, 1, ROW_BLOCK), h2)


def _ffn_kernel(nvalid_ref, bexp_ref, brows_ref, xs_ref, w1g_ref, w1l_ref, b1g_ref, b1l_ref, w2_ref, b2_ref, o_ref,
                act_ref):
    i = pl.program_id(0)
    j = pl.program_id(1)
    n_ff = act_ref.shape[0]
    rows_valid = brows_ref[i]
    n_sub = ROW_BLOCK // ROW_SUB

    @pl.when(j < n_ff)
    def _():
        w1g = w1g_ref[...].astype(BF16)
        w1l = w1l_ref[...].astype(BF16)
        for sb in range(n_sub):
            rs = slice(sb * ROW_SUB, (sb + 1) * ROW_SUB)

            @pl.when(sb * ROW_SUB < rows_valid)
            def _():
                x = xs_ref[rs, :]
                hg = _mm(x, w1g) + b1g_ref[...]
                hl = _mm(x, w1l) + b1l_ref[...]
                glu = jnp.minimum(hg, SWIGLU_LIMIT)
                lin = jnp.clip(hl, -SWIGLU_LIMIT, SWIGLU_LIMIT)
                act_ref[j, rs, :] = (glu * _sigmoid(SWIGLU_ALPHA * glu) * (lin + 1.0)).astype(BF16)

    @pl.when(j >= n_ff)
    def _():
        w2 = w2_ref[...].astype(BF16)
        for sb in range(n_sub):
            rs = slice(sb * ROW_SUB, (sb + 1) * ROW_SUB)

            @pl.when(sb * ROW_SUB < rows_valid)
            def _():
                acc = _mm(act_ref[0, rs, :], w2[0:FF_TILE])
                for jf in range(1, n_ff):
                    acc = acc + _mm(act_ref[jf, rs, :], w2[jf * FF_TILE:(jf + 1) * FF_TILE])
                o_ref[rs, :] = acc + b2_ref[...]

            @pl.when(sb * ROW_SUB >= rows_valid)
            def _():
                o_ref[rs, :] = jnp.zeros((ROW_SUB, o_ref.shape[1]), o_ref.dtype)


def _expert_ffn(xs, w1, b1, w2, b2, block_expert, block_rows, n_valid, n_blocks):
    d = xs.shape[1]
    n_exp, _, ff2 = w1.shape
    ff = ff2 // 2
    n_ff = ff // FF_TILE
    n_out = d // OUT_TILE

    def hid(i, j, nv):
        return jnp.where(i < nv[0], jnp.minimum(j, n_ff - 1), n_ff - 1)

    def out(i, j, nv):
        return jnp.where(i < nv[0], jnp.maximum(j - n_ff, 0), n_out - 1)

    grid_spec = pltpu.PrefetchScalarGridSpec(
        num_scalar_prefetch=3,
        grid=(n_blocks, n_ff + n_out),
        in_specs=[pl.BlockSpec((ROW_BLOCK, d), lambda i, j, nv, be, br: (i, 0)),
                  pl.BlockSpec((None, d, FF_TILE), lambda i, j, nv, be, br: (be[i], 0, hid(i, j, nv))),
                  pl.BlockSpec((None, d, FF_TILE), lambda i, j, nv, be, br: (be[i], 0, n_ff + hid(i, j, nv))),
                  pl.BlockSpec((None, 1, FF_TILE), lambda i, j, nv, be, br: (be[i], 0, hid(i, j, nv))),
                  pl.BlockSpec((None, 1, FF_TILE), lambda i, j, nv, be, br: (be[i], 0, n_ff + hid(i, j, nv))),
                  pl.BlockSpec((None, ff, OUT_TILE), lambda i, j, nv, be, br: (be[i], 0, out(i, j, nv))),
                  pl.BlockSpec((None, 1, OUT_TILE), lambda i, j, nv, be, br: (be[i], 0, out(i, j, nv)))],
        out_specs=pl.BlockSpec((ROW_BLOCK, OUT_TILE), lambda i, j, nv, be, br: (i, jnp.maximum(j - n_ff, 0))),
        scratch_shapes=[pltpu.VMEM((n_ff, ROW_BLOCK, FF_TILE), BF16)],
    )
    return pl.pallas_call(
        _ffn_kernel,
        grid_spec=grid_spec,
        out_shape=jax.ShapeDtypeStruct((n_blocks * ROW_BLOCK, d), F32),
        compiler_params=_cparams(("arbitrary", "arbitrary")),
        name="moe_ffn",
    )(n_valid, block_expert, block_rows, xs, w1, w1, b1.reshape(n_exp, 1, ff2), b1.reshape(n_exp, 1, ff2), w2,
      b2.reshape(n_exp, 1, d))


def _combine_kernel(pos_ref, x1_ref, gate_ref, nf_ref, ys_hbm, o_ref, buf, sem):
    tc = x1_ref.shape[0]
    _gather_rows(ys_hbm, pos_ref, buf, sem)
    acc = x1_ref[...]
    gate = gate_ref[...]
    for k in range(TOP_K):
        acc = acc + gate[:, k:k + 1] * buf[k * tc:(k + 1) * tc, :]
    o_ref[...] = _rmsnorm(acc, nf_ref[...])


def _combine(x1, gate, pos, ys, normf_g, tc):
    n, d = x1.shape
    nt = n // tc
    return pl.pallas_call(
        _combine_kernel,
        grid=(nt,),
        in_specs=[pl.BlockSpec((1, 1, TOP_K * tc), lambda i: (i, 0, 0), memory_space=pltpu.SMEM),
                  pl.BlockSpec((tc, d), lambda i: (i, 0)), pl.BlockSpec((tc, 128), lambda i: (i, 0)),
                  _resident((1, d)), pl.BlockSpec(memory_space=pl.ANY)],
        out_specs=pl.BlockSpec((tc, d), lambda i: (i, 0)),
        out_shape=jax.ShapeDtypeStruct((n, d), F32),
        scratch_shapes=[pltpu.VMEM((TOP_K * tc, d), F32), pltpu.SemaphoreType.DMA(())],
        compiler_params=_cparams(("arbitrary",)),
        name="moe_combine",
    )(pos, x1, gate, normf_g, ys)


def _route(top_idx, n_exp, tc):
    n = top_idx.shape[0]
    n_assign = n * TOP_K
    n_blocks = n_assign // ROW_BLOCK + n_exp
    tile = min(1024, n)
    hot = (top_idx[:, :, None] == jnp.arange(n_exp, dtype=jnp.int32)).any(axis=1).reshape(n // tile, tile, n_exp)
    earlier = jnp.asarray(np.tril(np.ones((tile, tile), np.float32), -1), BF16)
    within = jnp.einsum("ts,bse->bte", earlier, hot.astype(BF16), preferred_element_type=F32)
    tile_tot = jnp.sum(hot, axis=1, dtype=jnp.int32)
    tile_off = jnp.cumsum(tile_tot, axis=0) - tile_tot
    rank = (within.astype(jnp.int32) + tile_off[:, None, :]).reshape(n, n_exp)
    counts = jnp.sum(tile_tot, axis=0)
    padded = (counts + ROW_BLOCK - 1) // ROW_BLOCK * ROW_BLOCK
    pad_end = jnp.cumsum(padded)
    pad_start = pad_end - padded
    grp_start = jnp.cumsum(counts) - counts
    slot = jnp.take_along_axis(rank + pad_start[None, :], top_idx, axis=1).reshape(-1)
    block_start = jnp.arange(n_blocks, dtype=jnp.int32) * ROW_BLOCK
    block_expert = jnp.minimum(jnp.searchsorted(pad_end, block_start, side="right"), n_exp - 1).astype(jnp.int32)
    n_valid = (pad_end[-1] // ROW_BLOCK).astype(jnp.int32).reshape(1)
    order = jnp.argsort(top_idx.reshape(-1), stable=True)
    row_e = jnp.repeat(block_expert, ROW_BLOCK)
    row_r = jnp.arange(n_blocks * ROW_BLOCK, dtype=jnp.int32) - pad_start[row_e]
    src = jnp.clip(grp_start[row_e] + row_r, 0, n_assign - 1)
    row_tok = jnp.where((row_r >= 0) & (row_r < counts[row_e]), order[src] // TOP_K, 0).astype(jnp.int32)
    pos = slot.reshape(n // tc, tc, TOP_K).transpose(0, 2, 1).reshape(n // tc, 1, TOP_K * tc)
    block_rows = jnp.clip(counts[block_expert] - (block_start - pad_start[block_expert]), 0, ROW_BLOCK)
    block_rows = jnp.where(jnp.arange(n_blocks) < n_valid[0], block_rows, 0).astype(jnp.int32)
    return row_tok, block_expert, block_rows, n_valid, pos, n_blocks


def _layer(x, norm1_g, w_in, mix_mu, w0, w_decay_up, a0, w_iclr_up, w_gate_up, k_k, k_a, r_k, lnx_g, lnx_b, b_qkv,
           sinks, w_up_rwkv, w_up_attn, w_out, norm2_g, w_router, b_router, w1, b1, w2, b2, normf_g):
    b, s, d = x.shape
    n = b * s
    c = w_up_rwkv.shape[0]
    lora = w_decay_up.shape[0]
    qc = w_up_attn.shape[0]
    kvc = KV_HEADS * HEAD
    rwkv_cols = 3 * c + 2 * lora + w_gate_up.shape[0]
    qkv_cols = qc + 2 * kvc
    row = lambda t: t.reshape(1, -1).astype(F32)
    xf = x.reshape(n, d)
    g1 = row(norm1_g)
    w_in_b = w_in.astype(BF16)

    z_rwkv = _norm_proj(xf, g1, w_in_b[:, :rwkv_cols], jnp.zeros((1, rwkv_cols), F32), F32, 512)
    qkv = _norm_proj(xf, g1, w_in_b[:, rwkv_cols:rwkv_cols + qkv_cols], row(b_qkv), BF16, 512)
    gates = _norm_proj(xf, g1, w_in_b[:, rwkv_cols + qkv_cols:], jnp.zeros((1, 2 * d), F32), BF16, 512)

    zl = jnp.zeros((lora, c), F32)
    w_lora = jnp.concatenate([jnp.concatenate([w_decay_up, zl], axis=1), jnp.concatenate([zl, w_iclr_up], axis=1)],
                             axis=0)
    r, lw, kf, v, kn, ba, g = _rwkv_prep(z_rwkv, s, c, row(mix_mu), w_lora, w_gate_up, row(w0), row(a0), row(k_k),
                                         row(k_a), 256)
    as3 = lambda t: t.reshape(b, s, c)
    y_rwkv = _rwkv_scan(as3(r), as3(lw), as3(kf), as3(v), as3(kn), as3(ba), as3(g), row(r_k), row(lnx_g),
                        row(lnx_b)).reshape(n, c)

    q = qkv[:, :qc].reshape(b, s, qc)
    ka = qkv[:, qc:qc + kvc].reshape(b, s, kvc)
    va = qkv[:, qc + kvc:].reshape(b, s, kvc)
    sinks_b = jnp.broadcast_to(sinks.astype(F32).reshape(-1, 1), (sinks.shape[0], 128))
    y_attn = _attention(q, ka, va, sinks_b).reshape(n, qc)

    x1, h2, gate, top_idx = _merge(xf, y_rwkv, y_attn, gates, w_up_rwkv.astype(BF16), w_up_attn.astype(BF16),
                                   w_out.astype(BF16), row(norm2_g), w_router, row(b_router), 256)

    tc = 256
    row_tok, block_expert, block_rows, n_valid, pos, n_blocks = _route(top_idx[:, :TOP_K], w_router.shape[1], tc)
    xs = _dispatch(h2, row_tok, n_valid, n_blocks)
    ys = _expert_ffn(xs, w1, b1, w2, b2, block_expert, block_rows, n_valid, n_blocks)
    out = _combine(x1, gate, pos, ys, row(normf_g), tc)
    return out.reshape(b, s, d)


def kernel(x, norm1_g, w_in, mix_mu, w0, w_decay_up, a0, w_iclr_up, w_gate_up, k_k, k_a, r_k, lnx_g, lnx_b, b_qkv,
           sinks, w_up_rwkv, w_up_attn, w_out, norm2_g, w_router, b_router, w1, b1, w2, b2, normf_g):
    assert w_in.shape[0] == 1, "single-layer block"
    return _layer(x, norm1_g[0], w_in[0], mix_mu[0], w0[0], w_decay_up[0], a0[0], w_iclr_up[0], w_gate_up[0],
                  k_k[0], k_a[0], r_k[0], lnx_g[0], lnx_b[0], b_qkv[0], sinks[0], w_up_rwkv[0], w_up_attn[0],
                  w_out[0], norm2_g[0], w_router[0], b_router[0], w1[0], b1[0], w2[0], b2[0], normf_g)
```

```python
import functools

import jax
import jax.numpy as jnp
import numpy as np
from jax import lax
from jax.experimental import pallas as pl
from jax.experimental.pallas import tpu as pltpu

F32 = jnp.float32
BF16 = jnp.bfloat16

NORM_EPS = 1e-5
LNX_EPS = 64e-5
HEAD = 64
WINDOW = 128
KV_HEADS = 4
TOP_K = 4
SWIGLU_LIMIT = 7.0
SWIGLU_ALPHA = 1.702
DECAY_SCALE = float(np.exp(-0.5))

CHUNK = 64
ROW_BLOCK = 1024
ROW_SUB = 512
FF_TILE = 512
OUT_TILE = 512
VMEM_LIMIT = 56 * 1024 * 1024
HI = lax.Precision.HIGHEST


def _cparams(sem):
    return pltpu.CompilerParams(dimension_semantics=sem, vmem_limit_bytes=VMEM_LIMIT)


def _resident(shape):
    nd = len(shape)
    return pl.BlockSpec(shape, lambda *_: (0,) * nd, pipeline_mode=pl.Buffered(1))


def _rmsnorm(x, g):
    return x * lax.rsqrt(jnp.mean(x * x, axis=-1, keepdims=True) + NORM_EPS) * g


def _sigmoid(x):
    return 1.0 / (1.0 + jnp.exp(-x))


def _norm_proj_kernel(x_ref, g_ref, w_ref, b_ref, o_ref):
    h = _rmsnorm(x_ref[...], g_ref[...]).astype(BF16)
    z = jnp.dot(h, w_ref[...], preferred_element_type=F32) + b_ref[...]
    o_ref[...] = z.astype(o_ref.dtype)


def _norm_proj(x, g, w, b, out_dtype, tm):
    n, d = x.shape
    cols = w.shape[1]
    return pl.pallas_call(
        _norm_proj_kernel,
        grid=(n // tm,),
        in_specs=[pl.BlockSpec((tm, d), lambda i: (i, 0)), _resident((1, d)), _resident((d, cols)),
                  _resident((1, cols))],
        out_specs=pl.BlockSpec((tm, cols), lambda i: (i, 0)),
        out_shape=jax.ShapeDtypeStruct((n, cols), out_dtype),
        compiler_params=_cparams(("parallel",)),
        name="norm_proj",
    )(x, g, w, b)


def _head_sums(x):
    rows, c = x.shape
    lane = lax.broadcasted_iota(jnp.int32, (rows, 128), 1)
    low = lane < HEAD
    parts = []
    for gi in range(c // 128):
        xg = x[:, gi * 128:(gi + 1) * 128]
        s_lo = jnp.sum(jnp.where(low, xg, 0.0), axis=-1, keepdims=True)
        s_hi = jnp.sum(jnp.where(low, 0.0, xg), axis=-1, keepdims=True)
        parts.append(jnp.where(low, s_lo, s_hi))
    return jnp.concatenate(parts, axis=-1)


def _prep_kernel(seq_blocks, c, z_ref, zp_ref, mu_ref, wlora_ref, wgate_ref, w0_ref, a0_ref, kk_ref, ka_ref,
                 r_ref, lw_ref, kf_ref, v_ref, kn_ref, ba_ref, g_ref):
    i = pl.program_id(0)
    z = z_ref[...]
    tm = z.shape[0]
    prev = jnp.where(i % seq_blocks == 0, 0.0, zp_ref[7:8, :])
    row = lax.broadcasted_iota(jnp.int32, z.shape, 0)
    shifted = jnp.where(row == 0, prev, pltpu.roll(z, 1, 0))
    zs = z + (shifted - z) * mu_ref[...]
    r = zs[:, 0:c]
    k = zs[:, c:2 * c]
    v = zs[:, 2 * c:3 * c]
    zwa = zs[:, 3 * c:3 * c + 128]
    zg = zs[:, 3 * c + 128:3 * c + 256]
    lane = lax.broadcasted_iota(jnp.int32, (tm, 128), 1)
    lora_in = jnp.where(lane < 64, jnp.tanh(zwa), zwa)
    up = jnp.dot(lora_in, wlora_ref[...], preferred_element_type=F32, precision=HI)
    u = w0_ref[...] + up[:, 0:c]
    a = _sigmoid(a0_ref[...] + up[:, c:2 * c])
    g = jnp.dot(_sigmoid(zg), wgate_ref[...], preferred_element_type=F32, precision=HI)
    lw = -DECAY_SCALE * _sigmoid(u)
    kk = k * kk_ref[...]
    kn = kk / jnp.maximum(jnp.sqrt(_head_sums(kk * kk)), 1e-12)
    kf = k * (1.0 + (a - 1.0) * ka_ref[...])
    r_ref[...] = r
    lw_ref[...] = lw
    kf_ref[...] = kf
    v_ref[...] = v
    kn_ref[...] = kn
    ba_ref[...] = kn * a
    g_ref[...] = g


def _rwkv_prep(z, seq, c, mix_mu, w_lora, w_gate, w0, a0, k_k, k_a, tm):
    n, zc = z.shape
    row_spec = pl.BlockSpec((tm, c), lambda i: (i, 0))
    out = jax.ShapeDtypeStruct((n, c), F32)
    return pl.pallas_call(
        functools.partial(_prep_kernel, seq // tm, c),
        grid=(n // tm,),
        in_specs=[pl.BlockSpec((tm, zc), lambda i: (i, 0)),
                  pl.BlockSpec((8, zc), lambda i: (jnp.maximum(i * (tm // 8) - 1, 0), 0)),
                  _resident((1, zc)), _resident((128, 2 * c)), _resident((128, c)),
                  _resident((1, c)), _resident((1, c)), _resident((1, c)), _resident((1, c))],
        out_specs=[row_spec] * 7,
        out_shape=[out] * 7,
        compiler_params=_cparams(("parallel",)),
        name="rwkv_prep",
    )(z, z, mix_mu, w_lora, w_gate, w0, a0, k_k, k_a)


def _split3(x):
    h1 = x.astype(BF16)
    r1 = x - h1.astype(F32)
    h2 = r1.astype(BF16)
    h3 = (r1 - h2.astype(F32)).astype(BF16)
    return h1, h2, h3


def _mm(a, b):
    return jnp.dot(a, b, preferred_element_type=F32)


def _mm_nt(a, b):
    return lax.dot_general(a, b, (((1,), (1,)), ((), ())), preferred_element_type=F32)


def _mm_tn(a, b):
    return lax.dot_general(a, b, (((0,), (0,)), ((), ())), preferred_element_type=F32)


GROUP_HEADS = 4
GROUP_LANES = GROUP_HEADS * HEAD


def _block_diag(x, mask):
    return jnp.concatenate([x.astype(BF16)] * GROUP_HEADS, axis=0) * mask


def _scan_kernel(r_ref, lw_ref, kf_ref, v_ref, kn_ref, ba_ref, g_ref, rk_ref, lng_ref, lnb_ref, mask_ref, y_ref,
                 s_ref):
    @pl.when(pl.program_id(0) == 0)
    def _():
        s_ref[...] = jnp.zeros_like(s_ref)

    t = CHUNK
    nb = r_ref.shape[0]
    c = r_ref.shape[2]
    gl = GROUP_LANES
    n_groups = c // gl
    mask = mask_ref[...]
    ti = lax.broadcasted_iota(jnp.int32, (t, t), 0)
    si = lax.broadcasted_iota(jnp.int32, (t, t), 1)
    tri = jnp.where(si <= ti, 1.0, 0.0).astype(BF16)
    row = lax.broadcasted_iota(jnp.int32, (t, gl), 0)
    col = lax.broadcasted_iota(jnp.int32, (t, gl), 1) & (HEAD - 1)
    strict = col < row
    incl = col <= row
    eye = jnp.where(col == row, 1.0, 0.0)

    prep = []
    for b in range(nb):
        lw = lw_ref[b]
        cum = sum(jnp.dot(tri, part, preferred_element_type=F32) for part in _split3(lw))
        cum_end = cum[t - 1:t, :]
        e_pos = jnp.exp(cum)
        e_neg = jnp.exp(-cum)
        e_end = jnp.exp(cum_end - cum)
        r = r_ref[b]
        kf = kf_ref[b]
        ba = ba_ref[b]
        prep.append(dict(
            rt=r * e_pos,
            at=-kn_ref[b] * jnp.exp(cum - lw),
            bt=ba * e_neg,
            kt=kf * e_neg,
            bh=ba * e_end,
            kh=kf * e_end,
            v=v_ref[b],
            w_end=jnp.exp(cum_end),
            rkf=r * kf * rk_ref[...]))

    probs = [(b, gi) for b in range(nb) for gi in range(n_groups)]

    def part(b, gi, name):
        return prep[b][name][:, gi * gl:(gi + 1) * gl]

    a_ab, a_ak, a_rb, a_rk = [], [], [], []
    for b, gi in probs:
        lhs = jnp.concatenate([part(b, gi, "at"), part(b, gi, "rt")], axis=0).astype(BF16)
        rhs = jnp.concatenate([_block_diag(part(b, gi, "bt"), mask), _block_diag(part(b, gi, "kt"), mask)], axis=0)
        amat = _mm_nt(lhs, rhs)
        a_ab.append(jnp.where(strict, amat[0:t, 0:gl], 0.0))
        a_ak.append(jnp.where(strict, amat[0:t, gl:2 * gl], 0.0))
        a_rb.append(jnp.where(incl, amat[t:2 * t, 0:gl], 0.0))
        a_rk.append(jnp.where(incl, amat[t:2 * t, gl:2 * gl], 0.0))

    vbd = [_block_diag(part(b, gi, "v"), mask) for b, gi in probs]
    av = [_mm(a_ak[i].astype(BF16), vbd[i]) for i in range(len(probs))]
    minv = [eye + a for a in a_ab]
    power = [_mm(a.astype(BF16), _block_diag(a, mask)) for a in a_ab]
    span = 2
    while span < t:
        last = span * 2 >= t
        for i in range(len(probs)):
            pbd = _block_diag(power[i], mask)
            if last:
                minv[i] = minv[i] + _mm(minv[i].astype(BF16), pbd)
            else:
                both = _mm(jnp.concatenate([power[i], minv[i]], axis=0).astype(BF16), pbd)
                power[i] = both[0:t]
                minv[i] = minv[i] + both[t:2 * t]
        span *= 2

    s0 = [s_ref[b, gi] for b, gi in probs]
    sbd = [_block_diag(s, mask) for s in s0]
    ps = []
    for i, (b, gi) in enumerate(probs):
        lhs = jnp.concatenate([part(b, gi, "at"), part(b, gi, "rt")], axis=0).astype(BF16)
        ps.append(_mm_nt(lhs, sbd[i]))
    u = [_mm(minv[i].astype(BF16), _block_diag(ps[i][0:t] + av[i], mask)) for i in range(len(probs))]
    ys = []
    for i, (b, gi) in enumerate(probs):
        lhs = jnp.concatenate([a_rb[i], a_rk[i]], axis=1).astype(BF16)
        rhs = jnp.concatenate([_block_diag(u[i], mask), vbd[i]], axis=0)
        ys.append(ps[i][t:2 * t] + _mm(lhs, rhs))
        uv = jnp.concatenate([u[i], part(b, gi, "v")], axis=0).astype(BF16)
        bk = jnp.concatenate([part(b, gi, "bh"), part(b, gi, "kh")], axis=0).astype(BF16)
        full = _mm_tn(uv, bk) * mask.astype(F32)
        upd = full[0:HEAD]
        for hh in range(1, GROUP_HEADS):
            upd = upd + full[hh * HEAD:(hh + 1) * HEAD]
        s_ref[b, gi] = s0[i] * part(b, gi, "w_end") + upd

    for b in range(nb):
        y = jnp.concatenate([ys[b * n_groups + gi] for gi in range(n_groups)], axis=1)
        mu = _head_sums(y) * (1.0 / HEAD)
        yc = y - mu
        var = _head_sums(yc * yc) * (1.0 / HEAD)
        yn = yc * lax.rsqrt(var + LNX_EPS)
        bonus = _head_sums(prep[b]["rkf"]) * prep[b]["v"]
        y_ref[b] = ((yn * lng_ref[...] + lnb_ref[...] + bonus) * g_ref[b]).astype(y_ref.dtype)


def _rwkv_scan(r, lw, kf, v, kn, ba, g, r_k, lnx_g, lnx_b):
    b, s, c = r.shape
    blk = pl.BlockSpec((b, CHUNK, c), lambda ci: (0, ci, 0))
    hid = np.arange(GROUP_LANES) // HEAD
    mask = jnp.asarray(hid[:, None] == hid[None, :], BF16)
    return pl.pallas_call(
        _scan_kernel,
        grid=(s // CHUNK,),
        in_specs=[blk] * 7 + [_resident((1, c))] * 3 + [_resident((GROUP_LANES, GROUP_LANES))],
        out_specs=blk,
        out_shape=jax.ShapeDtypeStruct((b, s, c), BF16),
        scratch_shapes=[pltpu.VMEM((b, c // GROUP_LANES, HEAD, GROUP_LANES), F32)],
        compiler_params=_cparams(("arbitrary",)),
        name="rwkv_scan",
    )(r, lw, kf, v, kn, ba, g, r_k, lnx_g, lnx_b, mask)


def _attn_kernel(q_ref, kc_ref, kp_ref, vc_ref, vp_ref, sink_ref, o_ref):
    first = pl.program_id(1) == 0
    w = WINDOW
    group = q_ref.shape[-1] // HEAD // KV_HEADS
    row = lax.broadcasted_iota(jnp.int32, (group * w, 2 * w), 0)
    qi = row & (w - 1)
    kj = lax.broadcasted_iota(jnp.int32, (group * w, 2 * w), 1)
    lo = jnp.where(first, jnp.maximum(qi, w - 1), qi)
    valid = (kj > lo) & (kj <= qi + w)
    grow = lax.broadcasted_iota(jnp.int32, (group * w, 1), 0) // w
    q = q_ref[...]
    outs = []
    for hk in range(KV_HEADS):
        ksl = slice(hk * HEAD, (hk + 1) * HEAD)
        kcat = jnp.concatenate([kp_ref[:, ksl], kc_ref[:, ksl]], axis=0)
        vcat = jnp.concatenate([vp_ref[:, ksl], vc_ref[:, ksl]], axis=0)
        qg = jnp.concatenate([q[:, (hk * group + gi) * HEAD:(hk * group + gi + 1) * HEAD] for gi in range(group)],
                             axis=0)
        sink = jnp.zeros((group * w, 1), F32)
        for gi in range(group):
            sink = jnp.where(grow == gi, sink_ref[hk * group + gi:hk * group + gi + 1, 0:1], sink)
        s = _mm_nt(qg, kcat) * (HEAD ** -0.5)
        s = jnp.where(valid, s, -1e30)
        m = jnp.maximum(jnp.max(s, axis=-1, keepdims=True), sink)
        p = jnp.exp(s - m)
        pb = p.astype(BF16)
        psum = _mm(pb, jnp.ones((2 * w, HEAD), BF16))
        o = _mm(pb, vcat) / (psum + jnp.exp(sink - m))
        for gi in range(group):
            outs.append(o[gi * w:(gi + 1) * w, :])
    o_ref[...] = jnp.concatenate(outs, axis=-1).astype(o_ref.dtype)


def _attention(q, k, v, sinks_b):
    b, s, qc = q.shape
    kc = k.shape[-1]
    nb = s // WINDOW
    cur = lambda bi, i: (bi, i, 0)
    prev = lambda bi, i: (bi, jnp.maximum(i - 1, 0), 0)
    return pl.pallas_call(
        _attn_kernel,
        grid=(b, nb),
        in_specs=[pl.BlockSpec((None, WINDOW, qc), cur),
                  pl.BlockSpec((None, WINDOW, kc), cur), pl.BlockSpec((None, WINDOW, kc), prev),
                  pl.BlockSpec((None, WINDOW, kc), cur), pl.BlockSpec((None, WINDOW, kc), prev),
                  _resident(sinks_b.shape)],
        out_specs=pl.BlockSpec((None, WINDOW, qc), cur),
        out_shape=jax.ShapeDtypeStruct((b, s, qc), BF16),
        compiler_params=_cparams(("parallel", "parallel")),
        name="swa_attention",
    )(q, k, k, v, v, sinks_b)


MERGE_SPLIT = 2


def _merge_kernel(n_exp, x_ref, yr_ref, ya_ref, gr_ref, ga_ref, wur_ref, wua_ref, wo_ref, n2_ref, wr_ref, br_ref,
                  x1_ref, h2_ref, gate_ref, idx_ref):
    tm = x_ref.shape[0] // MERGE_SPLIT
    parts = [slice(p * tm, (p + 1) * tm) for p in range(MERGE_SPLIT)]
    ups = [(_mm(yr_ref[rs, :], wur_ref[...]), _mm(ya_ref[rs, :], wua_ref[...])) for rs in parts]
    merged = [_sigmoid(gr_ref[rs, :].astype(F32)) * ur + _sigmoid(ga_ref[rs, :].astype(F32)) * ua
              for rs, (ur, ua) in zip(parts, ups)]
    x1s = [x_ref[rs, :] + _mm(m.astype(BF16), wo_ref[...]) for rs, m in zip(parts, merged)]
    h2s = []
    for rs, x1 in zip(parts, x1s):
        x1_ref[rs, :] = x1
        h2 = _rmsnorm(x1, n2_ref[...])
        h2_ref[rs, :] = h2
        h2s.append(h2)
    all_logits = [jnp.dot(h2, wr_ref[...], preferred_element_type=F32, precision=HI) + br_ref[...] for h2 in h2s]
    lane_e = lax.broadcasted_iota(jnp.int32, (tm, n_exp), 1).astype(F32)
    lane_o = lax.broadcasted_iota(jnp.int32, (tm, 128), 1)
    for rs, logits in zip(parts, all_logits):
        vals, idxs = [], []
        for _ in range(TOP_K):
            mx = jnp.max(logits, axis=-1, keepdims=True)
            ix = jnp.min(jnp.where(logits == mx, lane_e, float(n_exp)), axis=-1, keepdims=True)
            vals.append(mx)
            idxs.append(ix)
            logits = jnp.where(lane_e == ix, -jnp.inf, logits)
        exps = [jnp.exp(vk - vals[0]) for vk in vals]
        tot = exps[0] + exps[1] + exps[2] + exps[3]
        gate_o, idx_o = jnp.zeros((tm, 128), F32), jnp.zeros((tm, 128), F32)
        for kk in range(TOP_K):
            gate_o = jnp.where(lane_o == kk, exps[kk] / tot, gate_o)
            idx_o = jnp.where(lane_o == kk, idxs[kk], idx_o)
        gate_ref[rs, :] = gate_o
        idx_ref[rs, :] = idx_o.astype(jnp.int32)


def _merge(x, y_rwkv, y_attn, gates, w_up_r, w_up_a, w_out, norm2_g, w_router, b_router, tm):
    n, d = x.shape
    c = y_rwkv.shape[1]
    n_exp = w_router.shape[1]
    row = lambda cols, j=0: pl.BlockSpec((tm, cols), lambda i: (i, j))
    return pl.pallas_call(
        functools.partial(_merge_kernel, n_exp),
        grid=(n // tm,),
        in_specs=[row(d), row(c), row(c), row(d, 0), row(d, 1),
                  _resident((c, d)), _resident((c, d)), _resident((d, d)), _resident((1, d)),
                  _resident((d, n_exp)), _resident((1, n_exp))],
        out_specs=[row(d), row(d), row(128), row(128)],
        out_shape=[jax.ShapeDtypeStruct((n, d), F32), jax.ShapeDtypeStruct((n, d), F32),
                   jax.ShapeDtypeStruct((n, 128), F32), jax.ShapeDtypeStruct((n, 128), jnp.int32)],
        compiler_params=_cparams(("parallel",)),
        name="merge_router",
    )(x, y_rwkv, y_attn, gates, gates, w_up_r, w_up_a, w_out, norm2_g, w_router, b_router)


GATHER_UNROLL = 8


def _gather_rows(src_hbm, idx_ref, buf, sem):
    rows = buf.shape[0]

    def start(blk, carry):
        for u in range(GATHER_UNROLL):
            r = blk * GATHER_UNROLL + u
            pltpu.make_async_copy(src_hbm.at[pl.ds(idx_ref[0, 0, r], 1)], buf.at[pl.ds(r, 1)], sem).start()
        return carry

    lax.fori_loop(0, rows // GATHER_UNROLL, start, 0)
    pltpu.make_async_copy(buf, buf, sem).wait()


def _dispatch_kernel(nvalid_ref, tok_ref, h_hbm, o_ref, buf, sem):
    i = pl.program_id(0)

    @pl.when(i < nvalid_ref[0])
    def _():
        _gather_rows(h_hbm, tok_ref, buf, sem)
        o_ref[...] = buf[...].astype(o_ref.dtype)

    @pl.when(i >= nvalid_ref[0])
    def _():
        o_ref[...] = jnp.zeros_like(o_ref)


def _dispatch(h2, row_tok, n_valid, n_blocks):
    d = h2.shape[1]
    grid_spec = pltpu.PrefetchScalarGridSpec(
        num_scalar_prefetch=1,
        grid=(n_blocks,),
        in_specs=[pl.BlockSpec((1, 1, ROW_BLOCK), lambda i, nv: (i, 0, 0), memory_space=pltpu.SMEM),
                  pl.BlockSpec(memory_space=pl.ANY)],
        out_specs=pl.BlockSpec((ROW_BLOCK, d), lambda i, nv: (i, 0)),
        scratch_shapes=[pltpu.VMEM((ROW_BLOCK, d), F32), pltpu.SemaphoreType.DMA(())],
    )
    return pl.pallas_call(
        _dispatch_kernel,
        grid_spec=grid_spec,
        out_shape=jax.ShapeDtypeStruct((n_blocks * ROW_BLOCK, d), BF16),
        compiler_params=_cparams(("arbitrary",)),
        name="moe_dispatch",
    )(n_valid, row_tok.reshape(n_blocks, 1, ROW_BLOCK), h2)


def _ffn_kernel(nvalid_ref, bexp_ref, brows_ref, xs_ref, w1g_ref, w1l_ref, b1g_ref, b1l_ref, w2_ref, b2_ref, o_ref,
                act_ref):
    i = pl.program_id(0)
    j = pl.program_id(1)
    n_ff = act_ref.shape[0]
    rows_valid = brows_ref[i]
    n_sub = ROW_BLOCK // ROW_SUB

    @pl.when(j < n_ff)
    def _():
        w1g = w1g_ref[...].astype(BF16)
        w1l = w1l_ref[...].astype(BF16)
        for sb in range(n_sub):
            rs = slice(sb * ROW_SUB, (sb + 1) * ROW_SUB)

            @pl.when(sb * ROW_SUB < rows_valid)
            def _():
                x = xs_ref[rs, :]
                hg = _mm(x, w1g) + b1g_ref[...]
                hl = _mm(x, w1l) + b1l_ref[...]
                glu = jnp.minimum(hg, SWIGLU_LIMIT)
                lin = jnp.clip(hl, -SWIGLU_LIMIT, SWIGLU_LIMIT)
                act_ref[j, rs, :] = (glu * _sigmoid(SWIGLU_ALPHA * glu) * (lin + 1.0)).astype(BF16)

    @pl.when(j >= n_ff)
    def _():
        w2 = w2_ref[...].astype(BF16)
        for sb in range(n_sub):
            rs = slice(sb * ROW_SUB, (sb + 1) * ROW_SUB)

            @pl.when(sb * ROW_SUB < rows_valid)
            def _():
                acc = _mm(act_ref[0, rs, :], w2[0:FF_TILE])
                for jf in range(1, n_ff):
                    acc = acc + _mm(act_ref[jf, rs, :], w2[jf * FF_TILE:(jf + 1) * FF_TILE])
                o_ref[rs, :] = acc + b2_ref[...]

            @pl.when(sb * ROW_SUB >= rows_valid)
            def _():
                o_ref[rs, :] = jnp.zeros((ROW_SUB, o_ref.shape[1]), o_ref.dtype)


def _expert_ffn(xs, w1, b1, w2, b2, block_expert, block_rows, n_valid, n_blocks):
    d = xs.shape[1]
    n_exp, _, ff2 = w1.shape
    ff = ff2 // 2
    n_ff = ff // FF_TILE
    n_out = d // OUT_TILE

    def hid(i, j, nv):
        return jnp.where(i < nv[0], jnp.minimum(j, n_ff - 1), n_ff - 1)

    def out(i, j, nv):
        return jnp.where(i < nv[0], jnp.maximum(j - n_ff, 0), n_out - 1)

    grid_spec = pltpu.PrefetchScalarGridSpec(
        num_scalar_prefetch=3,
        grid=(n_blocks, n_ff + n_out),
        in_specs=[pl.BlockSpec((ROW_BLOCK, d), lambda i, j, nv, be, br: (i, 0)),
                  pl.BlockSpec((None, d, FF_TILE), lambda i, j, nv, be, br: (be[i], 0, hid(i, j, nv))),
                  pl.BlockSpec((None, d, FF_TILE), lambda i, j, nv, be, br: (be[i], 0, n_ff + hid(i, j, nv))),
                  pl.BlockSpec((None, 1, FF_TILE), lambda i, j, nv, be, br: (be[i], 0, hid(i, j, nv))),
                  pl.BlockSpec((None, 1, FF_TILE), lambda i, j, nv, be, br: (be[i], 0, n_ff + hid(i, j, nv))),
                  pl.BlockSpec((None, ff, OUT_TILE), lambda i, j, nv, be, br: (be[i], 0, out(i, j, nv))),
                  pl.BlockSpec((None, 1, OUT_TILE), lambda i, j, nv, be, br: (be[i], 0, out(i, j, nv)))],
        out_specs=pl.BlockSpec((ROW_BLOCK, OUT_TILE), lambda i, j, nv, be, br: (i, jnp.maximum(j - n_ff, 0))),
        scratch_shapes=[pltpu.VMEM((n_ff, ROW_BLOCK, FF_TILE), BF16)],
    )
    return pl.pallas_call(
        _ffn_kernel,
        grid_spec=grid_spec,
        out_shape=jax.ShapeDtypeStruct((n_blocks * ROW_BLOCK, d), F32),
        compiler_params=_cparams(("arbitrary", "arbitrary")),
        name="moe_ffn",
    )(n_valid, block_expert, block_rows, xs, w1, w1, b1.reshape(n_exp, 1, ff2), b1.reshape(n_exp, 1, ff2), w2,
      b2.reshape(n_exp, 1, d))


def _combine_kernel(pos_ref, x1_ref, gate_ref, nf_ref, ys_hbm, o_ref, buf, sem):
    tc = x1_ref.shape[0]
    _gather_rows(ys_hbm, pos_ref, buf, sem)
    acc = x1_ref[...]
    gate = gate_ref[...]
    for k in range(TOP_K):
        acc = acc + gate[:, k:k + 1] * buf[k * tc:(k + 1) * tc, :]
    o_ref[...] = _rmsnorm(acc, nf_ref[...])


def _combine(x1, gate, pos, ys, normf_g, tc):
    n, d = x1.shape
    nt = n // tc
    return pl.pallas_call(
        _combine_kernel,
        grid=(nt,),
        in_specs=[pl.BlockSpec((1, 1, TOP_K * tc), lambda i: (i, 0, 0), memory_space=pltpu.SMEM),
                  pl.BlockSpec((tc, d), lambda i: (i, 0)), pl.BlockSpec((tc, 128), lambda i: (i, 0)),
                  _resident((1, d)), pl.BlockSpec(memory_space=pl.ANY)],
        out_specs=pl.BlockSpec((tc, d), lambda i: (i, 0)),
        out_shape=jax.ShapeDtypeStruct((n, d), F32),
        scratch_shapes=[pltpu.VMEM((TOP_K * tc, d), F32), pltpu.SemaphoreType.DMA(())],
        compiler_params=_cparams(("arbitrary",)),
        name="moe_combine",
    )(pos, x1, gate, normf_g, ys)


def _route(top_idx, n_exp, tc):
    n = top_idx.shape[0]
    n_assign = n * TOP_K
    n_blocks = n_assign // ROW_BLOCK + n_exp
    tile = min(1024, n)
    hot = (top_idx[:, :, None] == jnp.arange(n_exp, dtype=jnp.int32)).any(axis=1).reshape(n // tile, tile, n_exp)
    earlier = jnp.asarray(np.tril(np.ones((tile, tile), np.float32), -1), BF16)
    within = jnp.einsum("ts,bse->bte", earlier, hot.astype(BF16), preferred_element_type=F32)
    tile_tot = jnp.sum(hot, axis=1, dtype=jnp.int32)
    tile_off = jnp.cumsum(tile_tot, axis=0) - tile_tot
    rank = (within.astype(jnp.int32) + tile_off[:, None, :]).reshape(n, n_exp)
    counts = jnp.sum(tile_tot, axis=0)
    padded = (counts + ROW_BLOCK - 1) // ROW_BLOCK * ROW_BLOCK
    pad_end = jnp.cumsum(padded)
    pad_start = pad_end - padded
    grp_start = jnp.cumsum(counts) - counts
    slot = jnp.take_along_axis(rank + pad_start[None, :], top_idx, axis=1).reshape(-1)
    block_start = jnp.arange(n_blocks, dtype=jnp.int32) * ROW_BLOCK
    block_expert = jnp.minimum(jnp.searchsorted(pad_end, block_start, side="right"), n_exp - 1).astype(jnp.int32)
    n_valid = (pad_end[-1] // ROW_BLOCK).astype(jnp.int32).reshape(1)
    order = jnp.argsort(top_idx.reshape(-1), stable=True)
    row_e = jnp.repeat(block_expert, ROW_BLOCK)
    row_r = jnp.arange(n_blocks * ROW_BLOCK, dtype=jnp.int32) - pad_start[row_e]
    src = jnp.clip(grp_start[row_e] + row_r, 0, n_assign - 1)
    row_tok = jnp.where((row_r >= 0) & (row_r < counts[row_e]), order[src] // TOP_K, 0).astype(jnp.int32)
    pos = slot.reshape(n // tc, tc, TOP_K).transpose(0, 2, 1).reshape(n // tc, 1, TOP_K * tc)
    block_rows = jnp.clip(counts[block_expert] - (block_start - pad_start[block_expert]), 0, ROW_BLOCK)
    block_rows = jnp.where(jnp.arange(n_blocks) < n_valid[0], block_rows, 0).astype(jnp.int32)
    return row_tok, block_expert, block_rows, n_valid, pos, n_blocks


def _layer(x, norm1_g, w_in, mix_mu, w0, w_decay_up, a0, w_iclr_up, w_gate_up, k_k, k_a, r_k, lnx_g, lnx_b, b_qkv,
           sinks, w_up_rwkv, w_up_attn, w_out, norm2_g, w_router, b_router, w1, b1, w2, b2, normf_g):
    b, s, d = x.shape
    n = b * s
    c = w_up_rwkv.shape[0]
    lora = w_decay_up.shape[0]
    qc = w_up_attn.shape[0]
    kvc = KV_HEADS * HEAD
    rwkv_cols = 3 * c + 2 * lora + w_gate_up.shape[0]
    qkv_cols = qc + 2 * kvc
    row = lambda t: t.reshape(1, -1).astype(F32)
    xf = x.reshape(n, d)
    g1 = row(norm1_g)
    w_in_b = w_in.astype(BF16)

    z_rwkv = _norm_proj(xf, g1, w_in_b[:, :rwkv_cols], jnp.zeros((1, rwkv_cols), F32), F32, 512)
    qkv = _norm_proj(xf, g1, w_in_b[:, rwkv_cols:rwkv_cols + qkv_cols], row(b_qkv), BF16, 512)
    gates = _norm_proj(xf, g1, w_in_b[:, rwkv_cols + qkv_cols:], jnp.zeros((1, 2 * d), F32), BF16, 512)

    zl = jnp.zeros((lora, c), F32)
    w_lora = jnp.concatenate([jnp.concatenate([w_decay_up, zl], axis=1), jnp.concatenate([zl, w_iclr_up], axis=1)],
                             axis=0)
    r, lw, kf, v, kn, ba, g = _rwkv_prep(z_rwkv, s, c, row(mix_mu), w_lora, w_gate_up, row(w0), row(a0), row(k_k),
                                         row(k_a), 256)
    as3 = lambda t: t.reshape(b, s, c)
    y_rwkv = _rwkv_scan(as3(r), as3(lw), as3(kf), as3(v), as3(kn), as3(ba), as3(g), row(r_k), row(lnx_g),
                        row(lnx_b)).reshape(n, c)

    q = qkv[:, :qc].reshape(b, s, qc)
    ka = qkv[:, qc:qc + kvc].reshape(b, s, kvc)
    va = qkv[:, qc + kvc:].reshape(b, s, kvc)
    sinks_b = jnp.broadcast_to(sinks.astype(F32).reshape(-1, 1), (sinks.shape[0], 128))
    y_attn = _attention(q, ka, va, sinks_b).reshape(n, qc)

    x1, h2, gate, top_idx = _merge(xf, y_rwkv, y_attn, gates, w_up_rwkv.astype(BF16), w_up_attn.astype(BF16),
                                   w_out.astype(BF16), row(norm2_g), w_router, row(b_router), 256)

    tc = 256
    row_tok, block_expert, block_rows, n_valid, pos, n_blocks = _route(top_idx[:, :TOP_K], w_router.shape[1], tc)
    xs = _dispatch(h2, row_tok, n_valid, n_blocks)
    ys = _expert_ffn(xs, w1, b1, w2, b2, block_expert, block_rows, n_valid, n_blocks)
    out = _combine(x1, gate, pos, ys, row(normf_g), tc)
    return out.reshape(b, s, d)


def kernel(x, norm1_g, w_in, mix_mu, w0, w_decay_up, a0, w_iclr_up, w_gate_up, k_k, k_a, r_k, lnx_g, lnx_b, b_qkv,
           sinks, w_up_rwkv, w_up_attn, w_out, norm2_g, w_router, b_router, w1, b1, w2, b2, normf_g):
    assert w_in.shape[0] == 1, "single-layer block"
    return _layer(x, norm1_g[0], w_in[0], mix_mu[0], w0[0], w_decay_up[0], a0[0], w_iclr_up[0], w_gate_up[0],
                  k_k[0], k_a[0], r_k[0], lnx_g[0], lnx_b[0], b_qkv[0], sinks[0], w_up_rwkv[0], w_up_attn[0],
                  w_out[0], norm2_g[0], w_router[0], b_router[0], w1[0], b1[0], w2[0], b2[0], normf_g)
```

```python
import functools

import jax
import jax.numpy as jnp
import numpy as np
from jax import lax
from jax.experimental import pallas as pl
from jax.experimental.pallas import tpu as pltpu

F32 = jnp.float32
BF16 = jnp.bfloat16

NORM_EPS = 1e-5
LNX_EPS = 64e-5
HEAD = 64
WINDOW = 128
KV_HEADS = 4
TOP_K = 4
SWIGLU_LIMIT = 7.0
SWIGLU_ALPHA = 1.702
DECAY_SCALE = float(np.exp(-0.5))

CHUNK = 64
ROW_BLOCK = 2048
ROW_SUB = 512
FF_TILE = 256
OUT_TILE = 256
VMEM_LIMIT = 56 * 1024 * 1024
HI = lax.Precision.HIGHEST


def _cparams(sem):
    return pltpu.CompilerParams(dimension_semantics=sem, vmem_limit_bytes=VMEM_LIMIT)


def _resident(shape):
    nd = len(shape)
    return pl.BlockSpec(shape, lambda *_: (0,) * nd, pipeline_mode=pl.Buffered(1))


def _rmsnorm(x, g):
    return x * lax.rsqrt(jnp.mean(x * x, axis=-1, keepdims=True) + NORM_EPS) * g


def _sigmoid(x):
    return 1.0 / (1.0 + jnp.exp(-x))


def _norm_proj_kernel(x_ref, g_ref, w_ref, b_ref, o_ref):
    h = _rmsnorm(x_ref[...], g_ref[...]).astype(BF16)
    z = jnp.dot(h, w_ref[...], preferred_element_type=F32) + b_ref[...]
    o_ref[...] = z.astype(o_ref.dtype)


def _norm_proj(x, g, w, b, out_dtype, tm):
    n, d = x.shape
    cols = w.shape[1]
    return pl.pallas_call(
        _norm_proj_kernel,
        grid=(n // tm,),
        in_specs=[pl.BlockSpec((tm, d), lambda i: (i, 0)), _resident((1, d)), _resident((d, cols)),
                  _resident((1, cols))],
        out_specs=pl.BlockSpec((tm, cols), lambda i: (i, 0)),
        out_shape=jax.ShapeDtypeStruct((n, cols), out_dtype),
        compiler_params=_cparams(("parallel",)),
        name="norm_proj",
    )(x, g, w, b)


def _head_sums(x):
    rows, c = x.shape
    lane = lax.broadcasted_iota(jnp.int32, (rows, 128), 1)
    low = lane < HEAD
    parts = []
    for gi in range(c // 128):
        xg = x[:, gi * 128:(gi + 1) * 128]
        s_lo = jnp.sum(jnp.where(low, xg, 0.0), axis=-1, keepdims=True)
        s_hi = jnp.sum(jnp.where(low, 0.0, xg), axis=-1, keepdims=True)
        parts.append(jnp.where(low, s_lo, s_hi))
    return jnp.concatenate(parts, axis=-1)


def _prep_kernel(seq_blocks, c, z_ref, zp_ref, mu_ref, wlora_ref, wgate_ref, w0_ref, a0_ref, kk_ref, ka_ref,
                 r_ref, lw_ref, kf_ref, v_ref, kn_ref, ba_ref, g_ref):
    i = pl.program_id(0)
    z = z_ref[...]
    tm = z.shape[0]
    prev = jnp.where(i % seq_blocks == 0, 0.0, zp_ref[7:8, :])
    row = lax.broadcasted_iota(jnp.int32, z.shape, 0)
    shifted = jnp.where(row == 0, prev, pltpu.roll(z, 1, 0))
    zs = z + (shifted - z) * mu_ref[...]
    r = zs[:, 0:c]
    k = zs[:, c:2 * c]
    v = zs[:, 2 * c:3 * c]
    zwa = zs[:, 3 * c:3 * c + 128]
    zg = zs[:, 3 * c + 128:3 * c + 256]
    lane = lax.broadcasted_iota(jnp.int32, (tm, 128), 1)
    lora_in = jnp.where(lane < 64, jnp.tanh(zwa), zwa)
    up = jnp.dot(lora_in, wlora_ref[...], preferred_element_type=F32, precision=HI)
    u = w0_ref[...] + up[:, 0:c]
    a = _sigmoid(a0_ref[...] + up[:, c:2 * c])
    g = jnp.dot(_sigmoid(zg), wgate_ref[...], preferred_element_type=F32, precision=HI)
    lw = -DECAY_SCALE * _sigmoid(u)
    kk = k * kk_ref[...]
    kn = kk / jnp.maximum(jnp.sqrt(_head_sums(kk * kk)), 1e-12)
    kf = k * (1.0 + (a - 1.0) * ka_ref[...])
    r_ref[...] = r
    lw_ref[...] = lw
    kf_ref[...] = kf
    v_ref[...] = v
    kn_ref[...] = kn
    ba_ref[...] = kn * a
    g_ref[...] = g


def _rwkv_prep(z, seq, c, mix_mu, w_lora, w_gate, w0, a0, k_k, k_a, tm):
    n, zc = z.shape
    row_spec = pl.BlockSpec((tm, c), lambda i: (i, 0))
    out = jax.ShapeDtypeStruct((n, c), F32)
    return pl.pallas_call(
        functools.partial(_prep_kernel, seq // tm, c),
        grid=(n // tm,),
        in_specs=[pl.BlockSpec((tm, zc), lambda i: (i, 0)),
                  pl.BlockSpec((8, zc), lambda i: (jnp.maximum(i * (tm // 8) - 1, 0), 0)),
                  _resident((1, zc)), _resident((128, 2 * c)), _resident((128, c)),
                  _resident((1, c)), _resident((1, c)), _resident((1, c)), _resident((1, c))],
        out_specs=[row_spec] * 7,
        out_shape=[out] * 7,
        compiler_params=_cparams(("parallel",)),
        name="rwkv_prep",
    )(z, z, mix_mu, w_lora, w_gate, w0, a0, k_k, k_a)


def _split3(x):
    h1 = x.astype(BF16)
    r1 = x - h1.astype(F32)
    h2 = r1.astype(BF16)
    h3 = (r1 - h2.astype(F32)).astype(BF16)
    return h1, h2, h3


def _mm(a, b):
    return jnp.dot(a, b, preferred_element_type=F32)


def _mm_nt(a, b):
    return lax.dot_general(a, b, (((1,), (1,)), ((), ())), preferred_element_type=F32)


def _mm_tn(a, b):
    return lax.dot_general(a, b, (((0,), (0,)), ((), ())), preferred_element_type=F32)


GROUP_HEADS = 4
GROUP_LANES = GROUP_HEADS * HEAD


def _block_diag(x, mask):
    return jnp.concatenate([x.astype(BF16)] * GROUP_HEADS, axis=0) * mask


def _scan_kernel(r_ref, lw_ref, kf_ref, v_ref, kn_ref, ba_ref, g_ref, rk_ref, lng_ref, lnb_ref, mask_ref, y_ref,
                 s_ref):
    @pl.when(pl.program_id(0) == 0)
    def _():
        s_ref[...] = jnp.zeros_like(s_ref)

    t = CHUNK
    nb = r_ref.shape[0]
    c = r_ref.shape[2]
    gl = GROUP_LANES
    n_groups = c // gl
    mask = mask_ref[...]
    ti = lax.broadcasted_iota(jnp.int32, (t, t), 0)
    si = lax.broadcasted_iota(jnp.int32, (t, t), 1)
    tri = jnp.where(si <= ti, 1.0, 0.0).astype(BF16)
    row = lax.broadcasted_iota(jnp.int32, (t, gl), 0)
    col = lax.broadcasted_iota(jnp.int32, (t, gl), 1) & (HEAD - 1)
    strict = col < row
    incl = col <= row
    eye = jnp.where(col == row, 1.0, 0.0)

    prep = []
    for b in range(nb):
        lw = lw_ref[b]
        cum = sum(jnp.dot(tri, part, preferred_element_type=F32) for part in _split3(lw))
        cum_end = cum[t - 1:t, :]
        e_pos = jnp.exp(cum)
        e_neg = jnp.exp(-cum)
        e_end = jnp.exp(cum_end - cum)
        r = r_ref[b]
        kf = kf_ref[b]
        ba = ba_ref[b]
        prep.append(dict(
            rt=r * e_pos,
            at=-kn_ref[b] * jnp.exp(cum - lw),
            bt=ba * e_neg,
            kt=kf * e_neg,
            bh=ba * e_end,
            kh=kf * e_end,
            v=v_ref[b],
            w_end=jnp.exp(cum_end),
            rkf=r * kf * rk_ref[...]))

    probs = [(b, gi) for b in range(nb) for gi in range(n_groups)]

    def part(b, gi, name):
        return prep[b][name][:, gi * gl:(gi + 1) * gl]

    a_ab, a_ak, a_rb, a_rk = [], [], [], []
    for b, gi in probs:
        lhs = jnp.concatenate([part(b, gi, "at"), part(b, gi, "rt")], axis=0).astype(BF16)
        rhs = jnp.concatenate([_block_diag(part(b, gi, "bt"), mask), _block_diag(part(b, gi, "kt"), mask)], axis=0)
        amat = _mm_nt(lhs, rhs)
        a_ab.append(jnp.where(strict, amat[0:t, 0:gl], 0.0))
        a_ak.append(jnp.where(strict, amat[0:t, gl:2 * gl], 0.0))
        a_rb.append(jnp.where(incl, amat[t:2 * t, 0:gl], 0.0))
        a_rk.append(jnp.where(incl, amat[t:2 * t, gl:2 * gl], 0.0))

    vbd = [_block_diag(part(b, gi, "v"), mask) for b, gi in probs]
    av = [_mm(a_ak[i].astype(BF16), vbd[i]) for i in range(len(probs))]
    minv = [eye + a for a in a_ab]
    power = [_mm(a.astype(BF16), _block_diag(a, mask)) for a in a_ab]
    span = 2
    while span < t:
        last = span * 2 >= t
        for i in range(len(probs)):
            pbd = _block_diag(power[i], mask)
            if last:
                minv[i] = minv[i] + _mm(minv[i].astype(BF16), pbd)
            else:
                both = _mm(jnp.concatenate([power[i], minv[i]], axis=0).astype(BF16), pbd)
                power[i] = both[0:t]
                minv[i] = minv[i] + both[t:2 * t]
        span *= 2

    s0 = [s_ref[b, gi] for b, gi in probs]
    sbd = [_block_diag(s, mask) for s in s0]
    ps = []
    for i, (b, gi) in enumerate(probs):
        lhs = jnp.concatenate([part(b, gi, "at"), part(b, gi, "rt")], axis=0).astype(BF16)
        ps.append(_mm_nt(lhs, sbd[i]))
    u = [_mm(minv[i].astype(BF16), _block_diag(ps[i][0:t] + av[i], mask)) for i in range(len(probs))]
    ys = []
    for i, (b, gi) in enumerate(probs):
        lhs = jnp.concatenate([a_rb[i], a_rk[i]], axis=1).astype(BF16)
        rhs = jnp.concatenate([_block_diag(u[i], mask), vbd[i]], axis=0)
        ys.append(ps[i][t:2 * t] + _mm(lhs, rhs))
        uv = jnp.concatenate([u[i], part(b, gi, "v")], axis=0).astype(BF16)
        bk = jnp.concatenate([part(b, gi, "bh"), part(b, gi, "kh")], axis=0).astype(BF16)
        full = _mm_tn(uv, bk) * mask.astype(F32)
        upd = full[0:HEAD]
        for hh in range(1, GROUP_HEADS):
            upd = upd + full[hh * HEAD:(hh + 1) * HEAD]
        s_ref[b, gi] = s0[i] * part(b, gi, "w_end") + upd

    for b in range(nb):
        y = jnp.concatenate([ys[b * n_groups + gi] for gi in range(n_groups)], axis=1)
        mu = _head_sums(y) * (1.0 / HEAD)
        yc = y - mu
        var = _head_sums(yc * yc) * (1.0 / HEAD)
        yn = yc * lax.rsqrt(var + LNX_EPS)
        bonus = _head_sums(prep[b]["rkf"]) * prep[b]["v"]
        y_ref[b] = ((yn * lng_ref[...] + lnb_ref[...] + bonus) * g_ref[b]).astype(y_ref.dtype)


def _rwkv_scan(r, lw, kf, v, kn, ba, g, r_k, lnx_g, lnx_b):
    b, s, c = r.shape
    blk = pl.BlockSpec((b, CHUNK, c), lambda ci: (0, ci, 0))
    hid = np.arange(GROUP_LANES) // HEAD
    mask = jnp.asarray(hid[:, None] == hid[None, :], BF16)
    return pl.pallas_call(
        _scan_kernel,
        grid=(s // CHUNK,),
        in_specs=[blk] * 7 + [_resident((1, c))] * 3 + [_resident((GROUP_LANES, GROUP_LANES))],
        out_specs=blk,
        out_shape=jax.ShapeDtypeStruct((b, s, c), BF16),
        scratch_shapes=[pltpu.VMEM((b, c // GROUP_LANES, HEAD, GROUP_LANES), F32)],
        compiler_params=_cparams(("arbitrary",)),
        name="rwkv_scan",
    )(r, lw, kf, v, kn, ba, g, r_k, lnx_g, lnx_b, mask)


def _attn_kernel(q_ref, kc_ref, kp_ref, vc_ref, vp_ref, sink_ref, o_ref):
    first = pl.program_id(1) == 0
    w = WINDOW
    group = q_ref.shape[-1] // HEAD // KV_HEADS
    row = lax.broadcasted_iota(jnp.int32, (group * w, 2 * w), 0)
    qi = row & (w - 1)
    kj = lax.broadcasted_iota(jnp.int32, (group * w, 2 * w), 1)
    lo = jnp.where(first, jnp.maximum(qi, w - 1), qi)
    valid = (kj > lo) & (kj <= qi + w)
    grow = lax.broadcasted_iota(jnp.int32, (group * w, 1), 0) // w
    q = q_ref[...]
    outs = []
    for hk in range(KV_HEADS):
        ksl = slice(hk * HEAD, (hk + 1) * HEAD)
        kcat = jnp.concatenate([kp_ref[:, ksl], kc_ref[:, ksl]], axis=0)
        vcat = jnp.concatenate([vp_ref[:, ksl], vc_ref[:, ksl]], axis=0)
        qg = jnp.concatenate([q[:, (hk * group + gi) * HEAD:(hk * group + gi + 1) * HEAD] for gi in range(group)],
                             axis=0)
        sink = jnp.zeros((group * w, 1), F32)
        for gi in range(group):
            sink = jnp.where(grow == gi, sink_ref[hk * group + gi:hk * group + gi + 1, 0:1], sink)
        s = _mm_nt(qg, kcat) * (HEAD ** -0.5)
        s = jnp.where(valid, s, -1e30)
        m = jnp.maximum(jnp.max(s, axis=-1, keepdims=True), sink)
        p = jnp.exp(s - m)
        pb = p.astype(BF16)
        psum = _mm(pb, jnp.ones((2 * w, HEAD), BF16))
        o = _mm(pb, vcat) / (psum + jnp.exp(sink - m))
        for gi in range(group):
            outs.append(o[gi * w:(gi + 1) * w, :])
    o_ref[...] = jnp.concatenate(outs, axis=-1).astype(o_ref.dtype)


def _attention(q, k, v, sinks_b):
    b, s, qc = q.shape
    kc = k.shape[-1]
    nb = s // WINDOW
    cur = lambda bi, i: (bi, i, 0)
    prev = lambda bi, i: (bi, jnp.maximum(i - 1, 0), 0)
    return pl.pallas_call(
        _attn_kernel,
        grid=(b, nb),
        in_specs=[pl.BlockSpec((None, WINDOW, qc), cur),
                  pl.BlockSpec((None, WINDOW, kc), cur), pl.BlockSpec((None, WINDOW, kc), prev),
                  pl.BlockSpec((None, WINDOW, kc), cur), pl.BlockSpec((None, WINDOW, kc), prev),
                  _resident(sinks_b.shape)],
        out_specs=pl.BlockSpec((None, WINDOW, qc), cur),
        out_shape=jax.ShapeDtypeStruct((b, s, qc), BF16),
        compiler_params=_cparams(("parallel", "parallel")),
        name="swa_attention",
    )(q, k, k, v, v, sinks_b)


MERGE_SPLIT = 2


def _merge_kernel(n_exp, x_ref, yr_ref, ya_ref, gr_ref, ga_ref, wur_ref, wua_ref, wo_ref, n2_ref, wr_ref, br_ref,
                  x1_ref, h2_ref, gate_ref, idx_ref):
    tm = x_ref.shape[0] // MERGE_SPLIT
    parts = [slice(p * tm, (p + 1) * tm) for p in range(MERGE_SPLIT)]
    ups = [(_mm(yr_ref[rs, :], wur_ref[...]), _mm(ya_ref[rs, :], wua_ref[...])) for rs in parts]
    merged = [_sigmoid(gr_ref[rs, :].astype(F32)) * ur + _sigmoid(ga_ref[rs, :].astype(F32)) * ua
              for rs, (ur, ua) in zip(parts, ups)]
    x1s = [x_ref[rs, :] + _mm(m.astype(BF16), wo_ref[...]) for rs, m in zip(parts, merged)]
    h2s = []
    for rs, x1 in zip(parts, x1s):
        x1_ref[rs, :] = x1
        h2 = _rmsnorm(x1, n2_ref[...])
        h2_ref[rs, :] = h2
        h2s.append(h2)
    all_logits = [jnp.dot(h2, wr_ref[...], preferred_element_type=F32, precision=HI) + br_ref[...] for h2 in h2s]
    lane_e = lax.broadcasted_iota(jnp.int32, (tm, n_exp), 1).astype(F32)
    lane_o = lax.broadcasted_iota(jnp.int32, (tm, 128), 1)
    for rs, logits in zip(parts, all_logits):
        vals, idxs = [], []
        for _ in range(TOP_K):
            mx = jnp.max(logits, axis=-1, keepdims=True)
            ix = jnp.min(jnp.where(logits == mx, lane_e, float(n_exp)), axis=-1, keepdims=True)
            vals.append(mx)
            idxs.append(ix)
            logits = jnp.where(lane_e == ix, -jnp.inf, logits)
        exps = [jnp.exp(vk - vals[0]) for vk in vals]
        tot = exps[0] + exps[1] + exps[2] + exps[3]
        gate_o, idx_o = jnp.zeros((tm, 128), F32), jnp.zeros((tm, 128), F32)
        for kk in range(TOP_K):
            gate_o = jnp.where(lane_o == kk, exps[kk] / tot, gate_o)
            idx_o = jnp.where(lane_o == kk, idxs[kk], idx_o)
        gate_ref[rs, :] = gate_o
        idx_ref[rs, :] = idx_o.astype(jnp.int32)


def _merge(x, y_rwkv, y_attn, gates, w_up_r, w_up_a, w_out, norm2_g, w_router, b_router, tm):
    n, d = x.shape
    c = y_rwkv.shape[1]
    n_exp = w_router.shape[1]
    row = lambda cols, j=0: pl.BlockSpec((tm, cols), lambda i: (i, j))
    return pl.pallas_call(
        functools.partial(_merge_kernel, n_exp),
        grid=(n // tm,),
        in_specs=[row(d), row(c), row(c), row(d, 0), row(d, 1),
                  _resident((c, d)), _resident((c, d)), _resident((d, d)), _resident((1, d)),
                  _resident((d, n_exp)), _resident((1, n_exp))],
        out_specs=[row(d), row(d), row(128), row(128)],
        out_shape=[jax.ShapeDtypeStruct((n, d), F32), jax.ShapeDtypeStruct((n, d), F32),
                   jax.ShapeDtypeStruct((n, 128), F32), jax.ShapeDtypeStruct((n, 128), jnp.int32)],
        compiler_params=_cparams(("parallel",)),
        name="merge_router",
    )(x, y_rwkv, y_attn, gates, gates, w_up_r, w_up_a, w_out, norm2_g, w_router, b_router)


GATHER_UNROLL = 8


def _start_row_gather(src_hbm, idx_ref, buf, sem):
    rows = buf.shape[0]

    def start(blk, carry):
        for u in range(GATHER_UNROLL):
            r = blk * GATHER_UNROLL + u
            pltpu.make_async_copy(src_hbm.at[pl.ds(idx_ref[0, 0, r], 1)], buf.at[pl.ds(r, 1)], sem).start()
        return carry

    lax.fori_loop(0, rows // GATHER_UNROLL, start, 0)


def _wait_row_gather(buf, sem):
    pltpu.make_async_copy(buf, buf, sem).wait()


def _dispatch_kernel(brows_ref, tok_ref, tok_next_ref, h_hbm, o_ref, buf, sems):
    s = pl.program_id(0)
    n_sub = ROW_BLOCK // ROW_SUB

    def has_rows(t):
        return lax.rem(t, n_sub) * ROW_SUB < brows_ref[lax.div(t, n_sub)]

    slot = s % 2

    @pl.when((s == 0) & has_rows(0))
    def _():
        _start_row_gather(h_hbm, tok_ref, buf.at[0], sems.at[0])

    @pl.when((s + 1 < pl.num_programs(0)) & has_rows(jnp.minimum(s + 1, pl.num_programs(0) - 1)))
    def _():
        _start_row_gather(h_hbm, tok_next_ref, buf.at[1 - slot], sems.at[1 - slot])

    @pl.when(has_rows(s))
    def _():
        _wait_row_gather(buf.at[slot], sems.at[slot])
        o_ref[...] = buf[slot].astype(o_ref.dtype)

    @pl.when(jnp.logical_not(has_rows(s)))
    def _():
        o_ref[...] = jnp.zeros_like(o_ref)


def _dispatch(h2, row_tok, block_rows, n_blocks):
    d = h2.shape[1]
    steps = n_blocks * (ROW_BLOCK // ROW_SUB)
    grid_spec = pltpu.PrefetchScalarGridSpec(
        num_scalar_prefetch=1,
        grid=(steps,),
        in_specs=[pl.BlockSpec((1, 1, ROW_SUB), lambda s, br: (s, 0, 0), memory_space=pltpu.SMEM),
                  pl.BlockSpec((1, 1, ROW_SUB), lambda s, br: (jnp.minimum(s + 1, steps - 1), 0, 0),
                               memory_space=pltpu.SMEM),
                  pl.BlockSpec(memory_space=pl.ANY)],
        out_specs=pl.BlockSpec((ROW_SUB, d), lambda s, br: (s, 0)),
        scratch_shapes=[pltpu.VMEM((2, ROW_SUB, d), F32), pltpu.SemaphoreType.DMA((2,))],
    )
    tok = row_tok.reshape(steps, 1, ROW_SUB)
    return pl.pallas_call(
        _dispatch_kernel,
        grid_spec=grid_spec,
        out_shape=jax.ShapeDtypeStruct((n_blocks * ROW_BLOCK, d), BF16),
        compiler_params=_cparams(("arbitrary",)),
        name="moe_dispatch",
    )(block_rows, tok, tok, h2)


def _ffn_kernel(nvalid_ref, bexp_ref, brows_ref, xs_ref, w1g_ref, w1l_ref, b1g_ref, b1l_ref, w2_ref, b2_ref, o_ref,
                act_ref):
    i = pl.program_id(0)
    j = pl.program_id(1)
    n_ff = act_ref.shape[0]
    rows_valid = brows_ref[i]
    n_sub = ROW_BLOCK // ROW_SUB

    @pl.when(j < n_ff)
    def _():
        w1g = w1g_ref[...].astype(BF16)
        w1l = w1l_ref[...].astype(BF16)
        for sb in range(n_sub):
            rs = slice(sb * ROW_SUB, (sb + 1) * ROW_SUB)

            @pl.when(sb * ROW_SUB < rows_valid)
            def _():
                x = xs_ref[rs, :]
                hg = _mm(x, w1g) + b1g_ref[...]
                hl = _mm(x, w1l) + b1l_ref[...]
                glu = jnp.minimum(hg, SWIGLU_LIMIT)
                lin = jnp.clip(hl, -SWIGLU_LIMIT, SWIGLU_LIMIT)
                act_ref[j, rs, :] = (glu * _sigmoid(SWIGLU_ALPHA * glu) * (lin + 1.0)).astype(BF16)

    @pl.when(j >= n_ff)
    def _():
        w2 = w2_ref[...].astype(BF16)
        for sb in range(n_sub):
            rs = slice(sb * ROW_SUB, (sb + 1) * ROW_SUB)

            @pl.when(sb * ROW_SUB < rows_valid)
            def _():
                acc = _mm(act_ref[0, rs, :], w2[0:FF_TILE])
                for jf in range(1, n_ff):
                    acc = acc + _mm(act_ref[jf, rs, :], w2[jf * FF_TILE:(jf + 1) * FF_TILE])
                o_ref[rs, :] = acc + b2_ref[...]

            @pl.when(sb * ROW_SUB >= rows_valid)
            def _():
                o_ref[rs, :] = jnp.zeros((ROW_SUB, o_ref.shape[1]), o_ref.dtype)


def _expert_ffn(xs, w1, b1, w2, b2, block_expert, block_rows, n_valid, n_blocks):
    d = xs.shape[1]
    n_exp, _, ff2 = w1.shape
    ff = ff2 // 2
    n_ff = ff // FF_TILE
    n_out = d // OUT_TILE

    def hid(i, j, nv):
        return jnp.where(i < nv[0], jnp.minimum(j, n_ff - 1), n_ff - 1)

    def out(i, j, nv):
        return jnp.where(i < nv[0], jnp.maximum(j - n_ff, 0), n_out - 1)

    grid_spec = pltpu.PrefetchScalarGridSpec(
        num_scalar_prefetch=3,
        grid=(n_blocks, n_ff + n_out),
        in_specs=[pl.BlockSpec((ROW_BLOCK, d), lambda i, j, nv, be, br: (i, 0)),
                  pl.BlockSpec((None, d, FF_TILE), lambda i, j, nv, be, br: (be[i], 0, hid(i, j, nv))),
                  pl.BlockSpec((None, d, FF_TILE), lambda i, j, nv, be, br: (be[i], 0, n_ff + hid(i, j, nv))),
                  pl.BlockSpec((None, 1, FF_TILE), lambda i, j, nv, be, br: (be[i], 0, hid(i, j, nv))),
                  pl.BlockSpec((None, 1, FF_TILE), lambda i, j, nv, be, br: (be[i], 0, n_ff + hid(i, j, nv))),
                  pl.BlockSpec((None, ff, OUT_TILE), lambda i, j, nv, be, br: (be[i], 0, out(i, j, nv))),
                  pl.BlockSpec((None, 1, OUT_TILE), lambda i, j, nv, be, br: (be[i], 0, out(i, j, nv)))],
        out_specs=pl.BlockSpec((ROW_BLOCK, OUT_TILE), lambda i, j, nv, be, br: (i, jnp.maximum(j - n_ff, 0))),
        scratch_shapes=[pltpu.VMEM((n_ff, ROW_BLOCK, FF_TILE), BF16)],
    )
    return pl.pallas_call(
        _ffn_kernel,
        grid_spec=grid_spec,
        out_shape=jax.ShapeDtypeStruct((n_blocks * ROW_BLOCK, d), F32),
        compiler_params=_cparams(("arbitrary", "arbitrary")),
        name="moe_ffn",
    )(n_valid, block_expert, block_rows, xs, w1, w1, b1.reshape(n_exp, 1, ff2), b1.reshape(n_exp, 1, ff2), w2,
      b2.reshape(n_exp, 1, d))


def _combine_kernel(pos_ref, pos_next_ref, x1_ref, gate_ref, nf_ref, ys_hbm, o_ref, buf, sems):
    s = pl.program_id(0)
    tc = x1_ref.shape[0]
    slot = s % 2

    @pl.when(s == 0)
    def _():
        _start_row_gather(ys_hbm, pos_ref, buf.at[0], sems.at[0])

    @pl.when(s + 1 < pl.num_programs(0))
    def _():
        _start_row_gather(ys_hbm, pos_next_ref, buf.at[1 - slot], sems.at[1 - slot])

    _wait_row_gather(buf.at[slot], sems.at[slot])
    acc = x1_ref[...]
    gate = gate_ref[...]
    for k in range(TOP_K):
        acc = acc + gate[:, k:k + 1] * buf[slot, k * tc:(k + 1) * tc, :]
    o_ref[...] = _rmsnorm(acc, nf_ref[...])


def _combine(x1, gate, pos, ys, normf_g, tc):
    n, d = x1.shape
    nt = n // tc
    return pl.pallas_call(
        _combine_kernel,
        grid=(nt,),
        in_specs=[pl.BlockSpec((1, 1, TOP_K * tc), lambda i: (i, 0, 0), memory_space=pltpu.SMEM),
                  pl.BlockSpec((1, 1, TOP_K * tc), lambda i: (jnp.minimum(i + 1, nt - 1), 0, 0),
                               memory_space=pltpu.SMEM),
                  pl.BlockSpec((tc, d), lambda i: (i, 0)), pl.BlockSpec((tc, 128), lambda i: (i, 0)),
                  _resident((1, d)), pl.BlockSpec(memory_space=pl.ANY)],
        out_specs=pl.BlockSpec((tc, d), lambda i: (i, 0)),
        out_shape=jax.ShapeDtypeStruct((n, d), F32),
        scratch_shapes=[pltpu.VMEM((2, TOP_K * tc, d), F32), pltpu.SemaphoreType.DMA((2,))],
        compiler_params=_cparams(("arbitrary",)),
        name="moe_combine",
    )(pos, pos, x1, gate, normf_g, ys)


def _route(top_idx, n_exp, tc):
    n = top_idx.shape[0]
    n_assign = n * TOP_K
    n_blocks = n_assign // ROW_BLOCK + n_exp
    tile = min(1024, n)
    hot = (top_idx[:, :, None] == jnp.arange(n_exp, dtype=jnp.int32)).any(axis=1).reshape(n // tile, tile, n_exp)
    earlier = jnp.asarray(np.tril(np.ones((tile, tile), np.float32), -1), BF16)
    within = jnp.einsum("ts,bse->bte", earlier, hot.astype(BF16), preferred_element_type=F32)
    tile_tot = jnp.sum(hot, axis=1, dtype=jnp.int32)
    tile_off = jnp.cumsum(tile_tot, axis=0) - tile_tot
    rank = (within.astype(jnp.int32) + tile_off[:, None, :]).reshape(n, n_exp)
    counts = jnp.sum(tile_tot, axis=0)
    padded = (counts + ROW_BLOCK - 1) // ROW_BLOCK * ROW_BLOCK
    pad_end = jnp.cumsum(padded)
    pad_start = pad_end - padded
    grp_start = jnp.cumsum(counts) - counts
    slot = jnp.take_along_axis(rank + pad_start[None, :], top_idx, axis=1).reshape(-1)
    block_start = jnp.arange(n_blocks, dtype=jnp.int32) * ROW_BLOCK
    block_expert = jnp.minimum(jnp.searchsorted(pad_end, block_start, side="right"), n_exp - 1).astype(jnp.int32)
    n_valid = (pad_end[-1] // ROW_BLOCK).astype(jnp.int32).reshape(1)
    order = jnp.argsort(top_idx.reshape(-1), stable=True)
    row_e = jnp.repeat(block_expert, ROW_BLOCK)
    row_r = jnp.arange(n_blocks * ROW_BLOCK, dtype=jnp.int32) - pad_start[row_e]
    src = jnp.clip(grp_start[row_e] + row_r, 0, n_assign - 1)
    row_tok = jnp.where((row_r >= 0) & (row_r < counts[row_e]), order[src] // TOP_K, 0).astype(jnp.int32)
    pos = slot.reshape(n // tc, tc, TOP_K).transpose(0, 2, 1).reshape(n // tc, 1, TOP_K * tc)
    block_rows = jnp.clip(counts[block_expert] - (block_start - pad_start[block_expert]), 0, ROW_BLOCK)
    block_rows = jnp.where(jnp.arange(n_blocks) < n_valid[0], block_rows, 0).astype(jnp.int32)
    return row_tok, block_expert, block_rows, n_valid, pos, n_blocks


def _layer(x, norm1_g, w_in, mix_mu, w0, w_decay_up, a0, w_iclr_up, w_gate_up, k_k, k_a, r_k, lnx_g, lnx_b, b_qkv,
           sinks, w_up_rwkv, w_up_attn, w_out, norm2_g, w_router, b_router, w1, b1, w2, b2, normf_g):
    b, s, d = x.shape
    n = b * s
    c = w_up_rwkv.shape[0]
    lora = w_decay_up.shape[0]
    qc = w_up_attn.shape[0]
    kvc = KV_HEADS * HEAD
    rwkv_cols = 3 * c + 2 * lora + w_gate_up.shape[0]
    qkv_cols = qc + 2 * kvc
    row = lambda t: t.reshape(1, -1).astype(F32)
    xf = x.reshape(n, d)
    g1 = row(norm1_g)
    w_in_b = w_in.astype(BF16)

    z_rwkv = _norm_proj(xf, g1, w_in_b[:, :rwkv_cols], jnp.zeros((1, rwkv_cols), F32), F32, 512)
    qkv = _norm_proj(xf, g1, w_in_b[:, rwkv_cols:rwkv_cols + qkv_cols], row(b_qkv), BF16, 512)
    gates = _norm_proj(xf, g1, w_in_b[:, rwkv_cols + qkv_cols:], jnp.zeros((1, 2 * d), F32), BF16, 512)

    zl = jnp.zeros((lora, c), F32)
    w_lora = jnp.concatenate([jnp.concatenate([w_decay_up, zl], axis=1), jnp.concatenate([zl, w_iclr_up], axis=1)],
                             axis=0)
    r, lw, kf, v, kn, ba, g = _rwkv_prep(z_rwkv, s, c, row(mix_mu), w_lora, w_gate_up, row(w0), row(a0), row(k_k),
                                         row(k_a), 256)
    as3 = lambda t: t.reshape(b, s, c)
    y_rwkv = _rwkv_scan(as3(r), as3(lw), as3(kf), as3(v), as3(kn), as3(ba), as3(g), row(r_k), row(lnx_g),
                        row(lnx_b)).reshape(n, c)

    q = qkv[:, :qc].reshape(b, s, qc)
    ka = qkv[:, qc:qc + kvc].reshape(b, s, kvc)
    va = qkv[:, qc + kvc:].reshape(b, s, kvc)
    sinks_b = jnp.broadcast_to(sinks.astype(F32).reshape(-1, 1), (sinks.shape[0], 128))
    y_attn = _attention(q, ka, va, sinks_b).reshape(n, qc)

    x1, h2, gate, top_idx = _merge(xf, y_rwkv, y_attn, gates, w_up_rwkv.astype(BF16), w_up_attn.astype(BF16),
                                   w_out.astype(BF16), row(norm2_g), w_router, row(b_router), 256)

    tc = 256
    row_tok, block_expert, block_rows, n_valid, pos, n_blocks = _route(top_idx[:, :TOP_K], w_router.shape[1], tc)
    xs = _dispatch(h2, row_tok, block_rows, n_blocks)
    ys = _expert_ffn(xs, w1, b1, w2, b2, block_expert, block_rows, n_valid, n_blocks)
    out = _combine(x1, gate, pos, ys, row(normf_g), tc)
    return out.reshape(b, s, d)


def kernel(x, norm1_g, w_in, mix_mu, w0, w_decay_up, a0, w_iclr_up, w_gate_up, k_k, k_a, r_k, lnx_g, lnx_b, b_qkv,
           sinks, w_up_rwkv, w_up_attn, w_out, norm2_g, w_router, b_router, w1, b1, w2, b2, normf_g):
    assert w_in.shape[0] == 1, "single-layer block"
    return _layer(x, norm1_g[0], w_in[0], mix_mu[0], w0[0], w_decay_up[0], a0[0], w_iclr_up[0], w_gate_up[0],
                  k_k[0], k_a[0], r_k[0], lnx_g[0], lnx_b[0], b_qkv[0], sinks[0], w_up_rwkv[0], w_up_attn[0],
                  w_out[0], norm2_g[0], w_router[0], b_router[0], w1[0], b1[0], w2[0], b2[0], normf_g)
```

```python
import functools

import jax
import jax.numpy as jnp
import numpy as np
from jax import lax
from jax.experimental import pallas as pl
from jax.experimental.pallas import tpu as pltpu

F32 = jnp.float32
BF16 = jnp.bfloat16

NORM_EPS = 1e-5
LNX_EPS = 64e-5
HEAD = 64
WINDOW = 128
KV_HEADS = 4
TOP_K = 4
SWIGLU_LIMIT = 7.0
SWIGLU_ALPHA = 1.702
DECAY_SCALE = float(np.exp(-0.5))

CHUNK = 64
ROW_BLOCK = 1024
ROW_SUB = 512
FF_TILE = 512
OUT_TILE = 256
VMEM_LIMIT = 56 * 1024 * 1024
HI = lax.Precision.HIGHEST


def _cparams(sem):
    return pltpu.CompilerParams(dimension_semantics=sem, vmem_limit_bytes=VMEM_LIMIT)


def _resident(shape):
    nd = len(shape)
    return pl.BlockSpec(shape, lambda *_: (0,) * nd, pipeline_mode=pl.Buffered(1))


def _rmsnorm(x, g):
    return x * lax.rsqrt(jnp.mean(x * x, axis=-1, keepdims=True) + NORM_EPS) * g


def _sigmoid(x):
    return 1.0 / (1.0 + jnp.exp(-x))


def _pack_bf16_pair(lo, hi):
    lo_bits = lax.bitcast_convert_type(lo.astype(BF16).astype(F32), jnp.uint32)
    hi_bits = lax.bitcast_convert_type(hi.astype(BF16).astype(F32), jnp.uint32)
    return (hi_bits & jnp.uint32(0xFFFF0000)) | (lo_bits >> 16)


def _unpack_bf16_pair(packed):
    lo = lax.bitcast_convert_type(packed << 16, F32)
    hi = lax.bitcast_convert_type(packed & jnp.uint32(0xFFFF0000), F32)
    return lo, hi


def _norm_proj_kernel(x_ref, g_ref, w_ref, b_ref, o_ref):
    h = _rmsnorm(x_ref[...], g_ref[...]).astype(BF16)
    z = jnp.dot(h, w_ref[...], preferred_element_type=F32) + b_ref[...]
    o_ref[...] = z.astype(o_ref.dtype)


def _norm_proj(x, g, w, b, out_dtype, tm):
    n, d = x.shape
    cols = w.shape[1]
    return pl.pallas_call(
        _norm_proj_kernel,
        grid=(n // tm,),
        in_specs=[pl.BlockSpec((tm, d), lambda i: (i, 0)), _resident((1, d)), _resident((d, cols)),
                  _resident((1, cols))],
        out_specs=pl.BlockSpec((tm, cols), lambda i: (i, 0)),
        out_shape=jax.ShapeDtypeStruct((n, cols), out_dtype),
        compiler_params=_cparams(("parallel",)),
        name="norm_proj",
    )(x, g, w, b)


def _head_sums(x):
    rows, c = x.shape
    lane = lax.broadcasted_iota(jnp.int32, (rows, 128), 1)
    low = lane < HEAD
    parts = []
    for gi in range(c // 128):
        xg = x[:, gi * 128:(gi + 1) * 128]
        s_lo = jnp.sum(jnp.where(low, xg, 0.0), axis=-1, keepdims=True)
        s_hi = jnp.sum(jnp.where(low, 0.0, xg), axis=-1, keepdims=True)
        parts.append(jnp.where(low, s_lo, s_hi))
    return jnp.concatenate(parts, axis=-1)


def _prep_kernel(seq_blocks, c, z_ref, zp_ref, mu_ref, wlora_ref, wgate_ref, w0_ref, a0_ref, kk_ref, ka_ref,
                 r_ref, lw_ref, kf_ref, v_ref, kn_ref, ba_ref, g_ref):
    i = pl.program_id(0)
    z = z_ref[...]
    tm = z.shape[0]
    prev = jnp.where(i % seq_blocks == 0, 0.0, zp_ref[7:8, :])
    row = lax.broadcasted_iota(jnp.int32, z.shape, 0)
    shifted = jnp.where(row == 0, prev, pltpu.roll(z, 1, 0))
    zs = z + (shifted - z) * mu_ref[...]
    r = zs[:, 0:c]
    k = zs[:, c:2 * c]
    v = zs[:, 2 * c:3 * c]
    zwa = zs[:, 3 * c:3 * c + 128]
    zg = zs[:, 3 * c + 128:3 * c + 256]
    lane = lax.broadcasted_iota(jnp.int32, (tm, 128), 1)
    lora_in = jnp.where(lane < 64, jnp.tanh(zwa), zwa)
    up = jnp.dot(lora_in, wlora_ref[...], preferred_element_type=F32, precision=HI)
    u = w0_ref[...] + up[:, 0:c]
    a = _sigmoid(a0_ref[...] + up[:, c:2 * c])
    g = jnp.dot(_sigmoid(zg), wgate_ref[...], preferred_element_type=F32, precision=HI)
    lw = -DECAY_SCALE * _sigmoid(u)
    kk = k * kk_ref[...]
    kn = kk / jnp.maximum(jnp.sqrt(_head_sums(kk * kk)), 1e-12)
    kf = k * (1.0 + (a - 1.0) * ka_ref[...])
    r_ref[...] = r
    lw_ref[...] = lw
    kf_ref[...] = kf
    v_ref[...] = v
    kn_ref[...] = kn
    ba_ref[...] = kn * a
    g_ref[...] = g


def _rwkv_prep(z, seq, c, mix_mu, w_lora, w_gate, w0, a0, k_k, k_a, tm):
    n, zc = z.shape
    row_spec = pl.BlockSpec((tm, c), lambda i: (i, 0))
    out = jax.ShapeDtypeStruct((n, c), F32)
    return pl.pallas_call(
        functools.partial(_prep_kernel, seq // tm, c),
        grid=(n // tm,),
        in_specs=[pl.BlockSpec((tm, zc), lambda i: (i, 0)),
                  pl.BlockSpec((8, zc), lambda i: (jnp.maximum(i * (tm // 8) - 1, 0), 0)),
                  _resident((1, zc)), _resident((128, 2 * c)), _resident((128, c)),
                  _resident((1, c)), _resident((1, c)), _resident((1, c)), _resident((1, c))],
        out_specs=[row_spec] * 7,
        out_shape=[out] * 7,
        compiler_params=_cparams(("parallel",)),
        name="rwkv_prep",
    )(z, z, mix_mu, w_lora, w_gate, w0, a0, k_k, k_a)


def _split3(x):
    h1 = x.astype(BF16)
    r1 = x - h1.astype(F32)
    h2 = r1.astype(BF16)
    h3 = (r1 - h2.astype(F32)).astype(BF16)
    return h1, h2, h3


def _mm(a, b):
    return jnp.dot(a, b, preferred_element_type=F32)


def _mm_nt(a, b):
    return lax.dot_general(a, b, (((1,), (1,)), ((), ())), preferred_element_type=F32)


def _mm_tn(a, b):
    return lax.dot_general(a, b, (((0,), (0,)), ((), ())), preferred_element_type=F32)


GROUP_HEADS = 4
GROUP_LANES = GROUP_HEADS * HEAD


def _block_diag(x, mask):
    return jnp.concatenate([x.astype(BF16)] * GROUP_HEADS, axis=0) * mask


def _scan_kernel(r_ref, lw_ref, kf_ref, v_ref, kn_ref, ba_ref, g_ref, rk_ref, lng_ref, lnb_ref, mask_ref, y_ref,
                 s_ref):
    @pl.when(pl.program_id(0) == 0)
    def _():
        s_ref[...] = jnp.zeros_like(s_ref)

    t = CHUNK
    nb = r_ref.shape[0]
    c = r_ref.shape[2]
    gl = GROUP_LANES
    n_groups = c // gl
    mask = mask_ref[...]
    ti = lax.broadcasted_iota(jnp.int32, (t, t), 0)
    si = lax.broadcasted_iota(jnp.int32, (t, t), 1)
    tri = jnp.where(si <= ti, 1.0, 0.0).astype(BF16)
    row = lax.broadcasted_iota(jnp.int32, (t, gl), 0)
    col = lax.broadcasted_iota(jnp.int32, (t, gl), 1) & (HEAD - 1)
    strict = col < row
    incl = col <= row
    eye = jnp.where(col == row, 1.0, 0.0)

    prep = []
    for b in range(nb):
        lw = lw_ref[b]
        cum = sum(jnp.dot(tri, part, preferred_element_type=F32) for part in _split3(lw))
        cum_end = cum[t - 1:t, :]
        e_pos = jnp.exp(cum)
        e_neg = jnp.exp(-cum)
        e_end = jnp.exp(cum_end - cum)
        r = r_ref[b]
        kf = kf_ref[b]
        ba = ba_ref[b]
        prep.append(dict(
            rt=r * e_pos,
            at=-kn_ref[b] * jnp.exp(cum - lw),
            bt=ba * e_neg,
            kt=kf * e_neg,
            bh=ba * e_end,
            kh=kf * e_end,
            v=v_ref[b],
            w_end=jnp.exp(cum_end),
            rkf=r * kf * rk_ref[...]))

    probs = [(b, gi) for b in range(nb) for gi in range(n_groups)]

    def part(b, gi, name):
        return prep[b][name][:, gi * gl:(gi + 1) * gl]

    a_ab, a_ak, a_rb, a_rk = [], [], [], []
    for b, gi in probs:
        lhs = jnp.concatenate([part(b, gi, "at"), part(b, gi, "rt")], axis=0).astype(BF16)
        rhs = jnp.concatenate([_block_diag(part(b, gi, "bt"), mask), _block_diag(part(b, gi, "kt"), mask)], axis=0)
        amat = _mm_nt(lhs, rhs)
        a_ab.append(jnp.where(strict, amat[0:t, 0:gl], 0.0))
        a_ak.append(jnp.where(strict, amat[0:t, gl:2 * gl], 0.0))
        a_rb.append(jnp.where(incl, amat[t:2 * t, 0:gl], 0.0))
        a_rk.append(jnp.where(incl, amat[t:2 * t, gl:2 * gl], 0.0))

    vbd = [_block_diag(part(b, gi, "v"), mask) for b, gi in probs]
    av = [_mm(a_ak[i].astype(BF16), vbd[i]) for i in range(len(probs))]
    minv = [eye + a for a in a_ab]
    power = [_mm(a.astype(BF16), _block_diag(a, mask)) for a in a_ab]
    span = 2
    while span < t:
        last = span * 2 >= t
        for i in range(len(probs)):
            pbd = _block_diag(power[i], mask)
            if last:
                minv[i] = minv[i] + _mm(minv[i].astype(BF16), pbd)
            else:
                both = _mm(jnp.concatenate([power[i], minv[i]], axis=0).astype(BF16), pbd)
                power[i] = both[0:t]
                minv[i] = minv[i] + both[t:2 * t]
        span *= 2

    s0 = [s_ref[b, gi] for b, gi in probs]
    sbd = [_block_diag(s, mask) for s in s0]
    ps = []
    for i, (b, gi) in enumerate(probs):
        lhs = jnp.concatenate([part(b, gi, "at"), part(b, gi, "rt")], axis=0).astype(BF16)
        ps.append(_mm_nt(lhs, sbd[i]))
    u = [_mm(minv[i].astype(BF16), _block_diag(ps[i][0:t] + av[i], mask)) for i in range(len(probs))]
    ys = []
    for i, (b, gi) in enumerate(probs):
        lhs = jnp.concatenate([a_rb[i], a_rk[i]], axis=1).astype(BF16)
        rhs = jnp.concatenate([_block_diag(u[i], mask), vbd[i]], axis=0)
        ys.append(ps[i][t:2 * t] + _mm(lhs, rhs))
        uv = jnp.concatenate([u[i], part(b, gi, "v")], axis=0).astype(BF16)
        bk = jnp.concatenate([part(b, gi, "bh"), part(b, gi, "kh")], axis=0).astype(BF16)
        full = _mm_tn(uv, bk) * mask.astype(F32)
        upd = full[0:HEAD]
        for hh in range(1, GROUP_HEADS):
            upd = upd + full[hh * HEAD:(hh + 1) * HEAD]
        s_ref[b, gi] = s0[i] * part(b, gi, "w_end") + upd

    for b in range(nb):
        y = jnp.concatenate([ys[b * n_groups + gi] for gi in range(n_groups)], axis=1)
        mu = _head_sums(y) * (1.0 / HEAD)
        yc = y - mu
        var = _head_sums(yc * yc) * (1.0 / HEAD)
        yn = yc * lax.rsqrt(var + LNX_EPS)
        bonus = _head_sums(prep[b]["rkf"]) * prep[b]["v"]
        y_ref[b] = ((yn * lng_ref[...] + lnb_ref[...] + bonus) * g_ref[b]).astype(y_ref.dtype)


def _rwkv_scan(r, lw, kf, v, kn, ba, g, r_k, lnx_g, lnx_b):
    b, s, c = r.shape
    blk = pl.BlockSpec((b, CHUNK, c), lambda ci: (0, ci, 0))
    hid = np.arange(GROUP_LANES) // HEAD
    mask = jnp.asarray(hid[:, None] == hid[None, :], BF16)
    return pl.pallas_call(
        _scan_kernel,
        grid=(s // CHUNK,),
        in_specs=[blk] * 7 + [_resident((1, c))] * 3 + [_resident((GROUP_LANES, GROUP_LANES))],
        out_specs=blk,
        out_shape=jax.ShapeDtypeStruct((b, s, c), BF16),
        scratch_shapes=[pltpu.VMEM((b, c // GROUP_LANES, HEAD, GROUP_LANES), F32)],
        compiler_params=_cparams(("arbitrary",)),
        name="rwkv_scan",
    )(r, lw, kf, v, kn, ba, g, r_k, lnx_g, lnx_b, mask)


def _attn_kernel(q_ref, kc_ref, kp_ref, vc_ref, vp_ref, sink_ref, o_ref):
    first = pl.program_id(1) == 0
    w = WINDOW
    group = q_ref.shape[-1] // HEAD // KV_HEADS
    row = lax.broadcasted_iota(jnp.int32, (group * w, 2 * w), 0)
    qi = row & (w - 1)
    kj = lax.broadcasted_iota(jnp.int32, (group * w, 2 * w), 1)
    lo = jnp.where(first, jnp.maximum(qi, w - 1), qi)
    valid = (kj > lo) & (kj <= qi + w)
    grow = lax.broadcasted_iota(jnp.int32, (group * w, 1), 0) // w
    q = q_ref[...]
    outs = []
    for hk in range(KV_HEADS):
        ksl = slice(hk * HEAD, (hk + 1) * HEAD)
        kcat = jnp.concatenate([kp_ref[:, ksl], kc_ref[:, ksl]], axis=0)
        vcat = jnp.concatenate([vp_ref[:, ksl], vc_ref[:, ksl]], axis=0)
        qg = jnp.concatenate([q[:, (hk * group + gi) * HEAD:(hk * group + gi + 1) * HEAD] for gi in range(group)],
                             axis=0)
        sink = jnp.zeros((group * w, 1), F32)
        for gi in range(group):
            sink = jnp.where(grow == gi, sink_ref[hk * group + gi:hk * group + gi + 1, 0:1], sink)
        s = _mm_nt(qg, kcat) * (HEAD ** -0.5)
        s = jnp.where(valid, s, -1e30)
        m = jnp.maximum(jnp.max(s, axis=-1, keepdims=True), sink)
        p = jnp.exp(s - m)
        pb = p.astype(BF16)
        psum = _mm(pb, jnp.ones((2 * w, HEAD), BF16))
        o = _mm(pb, vcat) / (psum + jnp.exp(sink - m))
        for gi in range(group):
            outs.append(o[gi * w:(gi + 1) * w, :])
    o_ref[...] = jnp.concatenate(outs, axis=-1).astype(o_ref.dtype)


def _attention(q, k, v, sinks_b):
    b, s, qc = q.shape
    kc = k.shape[-1]
    nb = s // WINDOW
    cur = lambda bi, i: (bi, i, 0)
    prev = lambda bi, i: (bi, jnp.maximum(i - 1, 0), 0)
    return pl.pallas_call(
        _attn_kernel,
        grid=(b, nb),
        in_specs=[pl.BlockSpec((None, WINDOW, qc), cur),
                  pl.BlockSpec((None, WINDOW, kc), cur), pl.BlockSpec((None, WINDOW, kc), prev),
                  pl.BlockSpec((None, WINDOW, kc), cur), pl.BlockSpec((None, WINDOW, kc), prev),
                  _resident(sinks_b.shape)],
        out_specs=pl.BlockSpec((None, WINDOW, qc), cur),
        out_shape=jax.ShapeDtypeStruct((b, s, qc), BF16),
        compiler_params=_cparams(("parallel", "parallel")),
        name="swa_attention",
    )(q, k, k, v, v, sinks_b)


MERGE_SPLIT = 2


def _merge_kernel(n_exp, x_ref, yr_ref, ya_ref, gr_ref, ga_ref, wur_ref, wua_ref, wo_ref, n2_ref, wr_ref, br_ref,
                  x1_ref, h2_ref, gate_ref, idx_ref):
    tm = x_ref.shape[0] // MERGE_SPLIT
    parts = [slice(p * tm, (p + 1) * tm) for p in range(MERGE_SPLIT)]
    ups = [(_mm(yr_ref[rs, :], wur_ref[...]), _mm(ya_ref[rs, :], wua_ref[...])) for rs in parts]
    merged = [_sigmoid(gr_ref[rs, :].astype(F32)) * ur + _sigmoid(ga_ref[rs, :].astype(F32)) * ua
              for rs, (ur, ua) in zip(parts, ups)]
    x1s = [x_ref[rs, :] + _mm(m.astype(BF16), wo_ref[...]) for rs, m in zip(parts, merged)]
    h2s = []
    for rs, x1 in zip(parts, x1s):
        x1_ref[rs, :] = x1
        h2 = _rmsnorm(x1, n2_ref[...])
        half = h2.shape[1] // 2
        h2_ref[rs, :] = _pack_bf16_pair(h2[:, :half], h2[:, half:])
        h2s.append(h2)
    all_logits = [jnp.dot(h2, wr_ref[...], preferred_element_type=F32, precision=HI) + br_ref[...] for h2 in h2s]
    lane_e = lax.broadcasted_iota(jnp.int32, (tm, n_exp), 1).astype(F32)
    lane_o = lax.broadcasted_iota(jnp.int32, (tm, 128), 1)
    for rs, logits in zip(parts, all_logits):
        vals, idxs = [], []
        for _ in range(TOP_K):
            mx = jnp.max(logits, axis=-1, keepdims=True)
            ix = jnp.min(jnp.where(logits == mx, lane_e, float(n_exp)), axis=-1, keepdims=True)
            vals.append(mx)
            idxs.append(ix)
            logits = jnp.where(lane_e == ix, -jnp.inf, logits)
        exps = [jnp.exp(vk - vals[0]) for vk in vals]
        tot = exps[0] + exps[1] + exps[2] + exps[3]
        gate_o, idx_o = jnp.zeros((tm, 128), F32), jnp.zeros((tm, 128), F32)
        for kk in range(TOP_K):
            gate_o = jnp.where(lane_o == kk, exps[kk] / tot, gate_o)
            idx_o = jnp.where(lane_o == kk, idxs[kk], idx_o)
        gate_ref[rs, :] = gate_o
        idx_ref[rs, :] = idx_o.astype(jnp.int32)


def _merge(x, y_rwkv, y_attn, gates, w_up_r, w_up_a, w_out, norm2_g, w_router, b_router, tm):
    n, d = x.shape
    c = y_rwkv.shape[1]
    n_exp = w_router.shape[1]
    row = lambda cols, j=0: pl.BlockSpec((tm, cols), lambda i: (i, j))
    return pl.pallas_call(
        functools.partial(_merge_kernel, n_exp),
        grid=(n // tm,),
        in_specs=[row(d), row(c), row(c), row(d, 0), row(d, 1),
                  _resident((c, d)), _resident((c, d)), _resident((d, d)), _resident((1, d)),
                  _resident((d, n_exp)), _resident((1, n_exp))],
        out_specs=[row(d), row(d // 2), row(128), row(128)],
        out_shape=[jax.ShapeDtypeStruct((n, d), F32), jax.ShapeDtypeStruct((n, d // 2), jnp.uint32),
                   jax.ShapeDtypeStruct((n, 128), F32), jax.ShapeDtypeStruct((n, 128), jnp.int32)],
        compiler_params=_cparams(("parallel",)),
        name="merge_router",
    )(x, y_rwkv, y_attn, gates, gates, w_up_r, w_up_a, w_out, norm2_g, w_router, b_router)


GATHER_UNROLL = 8


def _start_row_gather(src_hbm, idx_ref, buf, sem):
    groups, unroll, _ = buf.shape
    assert unroll == GATHER_UNROLL

    def start(g, carry):
        for u in range(GATHER_UNROLL):
            src_row = idx_ref[0, 0, g * GATHER_UNROLL + u]
            pltpu.make_async_copy(src_hbm.at[pl.ds(src_row, 1)], buf.at[g, pl.ds(u, 1)], sem).start(priority=u % 2)
        return carry

    lax.fori_loop(0, groups, start, 0)


def _wait_row_gather(buf, sem):
    pltpu.make_async_copy(buf, buf, sem).wait()


def _dispatch_kernel(brows_ref, tok_ref, tok_next_ref, h_hbm, o_ref, buf, sems):
    s = pl.program_id(0)
    n_sub = ROW_BLOCK // ROW_SUB

    def has_rows(t):
        return lax.rem(t, n_sub) * ROW_SUB < brows_ref[lax.div(t, n_sub)]

    slot = s % 2

    @pl.when((s == 0) & has_rows(0))
    def _():
        _start_row_gather(h_hbm, tok_ref, buf.at[0], sems.at[0])

    @pl.when((s + 1 < pl.num_programs(0)) & has_rows(jnp.minimum(s + 1, pl.num_programs(0) - 1)))
    def _():
        _start_row_gather(h_hbm, tok_next_ref, buf.at[1 - slot], sems.at[1 - slot])

    @pl.when(has_rows(s))
    def _():
        _wait_row_gather(buf.at[slot], sems.at[slot])
        half = o_ref.shape[1] // 2
        lo, hi = _unpack_bf16_pair(buf[slot].reshape(ROW_SUB, half))
        o_ref[:, :half] = lo.astype(o_ref.dtype)
        o_ref[:, half:] = hi.astype(o_ref.dtype)

    @pl.when(jnp.logical_not(has_rows(s)))
    def _():
        o_ref[...] = jnp.zeros_like(o_ref)


def _dispatch(h2p, row_tok, block_rows, n_blocks):
    half = h2p.shape[1]
    d = 2 * half
    steps = n_blocks * (ROW_BLOCK // ROW_SUB)
    grid_spec = pltpu.PrefetchScalarGridSpec(
        num_scalar_prefetch=1,
        grid=(steps,),
        in_specs=[pl.BlockSpec((1, 1, ROW_SUB), lambda s, br: (s, 0, 0), memory_space=pltpu.SMEM),
                  pl.BlockSpec((1, 1, ROW_SUB), lambda s, br: (jnp.minimum(s + 1, steps - 1), 0, 0),
                               memory_space=pltpu.SMEM),
                  pl.BlockSpec(memory_space=pl.ANY)],
        out_specs=pl.BlockSpec((ROW_SUB, d), lambda s, br: (s, 0)),
        scratch_shapes=[pltpu.VMEM((2, ROW_SUB // GATHER_UNROLL, GATHER_UNROLL, half), jnp.uint32),
                        pltpu.SemaphoreType.DMA((2,))],
    )
    tok = row_tok.reshape(steps, 1, ROW_SUB)
    return pl.pallas_call(
        _dispatch_kernel,
        grid_spec=grid_spec,
        out_shape=jax.ShapeDtypeStruct((n_blocks * ROW_BLOCK, d), BF16),
        compiler_params=_cparams(("arbitrary",)),
        name="moe_dispatch",
    )(block_rows, tok, tok, h2p)


def _ffn_kernel(nvalid_ref, bexp_ref, brows_ref, xs_ref, w1g_ref, w1l_ref, b1g_ref, b1l_ref, w2a_ref, w2b_ref,
                b2a_ref, b2b_ref, o_ref, act_ref):
    i = pl.program_id(0)
    j = pl.program_id(1)
    n_ff = act_ref.shape[0]
    rows_valid = brows_ref[i]
    n_sub = ROW_BLOCK // ROW_SUB

    @pl.when(j < n_ff)
    def _():
        w1g = w1g_ref[...].astype(BF16)
        w1l = w1l_ref[...].astype(BF16)
        for sb in range(n_sub):
            rs = slice(sb * ROW_SUB, (sb + 1) * ROW_SUB)

            @pl.when(sb * ROW_SUB < rows_valid)
            def _():
                x = xs_ref[rs, :]
                hg = _mm(x, w1g) + b1g_ref[...]
                hl = _mm(x, w1l) + b1l_ref[...]
                glu = jnp.minimum(hg, SWIGLU_LIMIT)
                lin = jnp.clip(hl, -SWIGLU_LIMIT, SWIGLU_LIMIT)
                act_ref[j, rs, :] = (glu * _sigmoid(SWIGLU_ALPHA * glu) * (lin + 1.0)).astype(BF16)

    @pl.when(j >= n_ff)
    def _():
        w2a = w2a_ref[...].astype(BF16)
        w2b = w2b_ref[...].astype(BF16)
        for sb in range(n_sub):
            rs = slice(sb * ROW_SUB, (sb + 1) * ROW_SUB)

            @pl.when(sb * ROW_SUB < rows_valid)
            def _():
                def project(w2, b2_ref):
                    acc = _mm(act_ref[0, rs, :], w2[0:FF_TILE])
                    for jf in range(1, n_ff):
                        acc = acc + _mm(act_ref[jf, rs, :], w2[jf * FF_TILE:(jf + 1) * FF_TILE])
                    return acc + b2_ref[...]

                o_ref[rs, :] = _pack_bf16_pair(project(w2a, b2a_ref), project(w2b, b2b_ref))

            @pl.when(sb * ROW_SUB >= rows_valid)
            def _():
                o_ref[rs, :] = jnp.zeros((ROW_SUB, o_ref.shape[1]), o_ref.dtype)


def _expert_ffn(xs, w1, b1, w2, b2, block_expert, block_rows, n_valid, n_blocks):
    d = xs.shape[1]
    n_exp, _, ff2 = w1.shape
    ff = ff2 // 2
    n_ff = ff // FF_TILE
    n_out = d // 2 // OUT_TILE

    def hid(i, j, nv):
        return jnp.where(i < nv[0], jnp.minimum(j, n_ff - 1), n_ff - 1)

    def out(i, j, nv):
        return jnp.where(i < nv[0], jnp.maximum(j - n_ff, 0), n_out - 1)

    grid_spec = pltpu.PrefetchScalarGridSpec(
        num_scalar_prefetch=3,
        grid=(n_blocks, n_ff + n_out),
        in_specs=[pl.BlockSpec((ROW_BLOCK, d), lambda i, j, nv, be, br: (i, 0)),
                  pl.BlockSpec((None, d, FF_TILE), lambda i, j, nv, be, br: (be[i], 0, hid(i, j, nv))),
                  pl.BlockSpec((None, d, FF_TILE), lambda i, j, nv, be, br: (be[i], 0, n_ff + hid(i, j, nv))),
                  pl.BlockSpec((None, 1, FF_TILE), lambda i, j, nv, be, br: (be[i], 0, hid(i, j, nv))),
                  pl.BlockSpec((None, 1, FF_TILE), lambda i, j, nv, be, br: (be[i], 0, n_ff + hid(i, j, nv))),
                  pl.BlockSpec((None, ff, OUT_TILE), lambda i, j, nv, be, br: (be[i], 0, out(i, j, nv))),
                  pl.BlockSpec((None, ff, OUT_TILE), lambda i, j, nv, be, br: (be[i], 0, n_out + out(i, j, nv))),
                  pl.BlockSpec((None, 1, OUT_TILE), lambda i, j, nv, be, br: (be[i], 0, out(i, j, nv))),
                  pl.BlockSpec((None, 1, OUT_TILE), lambda i, j, nv, be, br: (be[i], 0, n_out + out(i, j, nv)))],
        out_specs=pl.BlockSpec((ROW_BLOCK, OUT_TILE), lambda i, j, nv, be, br: (i, jnp.maximum(j - n_ff, 0))),
        scratch_shapes=[pltpu.VMEM((n_ff, ROW_BLOCK, FF_TILE), BF16)],
    )
    b1r = b1.reshape(n_exp, 1, ff2)
    b2r = b2.reshape(n_exp, 1, d)
    return pl.pallas_call(
        _ffn_kernel,
        grid_spec=grid_spec,
        out_shape=jax.ShapeDtypeStruct((n_blocks * ROW_BLOCK, d // 2), jnp.uint32),
        compiler_params=_cparams(("arbitrary", "arbitrary")),
        name="moe_ffn",
    )(n_valid, block_expert, block_rows, xs, w1, w1, b1r, b1r, w2, w2, b2r, b2r)


def _combine_kernel(pos_ref, pos_next_ref, x1_ref, gate_ref, nf_ref, ys_hbm, o_ref, buf, sems):
    s = pl.program_id(0)
    tc = x1_ref.shape[0]
    slot = s % 2

    @pl.when(s == 0)
    def _():
        _start_row_gather(ys_hbm, pos_ref, buf.at[0], sems.at[0])

    @pl.when(s + 1 < pl.num_programs(0))
    def _():
        _start_row_gather(ys_hbm, pos_next_ref, buf.at[1 - slot], sems.at[1 - slot])

    _wait_row_gather(buf.at[slot], sems.at[slot])
    half = x1_ref.shape[1] // 2
    groups = tc // GATHER_UNROLL
    gate = gate_ref[...]
    acc_lo = x1_ref[:, :half]
    acc_hi = x1_ref[:, half:]
    for k in range(TOP_K):
        lo, hi = _unpack_bf16_pair(buf[slot, k * groups:(k + 1) * groups].reshape(tc, half))
        acc_lo = acc_lo + gate[:, k:k + 1] * lo
        acc_hi = acc_hi + gate[:, k:k + 1] * hi
    o_ref[...] = _rmsnorm(jnp.concatenate([acc_lo, acc_hi], axis=1), nf_ref[...])


def _combine(x1, gate, pos, ys, normf_g, tc):
    n, d = x1.shape
    nt = n // tc
    return pl.pallas_call(
        _combine_kernel,
        grid=(nt,),
        in_specs=[pl.BlockSpec((1, 1, TOP_K * tc), lambda i: (i, 0, 0), memory_space=pltpu.SMEM),
                  pl.BlockSpec((1, 1, TOP_K * tc), lambda i: (jnp.minimum(i + 1, nt - 1), 0, 0),
                               memory_space=pltpu.SMEM),
                  pl.BlockSpec((tc, d), lambda i: (i, 0)), pl.BlockSpec((tc, 128), lambda i: (i, 0)),
                  _resident((1, d)), pl.BlockSpec(memory_space=pl.ANY)],
        out_specs=pl.BlockSpec((tc, d), lambda i: (i, 0)),
        out_shape=jax.ShapeDtypeStruct((n, d), F32),
        scratch_shapes=[pltpu.VMEM((2, TOP_K * tc // GATHER_UNROLL, GATHER_UNROLL, d // 2), jnp.uint32),
                        pltpu.SemaphoreType.DMA((2,))],
        compiler_params=_cparams(("arbitrary",)),
        name="moe_combine",
    )(pos, pos, x1, gate, normf_g, ys)


def _route(top_idx, n_exp, tc):
    n = top_idx.shape[0]
    n_assign = n * TOP_K
    n_blocks = n_assign // ROW_BLOCK + n_exp
    tile = min(1024, n)
    hot = (top_idx[:, :, None] == jnp.arange(n_exp, dtype=jnp.int32)).any(axis=1).reshape(n // tile, tile, n_exp)
    earlier = jnp.asarray(np.tril(np.ones((tile, tile), np.float32), -1), BF16)
    within = jnp.einsum("ts,bse->bte", earlier, hot.astype(BF16), preferred_element_type=F32)
    tile_tot = jnp.sum(hot, axis=1, dtype=jnp.int32)
    tile_off = jnp.cumsum(tile_tot, axis=0) - tile_tot
    rank = (within.astype(jnp.int32) + tile_off[:, None, :]).reshape(n, n_exp)
    counts = jnp.sum(tile_tot, axis=0)
    padded = (counts + ROW_BLOCK - 1) // ROW_BLOCK * ROW_BLOCK
    pad_end = jnp.cumsum(padded)
    pad_start = pad_end - padded
    grp_start = jnp.cumsum(counts) - counts
    slot = jnp.take_along_axis(rank + pad_start[None, :], top_idx, axis=1).reshape(-1)
    block_start = jnp.arange(n_blocks, dtype=jnp.int32) * ROW_BLOCK
    block_expert = jnp.minimum(jnp.searchsorted(pad_end, block_start, side="right"), n_exp - 1).astype(jnp.int32)
    n_valid = (pad_end[-1] // ROW_BLOCK).astype(jnp.int32).reshape(1)
    order = jnp.argsort(top_idx.reshape(-1), stable=True)
    row_e = jnp.repeat(block_expert, ROW_BLOCK)
    row_r = jnp.arange(n_blocks * ROW_BLOCK, dtype=jnp.int32) - pad_start[row_e]
    src = jnp.clip(grp_start[row_e] + row_r, 0, n_assign - 1)
    row_tok = jnp.where((row_r >= 0) & (row_r < counts[row_e]), order[src] // TOP_K, 0).astype(jnp.int32)
    pos = slot.reshape(n // tc, tc, TOP_K).transpose(0, 2, 1).reshape(n // tc, 1, TOP_K * tc)
    block_rows = jnp.clip(counts[block_expert] - (block_start - pad_start[block_expert]), 0, ROW_BLOCK)
    block_rows = jnp.where(jnp.arange(n_blocks) < n_valid[0], block_rows, 0).astype(jnp.int32)
    return row_tok, block_expert, block_rows, n_valid, pos, n_blocks


def _layer(x, norm1_g, w_in, mix_mu, w0, w_decay_up, a0, w_iclr_up, w_gate_up, k_k, k_a, r_k, lnx_g, lnx_b, b_qkv,
           sinks, w_up_rwkv, w_up_attn, w_out, norm2_g, w_router, b_router, w1, b1, w2, b2, normf_g):
    b, s, d = x.shape
    n = b * s
    c = w_up_rwkv.shape[0]
    lora = w_decay_up.shape[0]
    qc = w_up_attn.shape[0]
    kvc = KV_HEADS * HEAD
    rwkv_cols = 3 * c + 2 * lora + w_gate_up.shape[0]
    qkv_cols = qc + 2 * kvc
    row = lambda t: t.reshape(1, -1).astype(F32)
    xf = x.reshape(n, d)
    g1 = row(norm1_g)
    w_in_b = w_in.astype(BF16)

    z_rwkv = _norm_proj(xf, g1, w_in_b[:, :rwkv_cols], jnp.zeros((1, rwkv_cols), F32), F32, 512)
    qkv = _norm_proj(xf, g1, w_in_b[:, rwkv_cols:rwkv_cols + qkv_cols], row(b_qkv), BF16, 512)
    gates = _norm_proj(xf, g1, w_in_b[:, rwkv_cols + qkv_cols:], jnp.zeros((1, 2 * d), F32), BF16, 512)

    zl = jnp.zeros((lora, c), F32)
    w_lora = jnp.concatenate([jnp.concatenate([w_decay_up, zl], axis=1), jnp.concatenate([zl, w_iclr_up], axis=1)],
                             axis=0)
    r, lw, kf, v, kn, ba, g = _rwkv_prep(z_rwkv, s, c, row(mix_mu), w_lora, w_gate_up, row(w0), row(a0), row(k_k),
                                         row(k_a), 256)
    as3 = lambda t: t.reshape(b, s, c)
    y_rwkv = _rwkv_scan(as3(r), as3(lw), as3(kf), as3(v), as3(kn), as3(ba), as3(g), row(r_k), row(lnx_g),
                        row(lnx_b)).reshape(n, c)

    q = qkv[:, :qc].reshape(b, s, qc)
    ka = qkv[:, qc:qc + kvc].reshape(b, s, kvc)
    va = qkv[:, qc + kvc:].reshape(b, s, kvc)
    sinks_b = jnp.broadcast_to(sinks.astype(F32).reshape(-1, 1), (sinks.shape[0], 128))
    y_attn = _attention(q, ka, va, sinks_b).reshape(n, qc)

    x1, h2, gate, top_idx = _merge(xf, y_rwkv, y_attn, gates, w_up_rwkv.astype(BF16), w_up_attn.astype(BF16),
                                   w_out.astype(BF16), row(norm2_g), w_router, row(b_router), 256)

    tc = 256
    row_tok, block_expert, block_rows, n_valid, pos, n_blocks = _route(top_idx[:, :TOP_K], w_router.shape[1], tc)
    xs = _dispatch(h2, row_tok, block_rows, n_blocks)
    ys = _expert_ffn(xs, w1, b1, w2, b2, block_expert, block_rows, n_valid, n_blocks)
    out = _combine(x1, gate, pos, ys, row(normf_g), tc)
    return out.reshape(b, s, d)


def kernel(x, norm1_g, w_in, mix_mu, w0, w_decay_up, a0, w_iclr_up, w_gate_up, k_k, k_a, r_k, lnx_g, lnx_b, b_qkv,
           sinks, w_up_rwkv, w_up_attn, w_out, norm2_g, w_router, b_router, w1, b1, w2, b2, normf_g):
    assert w_in.shape[0] == 1, "single-layer block"
    return _layer(x, norm1_g[0], w_in[0], mix_mu[0], w0[0], w_decay_up[0], a0[0], w_iclr_up[0], w_gate_up[0],
                  k_k[0], k_a[0], r_k[0], lnx_g[0], lnx_b[0], b_qkv[0], sinks[0], w_up_rwkv[0], w_up_attn[0],
                  w_out[0], norm2_g[0], w_router[0], b_router[0], w1[0], b1[0], w2[0], b2[0], normf_g)
```

```python
import functools

import jax
import jax.numpy as jnp
import numpy as np
from jax import lax
from jax.experimental import pallas as pl
from jax.experimental.pallas import tpu as pltpu

F32 = jnp.float32
BF16 = jnp.bfloat16

NORM_EPS = 1e-5
LNX_EPS = 64e-5
HEAD = 64
WINDOW = 128
KV_HEADS = 4
TOP_K = 4
SWIGLU_LIMIT = 7.0
SWIGLU_ALPHA = 1.702
DECAY_SCALE = float(np.exp(-0.5))

CHUNK = 64
ROW_BLOCK = 1152
ROW_SUB = 384
FF_TILE = 512
OUT_TILE = 256
VMEM_LIMIT = 56 * 1024 * 1024
HI = lax.Precision.HIGHEST


def _cparams(sem):
    return pltpu.CompilerParams(dimension_semantics=sem, vmem_limit_bytes=VMEM_LIMIT)


def _resident(shape):
    nd = len(shape)
    return pl.BlockSpec(shape, lambda *_: (0,) * nd, pipeline_mode=pl.Buffered(1))


def _rmsnorm(x, g):
    return x * lax.rsqrt(jnp.mean(x * x, axis=-1, keepdims=True) + NORM_EPS) * g


def _sigmoid(x):
    return 1.0 / (1.0 + jnp.exp(-x))


def _pack_bf16_pair(lo, hi):
    lo_bits = lax.bitcast_convert_type(lo.astype(BF16).astype(F32), jnp.uint32)
    hi_bits = lax.bitcast_convert_type(hi.astype(BF16).astype(F32), jnp.uint32)
    return (hi_bits & jnp.uint32(0xFFFF0000)) | (lo_bits >> 16)


def _unpack_bf16_pair(packed):
    lo = lax.bitcast_convert_type(packed << 16, F32)
    hi = lax.bitcast_convert_type(packed & jnp.uint32(0xFFFF0000), F32)
    return lo, hi


def _norm_proj_kernel(x_ref, g_ref, w_ref, b_ref, o_ref):
    h = _rmsnorm(x_ref[...], g_ref[...]).astype(BF16)
    z = jnp.dot(h, w_ref[...], preferred_element_type=F32) + b_ref[...]
    o_ref[...] = z.astype(o_ref.dtype)


def _norm_proj(x, g, w, b, out_dtype, tm):
    n, d = x.shape
    cols = w.shape[1]
    return pl.pallas_call(
        _norm_proj_kernel,
        grid=(n // tm,),
        in_specs=[pl.BlockSpec((tm, d), lambda i: (i, 0)), _resident((1, d)), _resident((d, cols)),
                  _resident((1, cols))],
        out_specs=pl.BlockSpec((tm, cols), lambda i: (i, 0)),
        out_shape=jax.ShapeDtypeStruct((n, cols), out_dtype),
        compiler_params=_cparams(("parallel",)),
        name="norm_proj",
    )(x, g, w, b)


def _head_sums(x):
    rows, c = x.shape
    lane = lax.broadcasted_iota(jnp.int32, (rows, 128), 1)
    low = lane < HEAD
    parts = []
    for gi in range(c // 128):
        xg = x[:, gi * 128:(gi + 1) * 128]
        s_lo = jnp.sum(jnp.where(low, xg, 0.0), axis=-1, keepdims=True)
        s_hi = jnp.sum(jnp.where(low, 0.0, xg), axis=-1, keepdims=True)
        parts.append(jnp.where(low, s_lo, s_hi))
    return jnp.concatenate(parts, axis=-1)


def _prep_kernel(seq_blocks, c, z_ref, zp_ref, mu_ref, wlora_ref, wgate_ref, w0_ref, a0_ref, kk_ref, ka_ref,
                 r_ref, lw_ref, kf_ref, v_ref, kn_ref, ba_ref, g_ref):
    i = pl.program_id(0)
    z = z_ref[...]
    tm = z.shape[0]
    prev = jnp.where(i % seq_blocks == 0, 0.0, zp_ref[7:8, :])
    row = lax.broadcasted_iota(jnp.int32, z.shape, 0)
    shifted = jnp.where(row == 0, prev, pltpu.roll(z, 1, 0))
    zs = z + (shifted - z) * mu_ref[...]
    r = zs[:, 0:c]
    k = zs[:, c:2 * c]
    v = zs[:, 2 * c:3 * c]
    zwa = zs[:, 3 * c:3 * c + 128]
    zg = zs[:, 3 * c + 128:3 * c + 256]
    lane = lax.broadcasted_iota(jnp.int32, (tm, 128), 1)
    lora_in = jnp.where(lane < 64, jnp.tanh(zwa), zwa)
    up = jnp.dot(lora_in, wlora_ref[...], preferred_element_type=F32, precision=HI)
    u = w0_ref[...] + up[:, 0:c]
    a = _sigmoid(a0_ref[...] + up[:, c:2 * c])
    g = jnp.dot(_sigmoid(zg), wgate_ref[...], preferred_element_type=F32, precision=HI)
    lw = -DECAY_SCALE * _sigmoid(u)
    kk = k * kk_ref[...]
    kn = kk / jnp.maximum(jnp.sqrt(_head_sums(kk * kk)), 1e-12)
    kf = k * (1.0 + (a - 1.0) * ka_ref[...])
    r_ref[...] = r
    lw_ref[...] = lw
    kf_ref[...] = kf
    v_ref[...] = v
    kn_ref[...] = kn
    ba_ref[...] = kn * a
    g_ref[...] = g


def _rwkv_prep(z, seq, c, mix_mu, w_lora, w_gate, w0, a0, k_k, k_a, tm):
    n, zc = z.shape
    row_spec = pl.BlockSpec((tm, c), lambda i: (i, 0))
    out = jax.ShapeDtypeStruct((n, c), F32)
    return pl.pallas_call(
        functools.partial(_prep_kernel, seq // tm, c),
        grid=(n // tm,),
        in_specs=[pl.BlockSpec((tm, zc), lambda i: (i, 0)),
                  pl.BlockSpec((8, zc), lambda i: (jnp.maximum(i * (tm // 8) - 1, 0), 0)),
                  _resident((1, zc)), _resident((128, 2 * c)), _resident((128, c)),
                  _resident((1, c)), _resident((1, c)), _resident((1, c)), _resident((1, c))],
        out_specs=[row_spec] * 7,
        out_shape=[out] * 7,
        compiler_params=_cparams(("parallel",)),
        name="rwkv_prep",
    )(z, z, mix_mu, w_lora, w_gate, w0, a0, k_k, k_a)


def _split3(x):
    h1 = x.astype(BF16)
    r1 = x - h1.astype(F32)
    h2 = r1.astype(BF16)
    h3 = (r1 - h2.astype(F32)).astype(BF16)
    return h1, h2, h3


def _mm(a, b):
    return jnp.dot(a, b, preferred_element_type=F32)


def _mm_nt(a, b):
    return lax.dot_general(a, b, (((1,), (1,)), ((), ())), preferred_element_type=F32)


def _mm_tn(a, b):
    return lax.dot_general(a, b, (((0,), (0,)), ((), ())), preferred_element_type=F32)


GROUP_HEADS = 4
GROUP_LANES = GROUP_HEADS * HEAD


def _block_diag(x, mask):
    return jnp.concatenate([x.astype(BF16)] * GROUP_HEADS, axis=0) * mask


def _scan_kernel(r_ref, lw_ref, kf_ref, v_ref, kn_ref, ba_ref, g_ref, rk_ref, lng_ref, lnb_ref, mask_ref, y_ref,
                 s_ref):
    @pl.when(pl.program_id(0) == 0)
    def _():
        s_ref[...] = jnp.zeros_like(s_ref)

    t = CHUNK
    nb = r_ref.shape[0]
    c = r_ref.shape[2]
    gl = GROUP_LANES
    n_groups = c // gl
    mask = mask_ref[...]
    ti = lax.broadcasted_iota(jnp.int32, (t, t), 0)
    si = lax.broadcasted_iota(jnp.int32, (t, t), 1)
    tri = jnp.where(si <= ti, 1.0, 0.0).astype(BF16)
    row = lax.broadcasted_iota(jnp.int32, (t, gl), 0)
    col = lax.broadcasted_iota(jnp.int32, (t, gl), 1) & (HEAD - 1)
    strict = col < row
    incl = col <= row
    eye = jnp.where(col == row, 1.0, 0.0)

    prep = []
    for b in range(nb):
        lw = lw_ref[b]
        cum = sum(jnp.dot(tri, part, preferred_element_type=F32) for part in _split3(lw))
        cum_end = cum[t - 1:t, :]
        e_pos = jnp.exp(cum)
        e_neg = jnp.exp(-cum)
        e_end = jnp.exp(cum_end - cum)
        r = r_ref[b]
        kf = kf_ref[b]
        ba = ba_ref[b]
        prep.append(dict(
            rt=r * e_pos,
            at=-kn_ref[b] * jnp.exp(cum - lw),
            bt=ba * e_neg,
            kt=kf * e_neg,
            bh=ba * e_end,
            kh=kf * e_end,
            v=v_ref[b],
            w_end=jnp.exp(cum_end),
            rkf=r * kf * rk_ref[...]))

    probs = [(b, gi) for b in range(nb) for gi in range(n_groups)]

    def part(b, gi, name):
        return prep[b][name][:, gi * gl:(gi + 1) * gl]

    a_ab, a_ak, a_rb, a_rk = [], [], [], []
    for b, gi in probs:
        lhs = jnp.concatenate([part(b, gi, "at"), part(b, gi, "rt")], axis=0).astype(BF16)
        rhs = jnp.concatenate([_block_diag(part(b, gi, "bt"), mask), _block_diag(part(b, gi, "kt"), mask)], axis=0)
        amat = _mm_nt(lhs, rhs)
        a_ab.append(jnp.where(strict, amat[0:t, 0:gl], 0.0))
        a_ak.append(jnp.where(strict, amat[0:t, gl:2 * gl], 0.0))
        a_rb.append(jnp.where(incl, amat[t:2 * t, 0:gl], 0.0))
        a_rk.append(jnp.where(incl, amat[t:2 * t, gl:2 * gl], 0.0))

    vbd = [_block_diag(part(b, gi, "v"), mask) for b, gi in probs]
    av = [_mm(a_ak[i].astype(BF16), vbd[i]) for i in range(len(probs))]
    minv = [eye + a for a in a_ab]
    power = [_mm(a.astype(BF16), _block_diag(a, mask)) for a in a_ab]
    span = 2
    while span < t:
        last = span * 2 >= t
        for i in range(len(probs)):
            pbd = _block_diag(power[i], mask)
            if last:
                minv[i] = minv[i] + _mm(minv[i].astype(BF16), pbd)
            else:
                both = _mm(jnp.concatenate([power[i], minv[i]], axis=0).astype(BF16), pbd)
                power[i] = both[0:t]
                minv[i] = minv[i] + both[t:2 * t]
        span *= 2

    s0 = [s_ref[b, gi] for b, gi in probs]
    sbd = [_block_diag(s, mask) for s in s0]
    ps = []
    for i, (b, gi) in enumerate(probs):
        lhs = jnp.concatenate([part(b, gi, "at"), part(b, gi, "rt")], axis=0).astype(BF16)
        ps.append(_mm_nt(lhs, sbd[i]))
    u = [_mm(minv[i].astype(BF16), _block_diag(ps[i][0:t] + av[i], mask)) for i in range(len(probs))]
    ys = []
    for i, (b, gi) in enumerate(probs):
        lhs = jnp.concatenate([a_rb[i], a_rk[i]], axis=1).astype(BF16)
        rhs = jnp.concatenate([_block_diag(u[i], mask), vbd[i]], axis=0)
        ys.append(ps[i][t:2 * t] + _mm(lhs, rhs))
        uv = jnp.concatenate([u[i], part(b, gi, "v")], axis=0).astype(BF16)
        bk = jnp.concatenate([part(b, gi, "bh"), part(b, gi, "kh")], axis=0).astype(BF16)
        full = _mm_tn(uv, bk) * mask.astype(F32)
        upd = full[0:HEAD]
        for hh in range(1, GROUP_HEADS):
            upd = upd + full[hh * HEAD:(hh + 1) * HEAD]
        s_ref[b, gi] = s0[i] * part(b, gi, "w_end") + upd

    for b in range(nb):
        y = jnp.concatenate([ys[b * n_groups + gi] for gi in range(n_groups)], axis=1)
        mu = _head_sums(y) * (1.0 / HEAD)
        yc = y - mu
        var = _head_sums(yc * yc) * (1.0 / HEAD)
        yn = yc * lax.rsqrt(var + LNX_EPS)
        bonus = _head_sums(prep[b]["rkf"]) * prep[b]["v"]
        y_ref[b] = ((yn * lng_ref[...] + lnb_ref[...] + bonus) * g_ref[b]).astype(y_ref.dtype)


def _rwkv_scan(r, lw, kf, v, kn, ba, g, r_k, lnx_g, lnx_b):
    b, s, c = r.shape
    blk = pl.BlockSpec((b, CHUNK, c), lambda ci: (0, ci, 0))
    hid = np.arange(GROUP_LANES) // HEAD
    mask = jnp.asarray(hid[:, None] == hid[None, :], BF16)
    return pl.pallas_call(
        _scan_kernel,
        grid=(s // CHUNK,),
        in_specs=[blk] * 7 + [_resident((1, c))] * 3 + [_resident((GROUP_LANES, GROUP_LANES))],
        out_specs=blk,
        out_shape=jax.ShapeDtypeStruct((b, s, c), BF16),
        scratch_shapes=[pltpu.VMEM((b, c // GROUP_LANES, HEAD, GROUP_LANES), F32)],
        compiler_params=_cparams(("arbitrary",)),
        name="rwkv_scan",
    )(r, lw, kf, v, kn, ba, g, r_k, lnx_g, lnx_b, mask)


def _attn_kernel(q_ref, kc_ref, kp_ref, vc_ref, vp_ref, sink_ref, o_ref):
    first = pl.program_id(1) == 0
    w = WINDOW
    group = q_ref.shape[-1] // HEAD // KV_HEADS
    row = lax.broadcasted_iota(jnp.int32, (group * w, 2 * w), 0)
    qi = row & (w - 1)
    kj = lax.broadcasted_iota(jnp.int32, (group * w, 2 * w), 1)
    lo = jnp.where(first, jnp.maximum(qi, w - 1), qi)
    valid = (kj > lo) & (kj <= qi + w)
    grow = lax.broadcasted_iota(jnp.int32, (group * w, 1), 0) // w
    q = q_ref[...]
    outs = []
    for hk in range(KV_HEADS):
        ksl = slice(hk * HEAD, (hk + 1) * HEAD)
        kcat = jnp.concatenate([kp_ref[:, ksl], kc_ref[:, ksl]], axis=0)
        vcat = jnp.concatenate([vp_ref[:, ksl], vc_ref[:, ksl]], axis=0)
        qg = jnp.concatenate([q[:, (hk * group + gi) * HEAD:(hk * group + gi + 1) * HEAD] for gi in range(group)],
                             axis=0)
        sink = jnp.zeros((group * w, 1), F32)
        for gi in range(group):
            sink = jnp.where(grow == gi, sink_ref[hk * group + gi:hk * group + gi + 1, 0:1], sink)
        s = _mm_nt(qg, kcat) * (HEAD ** -0.5)
        s = jnp.where(valid, s, -1e30)
        m = jnp.maximum(jnp.max(s, axis=-1, keepdims=True), sink)
        p = jnp.exp(s - m)
        pb = p.astype(BF16)
        psum = _mm(pb, jnp.ones((2 * w, HEAD), BF16))
        o = _mm(pb, vcat) / (psum + jnp.exp(sink - m))
        for gi in range(group):
            outs.append(o[gi * w:(gi + 1) * w, :])
    o_ref[...] = jnp.concatenate(outs, axis=-1).astype(o_ref.dtype)


def _attention(q, k, v, sinks_b):
    b, s, qc = q.shape
    kc = k.shape[-1]
    nb = s // WINDOW
    cur = lambda bi, i: (bi, i, 0)
    prev = lambda bi, i: (bi, jnp.maximum(i - 1, 0), 0)
    return pl.pallas_call(
        _attn_kernel,
        grid=(b, nb),
        in_specs=[pl.BlockSpec((None, WINDOW, qc), cur),
                  pl.BlockSpec((None, WINDOW, kc), cur), pl.BlockSpec((None, WINDOW, kc), prev),
                  pl.BlockSpec((None, WINDOW, kc), cur), pl.BlockSpec((None, WINDOW, kc), prev),
                  _resident(sinks_b.shape)],
        out_specs=pl.BlockSpec((None, WINDOW, qc), cur),
        out_shape=jax.ShapeDtypeStruct((b, s, qc), BF16),
        compiler_params=_cparams(("parallel", "parallel")),
        name="swa_attention",
    )(q, k, k, v, v, sinks_b)


MERGE_SPLIT = 2


def _merge_kernel(n_exp, x_ref, yr_ref, ya_ref, gr_ref, ga_ref, wur_ref, wua_ref, wo_ref, n2_ref, wr_ref, br_ref,
                  x1_ref, h2_ref, gate_ref, idx_ref):
    tm = x_ref.shape[0] // MERGE_SPLIT
    parts = [slice(p * tm, (p + 1) * tm) for p in range(MERGE_SPLIT)]
    ups = [(_mm(yr_ref[rs, :], wur_ref[...]), _mm(ya_ref[rs, :], wua_ref[...])) for rs in parts]
    merged = [_sigmoid(gr_ref[rs, :].astype(F32)) * ur + _sigmoid(ga_ref[rs, :].astype(F32)) * ua
              for rs, (ur, ua) in zip(parts, ups)]
    x1s = [x_ref[rs, :] + _mm(m.astype(BF16), wo_ref[...]) for rs, m in zip(parts, merged)]
    h2s = []
    for rs, x1 in zip(parts, x1s):
        x1_ref[rs, :] = x1
        h2 = _rmsnorm(x1, n2_ref[...])
        half = h2.shape[1] // 2
        h2_ref[rs, :] = _pack_bf16_pair(h2[:, :half], h2[:, half:])
        h2s.append(h2)
    all_logits = [jnp.dot(h2, wr_ref[...], preferred_element_type=F32, precision=HI) + br_ref[...] for h2 in h2s]
    lane_e = lax.broadcasted_iota(jnp.int32, (tm, n_exp), 1).astype(F32)
    lane_o = lax.broadcasted_iota(jnp.int32, (tm, 128), 1)
    for rs, logits in zip(parts, all_logits):
        vals, idxs = [], []
        for _ in range(TOP_K):
            mx = jnp.max(logits, axis=-1, keepdims=True)
            ix = jnp.min(jnp.where(logits == mx, lane_e, float(n_exp)), axis=-1, keepdims=True)
            vals.append(mx)
            idxs.append(ix)
            logits = jnp.where(lane_e == ix, -jnp.inf, logits)
        exps = [jnp.exp(vk - vals[0]) for vk in vals]
        tot = exps[0] + exps[1] + exps[2] + exps[3]
        gate_o, idx_o = jnp.zeros((tm, 128), F32), jnp.zeros((tm, 128), F32)
        for kk in range(TOP_K):
            gate_o = jnp.where(lane_o == kk, exps[kk] / tot, gate_o)
            idx_o = jnp.where(lane_o == kk, idxs[kk], idx_o)
        gate_ref[rs, :] = gate_o
        idx_ref[rs, :] = idx_o.astype(jnp.int32)


def _merge(x, y_rwkv, y_attn, gates, w_up_r, w_up_a, w_out, norm2_g, w_router, b_router, tm):
    n, d = x.shape
    c = y_rwkv.shape[1]
    n_exp = w_router.shape[1]
    row = lambda cols, j=0: pl.BlockSpec((tm, cols), lambda i: (i, j))
    return pl.pallas_call(
        functools.partial(_merge_kernel, n_exp),
        grid=(n // tm,),
        in_specs=[row(d), row(c), row(c), row(d, 0), row(d, 1),
                  _resident((c, d)), _resident((c, d)), _resident((d, d)), _resident((1, d)),
                  _resident((d, n_exp)), _resident((1, n_exp))],
        out_specs=[row(d), row(d // 2), row(128), row(128)],
        out_shape=[jax.ShapeDtypeStruct((n, d), F32), jax.ShapeDtypeStruct((n, d // 2), jnp.uint32),
                   jax.ShapeDtypeStruct((n, 128), F32), jax.ShapeDtypeStruct((n, 128), jnp.int32)],
        compiler_params=_cparams(("parallel",)),
        name="merge_router",
    )(x, y_rwkv, y_attn, gates, gates, w_up_r, w_up_a, w_out, norm2_g, w_router, b_router)


GATHER_UNROLL = 8


def _start_row_gather(src_hbm, idx_ref, buf, sem):
    groups, unroll, _ = buf.shape
    assert unroll == GATHER_UNROLL

    def start(g, carry):
        for u in range(GATHER_UNROLL):
            src_row = idx_ref[0, 0, g * GATHER_UNROLL + u]
            pltpu.make_async_copy(src_hbm.at[pl.ds(src_row, 1)], buf.at[g, pl.ds(u, 1)], sem).start(priority=u % 2)
        return carry

    lax.fori_loop(0, groups, start, 0)


def _wait_row_gather(buf, sem):
    pltpu.make_async_copy(buf, buf, sem).wait()


def _dispatch_kernel(brows_ref, tok_ref, tok_next_ref, h_hbm, o_ref, buf, sems):
    s = pl.program_id(0)
    n_sub = ROW_BLOCK // ROW_SUB

    def has_rows(t):
        return lax.rem(t, n_sub) * ROW_SUB < brows_ref[lax.div(t, n_sub)]

    slot = s % 2

    @pl.when((s == 0) & has_rows(0))
    def _():
        _start_row_gather(h_hbm, tok_ref, buf.at[0], sems.at[0])

    @pl.when((s + 1 < pl.num_programs(0)) & has_rows(jnp.minimum(s + 1, pl.num_programs(0) - 1)))
    def _():
        _start_row_gather(h_hbm, tok_next_ref, buf.at[1 - slot], sems.at[1 - slot])

    @pl.when(has_rows(s))
    def _():
        _wait_row_gather(buf.at[slot], sems.at[slot])
        half = o_ref.shape[1] // 2
        lo, hi = _unpack_bf16_pair(buf[slot].reshape(ROW_SUB, half))
        o_ref[:, :half] = lo.astype(o_ref.dtype)
        o_ref[:, half:] = hi.astype(o_ref.dtype)

    @pl.when(jnp.logical_not(has_rows(s)))
    def _():
        o_ref[...] = jnp.zeros_like(o_ref)


def _dispatch(h2p, row_tok, block_rows, n_blocks):
    half = h2p.shape[1]
    d = 2 * half
    steps = n_blocks * (ROW_BLOCK // ROW_SUB)
    grid_spec = pltpu.PrefetchScalarGridSpec(
        num_scalar_prefetch=1,
        grid=(steps,),
        in_specs=[pl.BlockSpec((1, 1, ROW_SUB), lambda s, br: (s, 0, 0), memory_space=pltpu.SMEM),
                  pl.BlockSpec((1, 1, ROW_SUB), lambda s, br: (jnp.minimum(s + 1, steps - 1), 0, 0),
                               memory_space=pltpu.SMEM),
                  pl.BlockSpec(memory_space=pl.ANY)],
        out_specs=pl.BlockSpec((ROW_SUB, d), lambda s, br: (s, 0)),
        scratch_shapes=[pltpu.VMEM((2, ROW_SUB // GATHER_UNROLL, GATHER_UNROLL, half), jnp.uint32),
                        pltpu.SemaphoreType.DMA((2,))],
    )
    tok = row_tok.reshape(steps, 1, ROW_SUB)
    return pl.pallas_call(
        _dispatch_kernel,
        grid_spec=grid_spec,
        out_shape=jax.ShapeDtypeStruct((n_blocks * ROW_BLOCK, d), BF16),
        compiler_params=_cparams(("arbitrary",)),
        name="moe_dispatch",
    )(block_rows, tok, tok, h2p)


def _ffn_kernel(nvalid_ref, bexp_ref, brows_ref, xs_ref, w1g_ref, w1l_ref, b1g_ref, b1l_ref, w2a_ref, w2b_ref,
                b2a_ref, b2b_ref, o_ref, act_ref):
    i = pl.program_id(0)
    j = pl.program_id(1)
    n_ff = act_ref.shape[0]
    rows_valid = brows_ref[i]
    n_sub = ROW_BLOCK // ROW_SUB

    @pl.when(j < n_ff)
    def _():
        w1g = w1g_ref[...].astype(BF16)
        w1l = w1l_ref[...].astype(BF16)
        for sb in range(n_sub):
            rs = slice(sb * ROW_SUB, (sb + 1) * ROW_SUB)

            @pl.when(sb * ROW_SUB < rows_valid)
            def _():
                x = xs_ref[rs, :]
                hg = _mm(x, w1g) + b1g_ref[...]
                hl = _mm(x, w1l) + b1l_ref[...]
                glu = jnp.minimum(hg, SWIGLU_LIMIT)
                lin = jnp.clip(hl, -SWIGLU_LIMIT, SWIGLU_LIMIT)
                act_ref[j, rs, :] = (glu * _sigmoid(SWIGLU_ALPHA * glu) * (lin + 1.0)).astype(BF16)

    @pl.when(j >= n_ff)
    def _():
        w2a = w2a_ref[...].astype(BF16)
        w2b = w2b_ref[...].astype(BF16)
        for sb in range(n_sub):
            rs = slice(sb * ROW_SUB, (sb + 1) * ROW_SUB)

            @pl.when(sb * ROW_SUB < rows_valid)
            def _():
                def project(w2, b2_ref):
                    acc = _mm(act_ref[0, rs, :], w2[0:FF_TILE])
                    for jf in range(1, n_ff):
                        acc = acc + _mm(act_ref[jf, rs, :], w2[jf * FF_TILE:(jf + 1) * FF_TILE])
                    return acc + b2_ref[...]

                o_ref[rs, :] = _pack_bf16_pair(project(w2a, b2a_ref), project(w2b, b2b_ref))

            @pl.when(sb * ROW_SUB >= rows_valid)
            def _():
                o_ref[rs, :] = jnp.zeros((ROW_SUB, o_ref.shape[1]), o_ref.dtype)


def _expert_ffn(xs, w1, b1, w2, b2, block_expert, block_rows, n_valid, n_blocks):
    d = xs.shape[1]
    n_exp, _, ff2 = w1.shape
    ff = ff2 // 2
    n_ff = ff // FF_TILE
    n_out = d // 2 // OUT_TILE

    def hid(i, j, nv):
        return jnp.where(i < nv[0], jnp.minimum(j, n_ff - 1), n_ff - 1)

    def w2_expert(i, j, be):
        return jnp.where(j < n_ff, be[jnp.maximum(i - 1, 0)], be[i])

    def w2_tile(i, j, nv):
        return jnp.where((i < nv[0]) & (j >= n_ff), j - n_ff, n_out - 1)

    grid_spec = pltpu.PrefetchScalarGridSpec(
        num_scalar_prefetch=3,
        grid=(n_blocks, n_ff + n_out),
        in_specs=[pl.BlockSpec((ROW_BLOCK, d), lambda i, j, nv, be, br: (i, 0)),
                  pl.BlockSpec((None, d, FF_TILE), lambda i, j, nv, be, br: (be[i], 0, hid(i, j, nv))),
                  pl.BlockSpec((None, d, FF_TILE), lambda i, j, nv, be, br: (be[i], 0, n_ff + hid(i, j, nv))),
                  pl.BlockSpec((None, 1, FF_TILE), lambda i, j, nv, be, br: (be[i], 0, hid(i, j, nv))),
                  pl.BlockSpec((None, 1, FF_TILE), lambda i, j, nv, be, br: (be[i], 0, n_ff + hid(i, j, nv))),
                  pl.BlockSpec((None, ff, OUT_TILE),
                               lambda i, j, nv, be, br: (w2_expert(i, j, be), 0, w2_tile(i, j, nv))),
                  pl.BlockSpec((None, ff, OUT_TILE),
                               lambda i, j, nv, be, br: (w2_expert(i, j, be), 0, n_out + w2_tile(i, j, nv))),
                  pl.BlockSpec((None, 1, OUT_TILE),
                               lambda i, j, nv, be, br: (w2_expert(i, j, be), 0, w2_tile(i, j, nv))),
                  pl.BlockSpec((None, 1, OUT_TILE),
                               lambda i, j, nv, be, br: (w2_expert(i, j, be), 0, n_out + w2_tile(i, j, nv)))],
        out_specs=pl.BlockSpec((ROW_BLOCK, OUT_TILE), lambda i, j, nv, be, br: (i, jnp.maximum(j - n_ff, 0))),
        scratch_shapes=[pltpu.VMEM((n_ff, ROW_BLOCK, FF_TILE), BF16)],
    )
    b1r = b1.reshape(n_exp, 1, ff2)
    b2r = b2.reshape(n_exp, 1, d)
    return pl.pallas_call(
        _ffn_kernel,
        grid_spec=grid_spec,
        out_shape=jax.ShapeDtypeStruct((n_blocks * ROW_BLOCK, d // 2), jnp.uint32),
        compiler_params=_cparams(("arbitrary", "arbitrary")),
        name="moe_ffn",
    )(n_valid, block_expert, block_rows, xs, w1, w1, b1r, b1r, w2, w2, b2r, b2r)


def _combine_kernel(pos_ref, pos_next_ref, x1_ref, gate_ref, nf_ref, ys_hbm, o_ref, buf, sems):
    s = pl.program_id(0)
    tc = x1_ref.shape[0]
    slot = s % 2

    @pl.when(s == 0)
    def _():
        _start_row_gather(ys_hbm, pos_ref, buf.at[0], sems.at[0])

    @pl.when(s + 1 < pl.num_programs(0))
    def _():
        _start_row_gather(ys_hbm, pos_next_ref, buf.at[1 - slot], sems.at[1 - slot])

    _wait_row_gather(buf.at[slot], sems.at[slot])
    half = x1_ref.shape[1] // 2
    groups = tc // GATHER_UNROLL
    gate = gate_ref[...]
    acc_lo = x1_ref[:, :half]
    acc_hi = x1_ref[:, half:]
    for k in range(TOP_K):
        lo, hi = _unpack_bf16_pair(buf[slot, k * groups:(k + 1) * groups].reshape(tc, half))
        acc_lo = acc_lo + gate[:, k:k + 1] * lo
        acc_hi = acc_hi + gate[:, k:k + 1] * hi
    o_ref[...] = _rmsnorm(jnp.concatenate([acc_lo, acc_hi], axis=1), nf_ref[...])


def _combine(x1, gate, pos, ys, normf_g, tc):
    n, d = x1.shape
    nt = n // tc
    return pl.pallas_call(
        _combine_kernel,
        grid=(nt,),
        in_specs=[pl.BlockSpec((1, 1, TOP_K * tc), lambda i: (i, 0, 0), memory_space=pltpu.SMEM),
                  pl.BlockSpec((1, 1, TOP_K * tc), lambda i: (jnp.minimum(i + 1, nt - 1), 0, 0),
                               memory_space=pltpu.SMEM),
                  pl.BlockSpec((tc, d), lambda i: (i, 0)), pl.BlockSpec((tc, 128), lambda i: (i, 0)),
                  _resident((1, d)), pl.BlockSpec(memory_space=pl.ANY)],
        out_specs=pl.BlockSpec((tc, d), lambda i: (i, 0)),
        out_shape=jax.ShapeDtypeStruct((n, d), F32),
        scratch_shapes=[pltpu.VMEM((2, TOP_K * tc // GATHER_UNROLL, GATHER_UNROLL, d // 2), jnp.uint32),
                        pltpu.SemaphoreType.DMA((2,))],
        compiler_params=_cparams(("arbitrary",)),
        name="moe_combine",
    )(pos, pos, x1, gate, normf_g, ys)


def _route(top_idx, n_exp, tc):
    n = top_idx.shape[0]
    n_assign = n * TOP_K
    n_blocks = n_assign // ROW_BLOCK + n_exp
    tile = min(1024, n)
    hot = (top_idx[:, :, None] == jnp.arange(n_exp, dtype=jnp.int32)).any(axis=1).reshape(n // tile, tile, n_exp)
    earlier = jnp.asarray(np.tril(np.ones((tile, tile), np.float32), -1), BF16)
    within = jnp.einsum("ts,bse->bte", earlier, hot.astype(BF16), preferred_element_type=F32)
    tile_tot = jnp.sum(hot, axis=1, dtype=jnp.int32)
    tile_off = jnp.cumsum(tile_tot, axis=0) - tile_tot
    rank = (within.astype(jnp.int32) + tile_off[:, None, :]).reshape(n, n_exp)
    counts = jnp.sum(tile_tot, axis=0)
    padded = (counts + ROW_BLOCK - 1) // ROW_BLOCK * ROW_BLOCK
    pad_end = jnp.cumsum(padded)
    pad_start = pad_end - padded
    grp_start = jnp.cumsum(counts) - counts
    slot = jnp.take_along_axis(rank + pad_start[None, :], top_idx, axis=1).reshape(-1)
    block_start = jnp.arange(n_blocks, dtype=jnp.int32) * ROW_BLOCK
    block_expert = jnp.minimum(jnp.searchsorted(pad_end, block_start, side="right"), n_exp - 1).astype(jnp.int32)
    n_valid = (pad_end[-1] // ROW_BLOCK).astype(jnp.int32).reshape(1)
    order = jnp.argsort(top_idx.reshape(-1), stable=True)
    row_e = jnp.repeat(block_expert, ROW_BLOCK)
    row_r = jnp.arange(n_blocks * ROW_BLOCK, dtype=jnp.int32) - pad_start[row_e]
    src = jnp.clip(grp_start[row_e] + row_r, 0, n_assign - 1)
    row_tok = jnp.where((row_r >= 0) & (row_r < counts[row_e]), order[src] // TOP_K, 0).astype(jnp.int32)
    pos = slot.reshape(n // tc, tc, TOP_K).transpose(0, 2, 1).reshape(n // tc, 1, TOP_K * tc)
    block_rows = jnp.clip(counts[block_expert] - (block_start - pad_start[block_expert]), 0, ROW_BLOCK)
    block_rows = jnp.where(jnp.arange(n_blocks) < n_valid[0], block_rows, 0).astype(jnp.int32)
    return row_tok, block_expert, block_rows, n_valid, pos, n_blocks


def _layer(x, norm1_g, w_in, mix_mu, w0, w_decay_up, a0, w_iclr_up, w_gate_up, k_k, k_a, r_k, lnx_g, lnx_b, b_qkv,
           sinks, w_up_rwkv, w_up_attn, w_out, norm2_g, w_router, b_router, w1, b1, w2, b2, normf_g):
    b, s, d = x.shape
    n = b * s
    c = w_up_rwkv.shape[0]
    lora = w_decay_up.shape[0]
    qc = w_up_attn.shape[0]
    kvc = KV_HEADS * HEAD
    rwkv_cols = 3 * c + 2 * lora + w_gate_up.shape[0]
    qkv_cols = qc + 2 * kvc
    row = lambda t: t.reshape(1, -1).astype(F32)
    xf = x.reshape(n, d)
    g1 = row(norm1_g)
    w_in_b = w_in.astype(BF16)

    z_rwkv = _norm_proj(xf, g1, w_in_b[:, :rwkv_cols], jnp.zeros((1, rwkv_cols), F32), F32, 512)
    qkv = _norm_proj(xf, g1, w_in_b[:, rwkv_cols:rwkv_cols + qkv_cols], row(b_qkv), BF16, 512)
    gates = _norm_proj(xf, g1, w_in_b[:, rwkv_cols + qkv_cols:], jnp.zeros((1, 2 * d), F32), BF16, 512)

    zl = jnp.zeros((lora, c), F32)
    w_lora = jnp.concatenate([jnp.concatenate([w_decay_up, zl], axis=1), jnp.concatenate([zl, w_iclr_up], axis=1)],
                             axis=0)
    r, lw, kf, v, kn, ba, g = _rwkv_prep(z_rwkv, s, c, row(mix_mu), w_lora, w_gate_up, row(w0), row(a0), row(k_k),
                                         row(k_a), 256)
    as3 = lambda t: t.reshape(b, s, c)
    y_rwkv = _rwkv_scan(as3(r), as3(lw), as3(kf), as3(v), as3(kn), as3(ba), as3(g), row(r_k), row(lnx_g),
                        row(lnx_b)).reshape(n, c)

    q = qkv[:, :qc].reshape(b, s, qc)
    ka = qkv[:, qc:qc + kvc].reshape(b, s, kvc)
    va = qkv[:, qc + kvc:].reshape(b, s, kvc)
    sinks_b = jnp.broadcast_to(sinks.astype(F32).reshape(-1, 1), (sinks.shape[0], 128))
    y_attn = _attention(q, ka, va, sinks_b).reshape(n, qc)

    x1, h2, gate, top_idx = _merge(xf, y_rwkv, y_attn, gates, w_up_rwkv.astype(BF16), w_up_attn.astype(BF16),
                                   w_out.astype(BF16), row(norm2_g), w_router, row(b_router), 256)

    tc = 256
    row_tok, block_expert, block_rows, n_valid, pos, n_blocks = _route(top_idx[:, :TOP_K], w_router.shape[1], tc)
    xs = _dispatch(h2, row_tok, block_rows, n_blocks)
    ys = _expert_ffn(xs, w1, b1, w2, b2, block_expert, block_rows, n_valid, n_blocks)
    out = _combine(x1, gate, pos, ys, row(normf_g), tc)
    return out.reshape(b, s, d)


def kernel(x, norm1_g, w_in, mix_mu, w0, w_decay_up, a0, w_iclr_up, w_gate_up, k_k, k_a, r_k, lnx_g, lnx_b, b_qkv,
           sinks, w_up_rwkv, w_up_attn, w_out, norm2_g, w_router, b_router, w1, b1, w2, b2, normf_g):
    assert w_in.shape[0] == 1, "single-layer block"
    return _layer(x, norm1_g[0], w_in[0], mix_mu[0], w0[0], w_decay_up[0], a0[0], w_iclr_up[0], w_gate_up[0],
                  k_k[0], k_a[0], r_k[0], lnx_g[0], lnx_b[0], b_qkv[0], sinks[0], w_up_rwkv[0], w_up_attn[0],
                  w_out[0], norm2_g[0], w_router[0], b_router[0], w1[0], b1[0], w2[0], b2[0], normf_g)
```

```python
import functools

import jax
import jax.numpy as jnp
import numpy as np
from jax import lax
from jax.experimental import pallas as pl
from jax.experimental.pallas import tpu as pltpu

F32 = jnp.float32
BF16 = jnp.bfloat16

NORM_EPS = 1e-5
LNX_EPS = 64e-5
HEAD = 64
WINDOW = 128
KV_HEADS = 4
TOP_K = 4
SWIGLU_LIMIT = 7.0
SWIGLU_ALPHA = 1.702
DECAY_SCALE = float(np.exp(-0.5))

CHUNK = 64
ROW_BLOCK = 1152
ROW_SUB = 384
FF_TILE = 512
OUT_TILE = 256
VMEM_LIMIT = 56 * 1024 * 1024
HI = lax.Precision.HIGHEST


def _cparams(sem):
    return pltpu.CompilerParams(dimension_semantics=sem, vmem_limit_bytes=VMEM_LIMIT)


def _resident(shape):
    nd = len(shape)
    return pl.BlockSpec(shape, lambda *_: (0,) * nd, pipeline_mode=pl.Buffered(1))


def _rmsnorm(x, g):
    return x * lax.rsqrt(jnp.mean(x * x, axis=-1, keepdims=True) + NORM_EPS) * g


def _sigmoid(x):
    return 1.0 / (1.0 + jnp.exp(-x))


def _pack_bf16_pair(lo, hi):
    lo_bits = lax.bitcast_convert_type(lo.astype(BF16).astype(F32), jnp.uint32)
    hi_bits = lax.bitcast_convert_type(hi.astype(BF16).astype(F32), jnp.uint32)
    return (hi_bits & jnp.uint32(0xFFFF0000)) | (lo_bits >> 16)


def _unpack_bf16_pair(packed):
    lo = lax.bitcast_convert_type(packed << 16, F32)
    hi = lax.bitcast_convert_type(packed & jnp.uint32(0xFFFF0000), F32)
    return lo, hi


def _norm_proj_kernel(x_ref, g_ref, w_ref, b_ref, o_ref):
    h = _rmsnorm(x_ref[...], g_ref[...]).astype(BF16)
    z = jnp.dot(h, w_ref[...], preferred_element_type=F32) + b_ref[...]
    o_ref[...] = z.astype(o_ref.dtype)


def _norm_proj(x, g, w, b, out_dtype, tm):
    n, d = x.shape
    cols = w.shape[1]
    return pl.pallas_call(
        _norm_proj_kernel,
        grid=(n // tm,),
        in_specs=[pl.BlockSpec((tm, d), lambda i: (i, 0)), _resident((1, d)), _resident((d, cols)),
                  _resident((1, cols))],
        out_specs=pl.BlockSpec((tm, cols), lambda i: (i, 0)),
        out_shape=jax.ShapeDtypeStruct((n, cols), out_dtype),
        compiler_params=_cparams(("parallel",)),
        name="norm_proj",
    )(x, g, w, b)


def _head_sums(x):
    rows, c = x.shape
    lane = lax.broadcasted_iota(jnp.int32, (rows, 128), 1)
    low = lane < HEAD
    parts = []
    for gi in range(c // 128):
        xg = x[:, gi * 128:(gi + 1) * 128]
        s_lo = jnp.sum(jnp.where(low, xg, 0.0), axis=-1, keepdims=True)
        s_hi = jnp.sum(jnp.where(low, 0.0, xg), axis=-1, keepdims=True)
        parts.append(jnp.where(low, s_lo, s_hi))
    return jnp.concatenate(parts, axis=-1)


def _prep_kernel(seq_blocks, c, x_ref, n1_ref, win_ref, mu_ref, wlora_ref, wgate_ref, w0_ref, a0_ref, kk_ref, ka_ref,
                 r_ref, lw_ref, kf_ref, v_ref, kn_ref, ba_ref, g_ref, last_ref):
    i = pl.program_id(0)

    @pl.when(i == 0)
    def _():
        last_ref[...] = jnp.zeros_like(last_ref)

    h = _rmsnorm(x_ref[...], n1_ref[...]).astype(BF16)
    z = jnp.dot(h, win_ref[...], preferred_element_type=F32)
    tm = z.shape[0]
    prev = jnp.where(i % seq_blocks == 0, 0.0, last_ref[0:1, :])
    last_ref[0:1, :] = z[tm - 1:tm, :]
    row = lax.broadcasted_iota(jnp.int32, z.shape, 0)
    shifted = jnp.where(row == 0, prev, pltpu.roll(z, 1, 0))
    zs = z + (shifted - z) * mu_ref[...]
    r = zs[:, 0:c]
    k = zs[:, c:2 * c]
    v = zs[:, 2 * c:3 * c]
    zwa = zs[:, 3 * c:3 * c + 128]
    zg = zs[:, 3 * c + 128:3 * c + 256]
    lane = lax.broadcasted_iota(jnp.int32, (tm, 128), 1)
    lora_in = jnp.where(lane < 64, jnp.tanh(zwa), zwa)
    up = jnp.dot(lora_in, wlora_ref[...], preferred_element_type=F32, precision=HI)
    u = w0_ref[...] + up[:, 0:c]
    a = _sigmoid(a0_ref[...] + up[:, c:2 * c])
    g = jnp.dot(_sigmoid(zg), wgate_ref[...], preferred_element_type=F32, precision=HI)
    lw = -DECAY_SCALE * _sigmoid(u)
    kk = k * kk_ref[...]
    kn = kk / jnp.maximum(jnp.sqrt(_head_sums(kk * kk)), 1e-12)
    kf = k * (1.0 + (a - 1.0) * ka_ref[...])
    r_ref[...] = r
    lw_ref[...] = lw
    kf_ref[...] = kf
    v_ref[...] = v
    kn_ref[...] = kn
    ba_ref[...] = kn * a
    g_ref[...] = g


def _rwkv_prep(x, norm_g, w_in_rwkv, seq, c, mix_mu, w_lora, w_gate, w0, a0, k_k, k_a, tm):
    n, d = x.shape
    zc = w_in_rwkv.shape[1]
    row_spec = pl.BlockSpec((tm, c), lambda i: (i, 0))
    out = jax.ShapeDtypeStruct((n, c), F32)
    return pl.pallas_call(
        functools.partial(_prep_kernel, seq // tm, c),
        grid=(n // tm,),
        in_specs=[pl.BlockSpec((tm, d), lambda i: (i, 0)), _resident((1, d)), _resident((d, zc)),
                  _resident((1, zc)), _resident((128, 2 * c)), _resident((128, c)),
                  _resident((1, c)), _resident((1, c)), _resident((1, c)), _resident((1, c))],
        out_specs=[row_spec] * 7,
        out_shape=[out] * 7,
        scratch_shapes=[pltpu.VMEM((8, zc), F32)],
        compiler_params=_cparams(("arbitrary",)),
        name="rwkv_prep",
    )(x, norm_g, w_in_rwkv, mix_mu, w_lora, w_gate, w0, a0, k_k, k_a)


def _split3(x):
    h1 = x.astype(BF16)
    r1 = x - h1.astype(F32)
    h2 = r1.astype(BF16)
    h3 = (r1 - h2.astype(F32)).astype(BF16)
    return h1, h2, h3


def _mm(a, b):
    return jnp.dot(a, b, preferred_element_type=F32)


def _mm_nt(a, b):
    return lax.dot_general(a, b, (((1,), (1,)), ((), ())), preferred_element_type=F32)


def _mm_tn(a, b):
    return lax.dot_general(a, b, (((0,), (0,)), ((), ())), preferred_element_type=F32)


GROUP_HEADS = 4
GROUP_LANES = GROUP_HEADS * HEAD


def _block_diag(x, mask):
    return jnp.concatenate([x.astype(BF16)] * GROUP_HEADS, axis=0) * mask


def _scan_kernel(r_ref, lw_ref, kf_ref, v_ref, kn_ref, ba_ref, g_ref, rk_ref, lng_ref, lnb_ref, mask_ref, y_ref,
                 s_ref):
    @pl.when(pl.program_id(0) == 0)
    def _():
        s_ref[...] = jnp.zeros_like(s_ref)

    t = CHUNK
    nb = r_ref.shape[0]
    c = r_ref.shape[2]
    gl = GROUP_LANES
    n_groups = c // gl
    mask = mask_ref[...]
    ti = lax.broadcasted_iota(jnp.int32, (t, t), 0)
    si = lax.broadcasted_iota(jnp.int32, (t, t), 1)
    tri = jnp.where(si <= ti, 1.0, 0.0).astype(BF16)
    row = lax.broadcasted_iota(jnp.int32, (t, gl), 0)
    col = lax.broadcasted_iota(jnp.int32, (t, gl), 1) & (HEAD - 1)
    strict = col < row
    incl = col <= row
    eye = jnp.where(col == row, 1.0, 0.0)

    prep = []
    for b in range(nb):
        lw = lw_ref[b]
        cum = sum(jnp.dot(tri, part, preferred_element_type=F32) for part in _split3(lw))
        cum_end = cum[t - 1:t, :]
        e_pos = jnp.exp(cum)
        e_neg = jnp.exp(-cum)
        e_end = jnp.exp(cum_end - cum)
        r = r_ref[b]
        kf = kf_ref[b]
        ba = ba_ref[b]
        prep.append(dict(
            rt=r * e_pos,
            at=-kn_ref[b] * jnp.exp(cum - lw),
            bt=ba * e_neg,
            kt=kf * e_neg,
            bh=ba * e_end,
            kh=kf * e_end,
            v=v_ref[b],
            w_end=jnp.exp(cum_end),
            rkf=r * kf * rk_ref[...]))

    probs = [(b, gi) for b in range(nb) for gi in range(n_groups)]

    def part(b, gi, name):
        return prep[b][name][:, gi * gl:(gi + 1) * gl]

    a_ab, a_ak, a_rb, a_rk = [], [], [], []
    for b, gi in probs:
        lhs = jnp.concatenate([part(b, gi, "at"), part(b, gi, "rt")], axis=0).astype(BF16)
        rhs = jnp.concatenate([_block_diag(part(b, gi, "bt"), mask), _block_diag(part(b, gi, "kt"), mask)], axis=0)
        amat = _mm_nt(lhs, rhs)
        a_ab.append(jnp.where(strict, amat[0:t, 0:gl], 0.0))
        a_ak.append(jnp.where(strict, amat[0:t, gl:2 * gl], 0.0))
        a_rb.append(jnp.where(incl, amat[t:2 * t, 0:gl], 0.0))
        a_rk.append(jnp.where(incl, amat[t:2 * t, gl:2 * gl], 0.0))

    vbd = [_block_diag(part(b, gi, "v"), mask) for b, gi in probs]
    av = [_mm(a_ak[i].astype(BF16), vbd[i]) for i in range(len(probs))]
    minv = [eye + a for a in a_ab]
    power = [_mm(a.astype(BF16), _block_diag(a, mask)) for a in a_ab]
    span = 2
    while span < t:
        last = span * 2 >= t
        for i in range(len(probs)):
            pbd = _block_diag(power[i], mask)
            if last:
                minv[i] = minv[i] + _mm(minv[i].astype(BF16), pbd)
            else:
                both = _mm(jnp.concatenate([power[i], minv[i]], axis=0).astype(BF16), pbd)
                power[i] = both[0:t]
                minv[i] = minv[i] + both[t:2 * t]
        span *= 2

    s0 = [s_ref[b, gi] for b, gi in probs]
    sbd = [_block_diag(s, mask) for s in s0]
    ps = []
    for i, (b, gi) in enumerate(probs):
        lhs = jnp.concatenate([part(b, gi, "at"), part(b, gi, "rt")], axis=0).astype(BF16)
        ps.append(_mm_nt(lhs, sbd[i]))
    u = [_mm(minv[i].astype(BF16), _block_diag(ps[i][0:t] + av[i], mask)) for i in range(len(probs))]
    ys = []
    for i, (b, gi) in enumerate(probs):
        lhs = jnp.concatenate([a_rb[i], a_rk[i]], axis=1).astype(BF16)
        rhs = jnp.concatenate([_block_diag(u[i], mask), vbd[i]], axis=0)
        ys.append(ps[i][t:2 * t] + _mm(lhs, rhs))
        uv = jnp.concatenate([u[i], part(b, gi, "v")], axis=0).astype(BF16)
        bk = jnp.concatenate([part(b, gi, "bh"), part(b, gi, "kh")], axis=0).astype(BF16)
        full = _mm_tn(uv, bk) * mask.astype(F32)
        upd = full[0:HEAD]
        for hh in range(1, GROUP_HEADS):
            upd = upd + full[hh * HEAD:(hh + 1) * HEAD]
        s_ref[b, gi] = s0[i] * part(b, gi, "w_end") + upd

    for b in range(nb):
        y = jnp.concatenate([ys[b * n_groups + gi] for gi in range(n_groups)], axis=1)
        mu = _head_sums(y) * (1.0 / HEAD)
        yc = y - mu
        var = _head_sums(yc * yc) * (1.0 / HEAD)
        yn = yc * lax.rsqrt(var + LNX_EPS)
        bonus = _head_sums(prep[b]["rkf"]) * prep[b]["v"]
        y_ref[b] = ((yn * lng_ref[...] + lnb_ref[...] + bonus) * g_ref[b]).astype(y_ref.dtype)


def _rwkv_scan(r, lw, kf, v, kn, ba, g, r_k, lnx_g, lnx_b):
    b, s, c = r.shape
    blk = pl.BlockSpec((b, CHUNK, c), lambda ci: (0, ci, 0))
    hid = np.arange(GROUP_LANES) // HEAD
    mask = jnp.asarray(hid[:, None] == hid[None, :], BF16)
    return pl.pallas_call(
        _scan_kernel,
        grid=(s // CHUNK,),
        in_specs=[blk] * 7 + [_resident((1, c))] * 3 + [_resident((GROUP_LANES, GROUP_LANES))],
        out_specs=blk,
        out_shape=jax.ShapeDtypeStruct((b, s, c), BF16),
        scratch_shapes=[pltpu.VMEM((b, c // GROUP_LANES, HEAD, GROUP_LANES), F32)],
        compiler_params=_cparams(("arbitrary",)),
        name="rwkv_scan",
    )(r, lw, kf, v, kn, ba, g, r_k, lnx_g, lnx_b, mask)


def _attn_kernel(q_ref, kc_ref, kp_ref, vc_ref, vp_ref, sink_ref, o_ref):
    first = pl.program_id(1) == 0
    w = WINDOW
    group = q_ref.shape[-1] // HEAD // KV_HEADS
    row = lax.broadcasted_iota(jnp.int32, (group * w, 2 * w), 0)
    qi = row & (w - 1)
    kj = lax.broadcasted_iota(jnp.int32, (group * w, 2 * w), 1)
    lo = jnp.where(first, jnp.maximum(qi, w - 1), qi)
    valid = (kj > lo) & (kj <= qi + w)
    grow = lax.broadcasted_iota(jnp.int32, (group * w, 1), 0) // w
    q = q_ref[...]
    outs = []
    for hk in range(KV_HEADS):
        ksl = slice(hk * HEAD, (hk + 1) * HEAD)
        kcat = jnp.concatenate([kp_ref[:, ksl], kc_ref[:, ksl]], axis=0)
        vcat = jnp.concatenate([vp_ref[:, ksl], vc_ref[:, ksl]], axis=0)
        qg = jnp.concatenate([q[:, (hk * group + gi) * HEAD:(hk * group + gi + 1) * HEAD] for gi in range(group)],
                             axis=0)
        sink = jnp.zeros((group * w, 1), F32)
        for gi in range(group):
            sink = jnp.where(grow == gi, sink_ref[hk * group + gi:hk * group + gi + 1, 0:1], sink)
        s = _mm_nt(qg, kcat) * (HEAD ** -0.5)
        s = jnp.where(valid, s, -1e30)
        m = jnp.maximum(jnp.max(s, axis=-1, keepdims=True), sink)
        p = jnp.exp(s - m)
        pb = p.astype(BF16)
        psum = _mm(pb, jnp.ones((2 * w, HEAD), BF16))
        o = _mm(pb, vcat) / (psum + jnp.exp(sink - m))
        for gi in range(group):
            outs.append(o[gi * w:(gi + 1) * w, :])
    o_ref[...] = jnp.concatenate(outs, axis=-1).astype(o_ref.dtype)


def _attention(q, k, v, sinks_b):
    b, s, qc = q.shape
    kc = k.shape[-1]
    nb = s // WINDOW
    cur = lambda bi, i: (bi, i, 0)
    prev = lambda bi, i: (bi, jnp.maximum(i - 1, 0), 0)
    return pl.pallas_call(
        _attn_kernel,
        grid=(b, nb),
        in_specs=[pl.BlockSpec((None, WINDOW, qc), cur),
                  pl.BlockSpec((None, WINDOW, kc), cur), pl.BlockSpec((None, WINDOW, kc), prev),
                  pl.BlockSpec((None, WINDOW, kc), cur), pl.BlockSpec((None, WINDOW, kc), prev),
                  _resident(sinks_b.shape)],
        out_specs=pl.BlockSpec((None, WINDOW, qc), cur),
        out_shape=jax.ShapeDtypeStruct((b, s, qc), BF16),
        compiler_params=_cparams(("parallel", "parallel")),
        name="swa_attention",
    )(q, k, k, v, v, sinks_b)


MERGE_SPLIT = 2


def _merge_kernel(n_exp, x_ref, yr_ref, ya_ref, gr_ref, ga_ref, wur_ref, wua_ref, wo_ref, n2_ref, wr_ref, br_ref,
                  x1_ref, h2_ref, gate_ref, idx_ref):
    tm = x_ref.shape[0] // MERGE_SPLIT
    parts = [slice(p * tm, (p + 1) * tm) for p in range(MERGE_SPLIT)]
    ups = [(_mm(yr_ref[rs, :], wur_ref[...]), _mm(ya_ref[rs, :], wua_ref[...])) for rs in parts]
    merged = [_sigmoid(gr_ref[rs, :].astype(F32)) * ur + _sigmoid(ga_ref[rs, :].astype(F32)) * ua
              for rs, (ur, ua) in zip(parts, ups)]
    x1s = [x_ref[rs, :] + _mm(m.astype(BF16), wo_ref[...]) for rs, m in zip(parts, merged)]
    h2s = []
    for rs, x1 in zip(parts, x1s):
        x1_ref[rs, :] = x1
        h2 = _rmsnorm(x1, n2_ref[...])
        half = h2.shape[1] // 2
        h2_ref[rs, :] = _pack_bf16_pair(h2[:, :half], h2[:, half:])
        h2s.append(h2)
    all_logits = [jnp.dot(h2, wr_ref[...], preferred_element_type=F32, precision=HI) + br_ref[...] for h2 in h2s]
    lane_e = lax.broadcasted_iota(jnp.int32, (tm, n_exp), 1).astype(F32)
    lane_o = lax.broadcasted_iota(jnp.int32, (tm, 128), 1)
    for rs, logits in zip(parts, all_logits):
        vals, idxs = [], []
        for _ in range(TOP_K):
            mx = jnp.max(logits, axis=-1, keepdims=True)
            ix = jnp.min(jnp.where(logits == mx, lane_e, float(n_exp)), axis=-1, keepdims=True)
            vals.append(mx)
            idxs.append(ix)
            logits = jnp.where(lane_e == ix, -jnp.inf, logits)
        exps = [jnp.exp(vk - vals[0]) for vk in vals]
        tot = exps[0] + exps[1] + exps[2] + exps[3]
        gate_o, idx_o = jnp.zeros((tm, 128), F32), jnp.zeros((tm, 128), F32)
        for kk in range(TOP_K):
            gate_o = jnp.where(lane_o == kk, exps[kk] / tot, gate_o)
            idx_o = jnp.where(lane_o == kk, idxs[kk], idx_o)
        gate_ref[rs, :] = gate_o
        idx_ref[rs, :] = idx_o.astype(jnp.int32)


def _merge(x, y_rwkv, y_attn, gates, w_up_r, w_up_a, w_out, norm2_g, w_router, b_router, tm):
    n, d = x.shape
    c = y_rwkv.shape[1]
    n_exp = w_router.shape[1]
    row = lambda cols, j=0: pl.BlockSpec((tm, cols), lambda i: (i, j))
    return pl.pallas_call(
        functools.partial(_merge_kernel, n_exp),
        grid=(n // tm,),
        in_specs=[row(d), row(c), row(c), row(d, 0), row(d, 1),
                  _resident((c, d)), _resident((c, d)), _resident((d, d)), _resident((1, d)),
                  _resident((d, n_exp)), _resident((1, n_exp))],
        out_specs=[row(d), row(d // 2), row(128), row(128)],
        out_shape=[jax.ShapeDtypeStruct((n, d), F32), jax.ShapeDtypeStruct((n, d // 2), jnp.uint32),
                   jax.ShapeDtypeStruct((n, 128), F32), jax.ShapeDtypeStruct((n, 128), jnp.int32)],
        compiler_params=_cparams(("parallel",)),
        name="merge_router",
    )(x, y_rwkv, y_attn, gates, gates, w_up_r, w_up_a, w_out, norm2_g, w_router, b_router)


GATHER_UNROLL = 8


def _start_row_gather(src_hbm, idx_ref, buf, sem):
    groups, unroll, _ = buf.shape
    assert unroll == GATHER_UNROLL

    def start(g, carry):
        for u in range(GATHER_UNROLL):
            src_row = idx_ref[0, 0, g * GATHER_UNROLL + u]
            pltpu.make_async_copy(src_hbm.at[pl.ds(src_row, 1)], buf.at[g, pl.ds(u, 1)], sem).start(priority=u % 2)
        return carry

    lax.fori_loop(0, groups, start, 0)


def _wait_row_gather(buf, sem):
    pltpu.make_async_copy(buf, buf, sem).wait()


def _dispatch_kernel(brows_ref, tok_ref, tok_next_ref, h_hbm, o_ref, buf, sems):
    s = pl.program_id(0)
    n_sub = ROW_BLOCK // ROW_SUB

    def has_rows(t):
        return lax.rem(t, n_sub) * ROW_SUB < brows_ref[lax.div(t, n_sub)]

    slot = s % 2

    @pl.when((s == 0) & has_rows(0))
    def _():
        _start_row_gather(h_hbm, tok_ref, buf.at[0], sems.at[0])

    @pl.when((s + 1 < pl.num_programs(0)) & has_rows(jnp.minimum(s + 1, pl.num_programs(0) - 1)))
    def _():
        _start_row_gather(h_hbm, tok_next_ref, buf.at[1 - slot], sems.at[1 - slot])

    @pl.when(has_rows(s))
    def _():
        _wait_row_gather(buf.at[slot], sems.at[slot])
        half = o_ref.shape[1] // 2
        lo, hi = _unpack_bf16_pair(buf[slot].reshape(ROW_SUB, half))
        o_ref[:, :half] = lo.astype(o_ref.dtype)
        o_ref[:, half:] = hi.astype(o_ref.dtype)

    @pl.when(jnp.logical_not(has_rows(s)))
    def _():
        o_ref[...] = jnp.zeros_like(o_ref)


def _dispatch(h2p, row_tok, block_rows, n_blocks):
    half = h2p.shape[1]
    d = 2 * half
    steps = n_blocks * (ROW_BLOCK // ROW_SUB)
    grid_spec = pltpu.PrefetchScalarGridSpec(
        num_scalar_prefetch=1,
        grid=(steps,),
        in_specs=[pl.BlockSpec((1, 1, ROW_SUB), lambda s, br: (s, 0, 0), memory_space=pltpu.SMEM),
                  pl.BlockSpec((1, 1, ROW_SUB), lambda s, br: (jnp.minimum(s + 1, steps - 1), 0, 0),
                               memory_space=pltpu.SMEM),
                  pl.BlockSpec(memory_space=pl.ANY)],
        out_specs=pl.BlockSpec((ROW_SUB, d), lambda s, br: (s, 0)),
        scratch_shapes=[pltpu.VMEM((2, ROW_SUB // GATHER_UNROLL, GATHER_UNROLL, half), jnp.uint32),
                        pltpu.SemaphoreType.DMA((2,))],
    )
    tok = row_tok.reshape(steps, 1, ROW_SUB)
    return pl.pallas_call(
        _dispatch_kernel,
        grid_spec=grid_spec,
        out_shape=jax.ShapeDtypeStruct((n_blocks * ROW_BLOCK, d), BF16),
        compiler_params=_cparams(("arbitrary",)),
        name="moe_dispatch",
    )(block_rows, tok, tok, h2p)


def _ffn_kernel(nvalid_ref, bexp_ref, brows_ref, xs_ref, w1g_ref, w1l_ref, b1g_ref, b1l_ref, w2a_ref, w2b_ref,
                b2a_ref, b2b_ref, o_ref, act_ref):
    i = pl.program_id(0)
    j = pl.program_id(1)
    n_ff = act_ref.shape[0]
    rows_valid = brows_ref[i]
    n_sub = ROW_BLOCK // ROW_SUB

    full = rows_valid == ROW_BLOCK
    whole = slice(0, ROW_BLOCK)
    subs = [slice(sb * ROW_SUB, (sb + 1) * ROW_SUB) for sb in range(n_sub)]

    def hidden(rs, w1g, w1l):
        x = xs_ref[rs, :]
        hg = _mm(x, w1g) + b1g_ref[...]
        hl = _mm(x, w1l) + b1l_ref[...]
        glu = jnp.minimum(hg, SWIGLU_LIMIT)
        lin = jnp.clip(hl, -SWIGLU_LIMIT, SWIGLU_LIMIT)
        act_ref[j, rs, :] = (glu * _sigmoid(SWIGLU_ALPHA * glu) * (lin + 1.0)).astype(BF16)

    def project(rs, w2a, w2b):
        def half(w2, b2_ref):
            acc = _mm(act_ref[0, rs, :], w2[0:FF_TILE])
            for jf in range(1, n_ff):
                acc = acc + _mm(act_ref[jf, rs, :], w2[jf * FF_TILE:(jf + 1) * FF_TILE])
            return acc + b2_ref[...]

        o_ref[rs, :] = _pack_bf16_pair(half(w2a, b2a_ref), half(w2b, b2b_ref))

    @pl.when((j < n_ff) & full)
    def _():
        hidden(whole, w1g_ref[...].astype(BF16), w1l_ref[...].astype(BF16))

    @pl.when((j < n_ff) & jnp.logical_not(full))
    def _():
        w1g = w1g_ref[...].astype(BF16)
        w1l = w1l_ref[...].astype(BF16)
        for sb, rs in enumerate(subs):
            pl.when(sb * ROW_SUB < rows_valid)(functools.partial(hidden, rs, w1g, w1l))

    @pl.when((j >= n_ff) & full)
    def _():
        project(whole, w2a_ref[...].astype(BF16), w2b_ref[...].astype(BF16))

    @pl.when((j >= n_ff) & jnp.logical_not(full))
    def _():
        w2a = w2a_ref[...].astype(BF16)
        w2b = w2b_ref[...].astype(BF16)
        for sb, rs in enumerate(subs):
            pl.when(sb * ROW_SUB < rows_valid)(functools.partial(project, rs, w2a, w2b))

            @pl.when(sb * ROW_SUB >= rows_valid)
            def _():
                o_ref[rs, :] = jnp.zeros((ROW_SUB, o_ref.shape[1]), o_ref.dtype)


def _expert_ffn(xs, w1, b1, w2, b2, block_expert, block_rows, n_valid, n_blocks):
    d = xs.shape[1]
    n_exp, _, ff2 = w1.shape
    ff = ff2 // 2
    n_ff = ff // FF_TILE
    n_out = d // 2 // OUT_TILE

    def hid(i, j, nv):
        return jnp.where(i < nv[0], jnp.minimum(j, n_ff - 1), n_ff - 1)

    def w2_expert(i, j, be):
        return jnp.where(j < n_ff, be[jnp.maximum(i - 1, 0)], be[i])

    def w2_tile(i, j, nv):
        return jnp.where((i < nv[0]) & (j >= n_ff), j - n_ff, n_out - 1)

    grid_spec = pltpu.PrefetchScalarGridSpec(
        num_scalar_prefetch=3,
        grid=(n_blocks, n_ff + n_out),
        in_specs=[pl.BlockSpec((ROW_BLOCK, d), lambda i, j, nv, be, br: (i, 0)),
                  pl.BlockSpec((None, d, FF_TILE), lambda i, j, nv, be, br: (be[i], 0, hid(i, j, nv))),
                  pl.BlockSpec((None, d, FF_TILE), lambda i, j, nv, be, br: (be[i], 0, n_ff + hid(i, j, nv))),
                  pl.BlockSpec((None, 1, FF_TILE), lambda i, j, nv, be, br: (be[i], 0, hid(i, j, nv))),
                  pl.BlockSpec((None, 1, FF_TILE), lambda i, j, nv, be, br: (be[i], 0, n_ff + hid(i, j, nv))),
                  pl.BlockSpec((None, ff, OUT_TILE),
                               lambda i, j, nv, be, br: (w2_expert(i, j, be), 0, w2_tile(i, j, nv))),
                  pl.BlockSpec((None, ff, OUT_TILE),
                               lambda i, j, nv, be, br: (w2_expert(i, j, be), 0, n_out + w2_tile(i, j, nv))),
                  pl.BlockSpec((None, 1, OUT_TILE),
                               lambda i, j, nv, be, br: (w2_expert(i, j, be), 0, w2_tile(i, j, nv))),
                  pl.BlockSpec((None, 1, OUT_TILE),
                               lambda i, j, nv, be, br: (w2_expert(i, j, be), 0, n_out + w2_tile(i, j, nv)))],
        out_specs=pl.BlockSpec((ROW_BLOCK, OUT_TILE), lambda i, j, nv, be, br: (i, jnp.maximum(j - n_ff, 0))),
        scratch_shapes=[pltpu.VMEM((n_ff, ROW_BLOCK, FF_TILE), BF16)],
    )
    b1r = b1.reshape(n_exp, 1, ff2)
    b2r = b2.reshape(n_exp, 1, d)
    return pl.pallas_call(
        _ffn_kernel,
        grid_spec=grid_spec,
        out_shape=jax.ShapeDtypeStruct((n_blocks * ROW_BLOCK, d // 2), jnp.uint32),
        compiler_params=_cparams(("arbitrary", "arbitrary")),
        name="moe_ffn",
    )(n_valid, block_expert, block_rows, xs, w1, w1, b1r, b1r, w2, w2, b2r, b2r)


def _combine_kernel(pos_ref, pos_next_ref, x1_ref, gate_ref, nf_ref, ys_hbm, o_ref, buf, sems):
    s = pl.program_id(0)
    tc = x1_ref.shape[0]
    slot = s % 2

    @pl.when(s == 0)
    def _():
        _start_row_gather(ys_hbm, pos_ref, buf.at[0], sems.at[0])

    @pl.when(s + 1 < pl.num_programs(0))
    def _():
        _start_row_gather(ys_hbm, pos_next_ref, buf.at[1 - slot], sems.at[1 - slot])

    _wait_row_gather(buf.at[slot], sems.at[slot])
    half = x1_ref.shape[1] // 2
    groups = tc // GATHER_UNROLL
    gate = gate_ref[...]
    acc_lo = x1_ref[:, :half]
    acc_hi = x1_ref[:, half:]
    for k in range(TOP_K):
        lo, hi = _unpack_bf16_pair(buf[slot, k * groups:(k + 1) * groups].reshape(tc, half))
        acc_lo = acc_lo + gate[:, k:k + 1] * lo
        acc_hi = acc_hi + gate[:, k:k + 1] * hi
    o_ref[...] = _rmsnorm(jnp.concatenate([acc_lo, acc_hi], axis=1), nf_ref[...])


def _combine(x1, gate, pos, ys, normf_g, tc):
    n, d = x1.shape
    nt = n // tc
    return pl.pallas_call(
        _combine_kernel,
        grid=(nt,),
        in_specs=[pl.BlockSpec((1, 1, TOP_K * tc), lambda i: (i, 0, 0), memory_space=pltpu.SMEM),
                  pl.BlockSpec((1, 1, TOP_K * tc), lambda i: (jnp.minimum(i + 1, nt - 1), 0, 0),
                               memory_space=pltpu.SMEM),
                  pl.BlockSpec((tc, d), lambda i: (i, 0)), pl.BlockSpec((tc, 128), lambda i: (i, 0)),
                  _resident((1, d)), pl.BlockSpec(memory_space=pl.ANY)],
        out_specs=pl.BlockSpec((tc, d), lambda i: (i, 0)),
        out_shape=jax.ShapeDtypeStruct((n, d), F32),
        scratch_shapes=[pltpu.VMEM((2, TOP_K * tc // GATHER_UNROLL, GATHER_UNROLL, d // 2), jnp.uint32),
                        pltpu.SemaphoreType.DMA((2,))],
        compiler_params=_cparams(("arbitrary",)),
        name="moe_combine",
    )(pos, pos, x1, gate, normf_g, ys)


def _route(top_idx, n_exp, tc):
    n = top_idx.shape[0]
    n_assign = n * TOP_K
    n_blocks = n_assign // ROW_BLOCK + n_exp
    tile = min(1024, n)
    hot = (top_idx[:, :, None] == jnp.arange(n_exp, dtype=jnp.int32)).any(axis=1).reshape(n // tile, tile, n_exp)
    earlier = jnp.asarray(np.tril(np.ones((tile, tile), np.float32), -1), BF16)
    within = jnp.einsum("ts,bse->bte", earlier, hot.astype(BF16), preferred_element_type=F32)
    tile_tot = jnp.sum(hot, axis=1, dtype=jnp.int32)
    tile_off = jnp.cumsum(tile_tot, axis=0) - tile_tot
    rank = (within.astype(jnp.int32) + tile_off[:, None, :]).reshape(n, n_exp)
    counts = jnp.sum(tile_tot, axis=0)
    padded = (counts + ROW_BLOCK - 1) // ROW_BLOCK * ROW_BLOCK
    pad_end = jnp.cumsum(padded)
    pad_start = pad_end - padded
    grp_start = jnp.cumsum(counts) - counts
    slot = jnp.take_along_axis(rank + pad_start[None, :], top_idx, axis=1).reshape(-1)
    block_start = jnp.arange(n_blocks, dtype=jnp.int32) * ROW_BLOCK
    block_expert = jnp.minimum(jnp.sum(pad_end[None, :] <= block_start[:, None], axis=1), n_exp - 1).astype(jnp.int32)
    n_valid = (pad_end[-1] // ROW_BLOCK).astype(jnp.int32).reshape(1)
    order = jnp.argsort(top_idx.reshape(-1), stable=True)
    row_e = jnp.repeat(block_expert, ROW_BLOCK)
    row_r = jnp.arange(n_blocks * ROW_BLOCK, dtype=jnp.int32) - pad_start[row_e]
    src = jnp.clip(grp_start[row_e] + row_r, 0, n_assign - 1)
    row_tok = jnp.where((row_r >= 0) & (row_r < counts[row_e]), order[src] // TOP_K, 0).astype(jnp.int32)
    pos = slot.reshape(n // tc, tc, TOP_K).transpose(0, 2, 1).reshape(n // tc, 1, TOP_K * tc)
    block_rows = jnp.clip(counts[block_expert] - (block_start - pad_start[block_expert]), 0, ROW_BLOCK)
    block_rows = jnp.where(jnp.arange(n_blocks) < n_valid[0], block_rows, 0).astype(jnp.int32)
    return row_tok, block_expert, block_rows, n_valid, pos, n_blocks


def _layer(x, norm1_g, w_in, mix_mu, w0, w_decay_up, a0, w_iclr_up, w_gate_up, k_k, k_a, r_k, lnx_g, lnx_b, b_qkv,
           sinks, w_up_rwkv, w_up_attn, w_out, norm2_g, w_router, b_router, w1, b1, w2, b2, normf_g):
    b, s, d = x.shape
    n = b * s
    c = w_up_rwkv.shape[0]
    lora = w_decay_up.shape[0]
    qc = w_up_attn.shape[0]
    kvc = KV_HEADS * HEAD
    rwkv_cols = 3 * c + 2 * lora + w_gate_up.shape[0]
    qkv_cols = qc + 2 * kvc
    row = lambda t: t.reshape(1, -1).astype(F32)
    xf = x.reshape(n, d)
    g1 = row(norm1_g)
    w_in_b = w_in.astype(BF16)

    qkv = _norm_proj(xf, g1, w_in_b[:, rwkv_cols:rwkv_cols + qkv_cols], row(b_qkv), BF16, 512)
    gates = _norm_proj(xf, g1, w_in_b[:, rwkv_cols + qkv_cols:], jnp.zeros((1, 2 * d), F32), BF16, 512)

    zl = jnp.zeros((lora, c), F32)
    w_lora = jnp.concatenate([jnp.concatenate([w_decay_up, zl], axis=1), jnp.concatenate([zl, w_iclr_up], axis=1)],
                             axis=0)
    r, lw, kf, v, kn, ba, g = _rwkv_prep(xf, g1, w_in_b[:, :rwkv_cols], s, c, row(mix_mu), w_lora, w_gate_up, row(w0),
                                         row(a0), row(k_k), row(k_a), 256)
    as3 = lambda t: t.reshape(b, s, c)
    y_rwkv = _rwkv_scan(as3(r), as3(lw), as3(kf), as3(v), as3(kn), as3(ba), as3(g), row(r_k), row(lnx_g),
                        row(lnx_b)).reshape(n, c)

    q = qkv[:, :qc].reshape(b, s, qc)
    ka = qkv[:, qc:qc + kvc].reshape(b, s, kvc)
    va = qkv[:, qc + kvc:].reshape(b, s, kvc)
    sinks_b = jnp.broadcast_to(sinks.astype(F32).reshape(-1, 1), (sinks.shape[0], 128))
    y_attn = _attention(q, ka, va, sinks_b).reshape(n, qc)

    x1, h2, gate, top_idx = _merge(xf, y_rwkv, y_attn, gates, w_up_rwkv.astype(BF16), w_up_attn.astype(BF16),
                                   w_out.astype(BF16), row(norm2_g), w_router, row(b_router), 256)

    tc = 256
    row_tok, block_expert, block_rows, n_valid, pos, n_blocks = _route(top_idx[:, :TOP_K], w_router.shape[1], tc)
    xs = _dispatch(h2, row_tok, block_rows, n_blocks)
    ys = _expert_ffn(xs, w1, b1, w2, b2, block_expert, block_rows, n_valid, n_blocks)
    out = _combine(x1, gate, pos, ys, row(normf_g), tc)
    return out.reshape(b, s, d)


def kernel(x, norm1_g, w_in, mix_mu, w0, w_decay_up, a0, w_iclr_up, w_gate_up, k_k, k_a, r_k, lnx_g, lnx_b, b_qkv,
           sinks, w_up_rwkv, w_up_attn, w_out, norm2_g, w_router, b_router, w1, b1, w2, b2, normf_g):
    assert w_in.shape[0] == 1, "single-layer block"
    return _layer(x, norm1_g[0], w_in[0], mix_mu[0], w0[0], w_decay_up[0], a0[0], w_iclr_up[0], w_gate_up[0],
                  k_k[0], k_a[0], r_k[0], lnx_g[0], lnx_b[0], b_qkv[0], sinks[0], w_up_rwkv[0], w_up_attn[0],
                  w_out[0], norm2_g[0], w_router[0], b_router[0], w1[0], b1[0], w2[0], b2[0], normf_g)
```

```python
import functools

import jax
import jax.numpy as jnp
import numpy as np
from jax import lax
from jax.experimental import pallas as pl
from jax.experimental.pallas import tpu as pltpu

F32 = jnp.float32
BF16 = jnp.bfloat16

NORM_EPS = 1e-5
LNX_EPS = 64e-5
HEAD = 64
WINDOW = 128
KV_HEADS = 4
TOP_K = 4
SWIGLU_LIMIT = 7.0
SWIGLU_ALPHA = 1.702
DECAY_SCALE = float(np.exp(-0.5))

CHUNK = 64
ROW_BLOCK = 1152
ROW_SUB = 384
FF_TILE = 512
OUT_TILE = 256
VMEM_LIMIT = 56 * 1024 * 1024
HI = lax.Precision.HIGHEST


def _cparams(sem):
    return pltpu.CompilerParams(dimension_semantics=sem, vmem_limit_bytes=VMEM_LIMIT)


def _resident(shape):
    nd = len(shape)
    return pl.BlockSpec(shape, lambda *_: (0,) * nd, pipeline_mode=pl.Buffered(1))


def _rmsnorm(x, g):
    return x * lax.rsqrt(jnp.mean(x * x, axis=-1, keepdims=True) + NORM_EPS) * g


def _sigmoid(x):
    return 1.0 / (1.0 + jnp.exp(-x))


def _pack_bf16_pair(lo, hi):
    lo_bits = lax.bitcast_convert_type(lo.astype(BF16).astype(F32), jnp.uint32)
    hi_bits = lax.bitcast_convert_type(hi.astype(BF16).astype(F32), jnp.uint32)
    return (hi_bits & jnp.uint32(0xFFFF0000)) | (lo_bits >> 16)


ROW_TILES = 8


def _store_row_chunk(ref, row0, chunk, value):
    ref[pl.ds(row0 * ROW_TILES + chunk, value.shape[0], stride=ROW_TILES), :] = value


def _store_row_tiles(ref, row0, packed):
    for j in range(ROW_TILES):
        _store_row_chunk(ref, row0, j, packed[:, j * 128:(j + 1) * 128])


def _unpack_bf16_pair(packed):
    lo = lax.bitcast_convert_type(packed << 16, F32)
    hi = lax.bitcast_convert_type(packed & jnp.uint32(0xFFFF0000), F32)
    return lo, hi


def _norm_proj_kernel(x_ref, g_ref, w_ref, b_ref, o_ref):
    h = _rmsnorm(x_ref[...], g_ref[...]).astype(BF16)
    z = jnp.dot(h, w_ref[...], preferred_element_type=F32) + b_ref[...]
    o_ref[...] = z.astype(o_ref.dtype)


def _norm_proj(x, g, w, b, out_dtype, tm):
    n, d = x.shape
    cols = w.shape[1]
    return pl.pallas_call(
        _norm_proj_kernel,
        grid=(n // tm,),
        in_specs=[pl.BlockSpec((tm, d), lambda i: (i, 0)), _resident((1, d)), _resident((d, cols)),
                  _resident((1, cols))],
        out_specs=pl.BlockSpec((tm, cols), lambda i: (i, 0)),
        out_shape=jax.ShapeDtypeStruct((n, cols), out_dtype),
        compiler_params=_cparams(("parallel",)),
        name="norm_proj",
    )(x, g, w, b)


def _head_sums(x):
    rows, c = x.shape
    lane = lax.broadcasted_iota(jnp.int32, (rows, 128), 1)
    low = lane < HEAD
    parts = []
    for gi in range(c // 128):
        xg = x[:, gi * 128:(gi + 1) * 128]
        s_lo = jnp.sum(jnp.where(low, xg, 0.0), axis=-1, keepdims=True)
        s_hi = jnp.sum(jnp.where(low, 0.0, xg), axis=-1, keepdims=True)
        parts.append(jnp.where(low, s_lo, s_hi))
    return jnp.concatenate(parts, axis=-1)


def _dot_bf16x3(a, w):
    a_hi = a.astype(BF16)
    a_lo = (a - a_hi.astype(F32)).astype(BF16)
    w_hi = w.astype(BF16)
    w_lo = (w - w_hi.astype(F32)).astype(BF16)
    return (jnp.dot(a_hi, w_hi, preferred_element_type=F32) + jnp.dot(a_lo, w_hi, preferred_element_type=F32)
            + jnp.dot(a_hi, w_lo, preferred_element_type=F32))


def _prep_kernel(seq_blocks, c, x_ref, n1_ref, win_ref, mu_ref, wlora_ref, wgate_ref, w0_ref, a0_ref, kk_ref, ka_ref,
                 r_ref, lw_ref, kf_ref, v_ref, kn_ref, ba_ref, g_ref, last_ref):
    i = pl.program_id(0)

    @pl.when(i == 0)
    def _():
        last_ref[...] = jnp.zeros_like(last_ref)

    h = _rmsnorm(x_ref[...], n1_ref[...]).astype(BF16)
    z = jnp.dot(h, win_ref[...], preferred_element_type=F32)
    tm = z.shape[0]
    prev = jnp.where(i % seq_blocks == 0, 0.0, last_ref[0:1, :])
    last_ref[0:1, :] = z[tm - 1:tm, :]
    row = lax.broadcasted_iota(jnp.int32, z.shape, 0)
    shifted = jnp.where(row == 0, prev, pltpu.roll(z, 1, 0))
    zs = z + (shifted - z) * mu_ref[...]
    r = zs[:, 0:c]
    k = zs[:, c:2 * c]
    v = zs[:, 2 * c:3 * c]
    zwa = zs[:, 3 * c:3 * c + 128]
    zg = zs[:, 3 * c + 128:3 * c + 256]
    lane = lax.broadcasted_iota(jnp.int32, (tm, 128), 1)
    lora_in = jnp.where(lane < 64, jnp.tanh(zwa), zwa)
    up = _dot_bf16x3(lora_in, wlora_ref[...])
    u = w0_ref[...] + up[:, 0:c]
    a = _sigmoid(a0_ref[...] + up[:, c:2 * c])
    g = _dot_bf16x3(_sigmoid(zg), wgate_ref[...])
    lw = -DECAY_SCALE * _sigmoid(u)
    kk = k * kk_ref[...]
    kn = kk / jnp.maximum(jnp.sqrt(_head_sums(kk * kk)), 1e-12)
    kf = k * (1.0 + (a - 1.0) * ka_ref[...])
    r_ref[...] = r
    lw_ref[...] = lw
    kf_ref[...] = kf
    v_ref[...] = v
    kn_ref[...] = kn
    ba_ref[...] = kn * a
    g_ref[...] = g


def _rwkv_prep(x, norm_g, w_in_rwkv, seq, c, mix_mu, w_lora, w_gate, w0, a0, k_k, k_a, tm):
    n, d = x.shape
    zc = w_in_rwkv.shape[1]
    row_spec = pl.BlockSpec((tm, c), lambda i: (i, 0))
    out = jax.ShapeDtypeStruct((n, c), F32)
    return pl.pallas_call(
        functools.partial(_prep_kernel, seq // tm, c),
        grid=(n // tm,),
        in_specs=[pl.BlockSpec((tm, d), lambda i: (i, 0)), _resident((1, d)), _resident((d, zc)),
                  _resident((1, zc)), _resident((128, 2 * c)), _resident((128, c)),
                  _resident((1, c)), _resident((1, c)), _resident((1, c)), _resident((1, c))],
        out_specs=[row_spec] * 7,
        out_shape=[out] * 7,
        scratch_shapes=[pltpu.VMEM((8, zc), F32)],
        compiler_params=_cparams(("arbitrary",)),
        name="rwkv_prep",
    )(x, norm_g, w_in_rwkv, mix_mu, w_lora, w_gate, w0, a0, k_k, k_a)


def _split3(x):
    h1 = x.astype(BF16)
    r1 = x - h1.astype(F32)
    h2 = r1.astype(BF16)
    h3 = (r1 - h2.astype(F32)).astype(BF16)
    return h1, h2, h3


def _mm(a, b):
    return jnp.dot(a, b, preferred_element_type=F32)


def _mm_nt(a, b):
    return lax.dot_general(a, b, (((1,), (1,)), ((), ())), preferred_element_type=F32)


def _mm_tn(a, b):
    return lax.dot_general(a, b, (((0,), (0,)), ((), ())), preferred_element_type=F32)


GROUP_HEADS = 4
GROUP_LANES = GROUP_HEADS * HEAD


def _block_diag(x, mask):
    return jnp.concatenate([x.astype(BF16)] * GROUP_HEADS, axis=0) * mask


def _scan_kernel(r_ref, lw_ref, kf_ref, v_ref, kn_ref, ba_ref, g_ref, rk_ref, lng_ref, lnb_ref, mask_ref, y_ref,
                 s_ref):
    @pl.when(pl.program_id(0) == 0)
    def _():
        s_ref[...] = jnp.zeros_like(s_ref)

    t = CHUNK
    nb = r_ref.shape[0]
    c = r_ref.shape[2]
    gl = GROUP_LANES
    n_groups = c // gl
    mask = mask_ref[...]
    ti = lax.broadcasted_iota(jnp.int32, (t, t), 0)
    si = lax.broadcasted_iota(jnp.int32, (t, t), 1)
    tri = jnp.where(si <= ti, 1.0, 0.0).astype(BF16)
    row = lax.broadcasted_iota(jnp.int32, (t, gl), 0)
    col = lax.broadcasted_iota(jnp.int32, (t, gl), 1) & (HEAD - 1)
    strict = col < row
    incl = col <= row
    eye = jnp.where(col == row, 1.0, 0.0)

    prep = []
    for b in range(nb):
        lw = lw_ref[b]
        cum = sum(jnp.dot(tri, part, preferred_element_type=F32) for part in _split3(lw))
        cum_end = cum[t - 1:t, :]
        e_pos = jnp.exp(cum)
        e_neg = jnp.exp(-cum)
        e_end = jnp.exp(cum_end - cum)
        r = r_ref[b]
        kf = kf_ref[b]
        ba = ba_ref[b]
        prep.append(dict(
            rt=r * e_pos,
            at=-kn_ref[b] * jnp.exp(cum - lw),
            bt=ba * e_neg,
            kt=kf * e_neg,
            bh=ba * e_end,
            kh=kf * e_end,
            v=v_ref[b],
            w_end=jnp.exp(cum_end),
            rkf=r * kf * rk_ref[...]))

    probs = [(b, gi) for b in range(nb) for gi in range(n_groups)]

    def part(b, gi, name):
        return prep[b][name][:, gi * gl:(gi + 1) * gl]

    a_ab, a_ak, a_rb, a_rk = [], [], [], []
    for b, gi in probs:
        lhs = jnp.concatenate([part(b, gi, "at"), part(b, gi, "rt")], axis=0).astype(BF16)
        rhs = jnp.concatenate([_block_diag(part(b, gi, "bt"), mask), _block_diag(part(b, gi, "kt"), mask)], axis=0)
        amat = _mm_nt(lhs, rhs)
        a_ab.append(jnp.where(strict, amat[0:t, 0:gl], 0.0))
        a_ak.append(jnp.where(strict, amat[0:t, gl:2 * gl], 0.0))
        a_rb.append(jnp.where(incl, amat[t:2 * t, 0:gl], 0.0))
        a_rk.append(jnp.where(incl, amat[t:2 * t, gl:2 * gl], 0.0))

    vbd = [_block_diag(part(b, gi, "v"), mask) for b, gi in probs]
    av = [_mm(a_ak[i].astype(BF16), vbd[i]) for i in range(len(probs))]
    minv = [eye + a for a in a_ab]
    power = [_mm(a.astype(BF16), _block_diag(a, mask)) for a in a_ab]
    span = 2
    while span < t:
        last = span * 2 >= t
        for i in range(len(probs)):
            pbd = _block_diag(power[i], mask)
            if last:
                minv[i] = minv[i] + _mm(minv[i].astype(BF16), pbd)
            else:
                both = _mm(jnp.concatenate([power[i], minv[i]], axis=0).astype(BF16), pbd)
                power[i] = both[0:t]
                minv[i] = minv[i] + both[t:2 * t]
        span *= 2

    s0 = [s_ref[b, gi] for b, gi in probs]
    sbd = [_block_diag(s, mask) for s in s0]
    ps = []
    for i, (b, gi) in enumerate(probs):
        lhs = jnp.concatenate([part(b, gi, "at"), part(b, gi, "rt")], axis=0).astype(BF16)
        ps.append(_mm_nt(lhs, sbd[i]))
    u = [_mm(minv[i].astype(BF16), _block_diag(ps[i][0:t] + av[i], mask)) for i in range(len(probs))]
    ys = []
    for i, (b, gi) in enumerate(probs):
        lhs = jnp.concatenate([a_rb[i], a_rk[i]], axis=1).astype(BF16)
        rhs = jnp.concatenate([_block_diag(u[i], mask), vbd[i]], axis=0)
        ys.append(ps[i][t:2 * t] + _mm(lhs, rhs))
        uv = jnp.concatenate([u[i], part(b, gi, "v")], axis=0).astype(BF16)
        bk = jnp.concatenate([part(b, gi, "bh"), part(b, gi, "kh")], axis=0).astype(BF16)
        full = _mm_tn(uv, bk) * mask.astype(F32)
        upd = full[0:HEAD]
        for hh in range(1, GROUP_HEADS):
            upd = upd + full[hh * HEAD:(hh + 1) * HEAD]
        s_ref[b, gi] = s0[i] * part(b, gi, "w_end") + upd

    for b in range(nb):
        y = jnp.concatenate([ys[b * n_groups + gi] for gi in range(n_groups)], axis=1)
        mu = _head_sums(y) * (1.0 / HEAD)
        yc = y - mu
        var = _head_sums(yc * yc) * (1.0 / HEAD)
        yn = yc * lax.rsqrt(var + LNX_EPS)
        bonus = _head_sums(prep[b]["rkf"]) * prep[b]["v"]
        y_ref[b] = ((yn * lng_ref[...] + lnb_ref[...] + bonus) * g_ref[b]).astype(y_ref.dtype)


def _rwkv_scan(r, lw, kf, v, kn, ba, g, r_k, lnx_g, lnx_b):
    b, s, c = r.shape
    blk = pl.BlockSpec((b, CHUNK, c), lambda ci: (0, ci, 0))
    hid = np.arange(GROUP_LANES) // HEAD
    mask = jnp.asarray(hid[:, None] == hid[None, :], BF16)
    return pl.pallas_call(
        _scan_kernel,
        grid=(s // CHUNK,),
        in_specs=[blk] * 7 + [_resident((1, c))] * 3 + [_resident((GROUP_LANES, GROUP_LANES))],
        out_specs=blk,
        out_shape=jax.ShapeDtypeStruct((b, s, c), BF16),
        scratch_shapes=[pltpu.VMEM((b, c // GROUP_LANES, HEAD, GROUP_LANES), F32)],
        compiler_params=_cparams(("arbitrary",)),
        name="rwkv_scan",
    )(r, lw, kf, v, kn, ba, g, r_k, lnx_g, lnx_b, mask)


def _attn_kernel(q_ref, kc_ref, kp_ref, vc_ref, vp_ref, sink_ref, o_ref):
    first = pl.program_id(1) == 0
    w = WINDOW
    group = q_ref.shape[-1] // HEAD // KV_HEADS
    row = lax.broadcasted_iota(jnp.int32, (group * w, 2 * w), 0)
    qi = row & (w - 1)
    kj = lax.broadcasted_iota(jnp.int32, (group * w, 2 * w), 1)
    lo = jnp.where(first, jnp.maximum(qi, w - 1), qi)
    valid = (kj > lo) & (kj <= qi + w)
    grow = lax.broadcasted_iota(jnp.int32, (group * w, 1), 0) // w
    q = q_ref[...]
    outs = []
    for hk in range(KV_HEADS):
        ksl = slice(hk * HEAD, (hk + 1) * HEAD)
        kcat = jnp.concatenate([kp_ref[:, ksl], kc_ref[:, ksl]], axis=0)
        vcat = jnp.concatenate([vp_ref[:, ksl], vc_ref[:, ksl]], axis=0)
        qg = jnp.concatenate([q[:, (hk * group + gi) * HEAD:(hk * group + gi + 1) * HEAD] for gi in range(group)],
                             axis=0)
        sink = jnp.zeros((group * w, 1), F32)
        for gi in range(group):
            sink = jnp.where(grow == gi, sink_ref[hk * group + gi:hk * group + gi + 1, 0:1], sink)
        s = _mm_nt(qg, kcat) * (HEAD ** -0.5)
        s = jnp.where(valid, s, -1e30)
        m = jnp.maximum(jnp.max(s, axis=-1, keepdims=True), sink)
        p = jnp.exp(s - m)
        pb = p.astype(BF16)
        psum = _mm(pb, jnp.ones((2 * w, HEAD), BF16))
        o = _mm(pb, vcat) / (psum + jnp.exp(sink - m))
        for gi in range(group):
            outs.append(o[gi * w:(gi + 1) * w, :])
    o_ref[...] = jnp.concatenate(outs, axis=-1).astype(o_ref.dtype)


def _attention(q, k, v, sinks_b):
    b, s, qc = q.shape
    kc = k.shape[-1]
    nb = s // WINDOW
    cur = lambda bi, i: (bi, i, 0)
    prev = lambda bi, i: (bi, jnp.maximum(i - 1, 0), 0)
    return pl.pallas_call(
        _attn_kernel,
        grid=(b, nb),
        in_specs=[pl.BlockSpec((None, WINDOW, qc), cur),
                  pl.BlockSpec((None, WINDOW, kc), cur), pl.BlockSpec((None, WINDOW, kc), prev),
                  pl.BlockSpec((None, WINDOW, kc), cur), pl.BlockSpec((None, WINDOW, kc), prev),
                  _resident(sinks_b.shape)],
        out_specs=pl.BlockSpec((None, WINDOW, qc), cur),
        out_shape=jax.ShapeDtypeStruct((b, s, qc), BF16),
        compiler_params=_cparams(("parallel", "parallel")),
        name="swa_attention",
    )(q, k, k, v, v, sinks_b)


MERGE_SPLIT = 2


def _merge_kernel(n_exp, x_ref, yr_ref, ya_ref, gr_ref, ga_ref, wur_ref, wua_ref, wo_ref, n2_ref, wr_ref, br_ref,
                  x1_ref, h2_ref, gate_ref, idx_ref):
    tm = x_ref.shape[0] // MERGE_SPLIT
    parts = [slice(p * tm, (p + 1) * tm) for p in range(MERGE_SPLIT)]
    ups = [(_mm(yr_ref[rs, :], wur_ref[...]), _mm(ya_ref[rs, :], wua_ref[...])) for rs in parts]
    merged = [_sigmoid(gr_ref[rs, :].astype(F32)) * ur + _sigmoid(ga_ref[rs, :].astype(F32)) * ua
              for rs, (ur, ua) in zip(parts, ups)]
    x1s = [x_ref[rs, :] + _mm(m.astype(BF16), wo_ref[...]) for rs, m in zip(parts, merged)]
    h2s = []
    for rs, x1 in zip(parts, x1s):
        x1_ref[rs, :] = x1
        h2 = _rmsnorm(x1, n2_ref[...])
        half = h2.shape[1] // 2
        _store_row_tiles(h2_ref, rs.start, _pack_bf16_pair(h2[:, :half], h2[:, half:]))
        h2s.append(h2)
    all_logits = [jnp.dot(h2, wr_ref[...], preferred_element_type=F32, precision=HI) + br_ref[...] for h2 in h2s]
    lane_e = lax.broadcasted_iota(jnp.int32, (tm, n_exp), 1).astype(F32)
    lane_o = lax.broadcasted_iota(jnp.int32, (tm, 128), 1)
    for rs, logits in zip(parts, all_logits):
        vals, idxs = [], []
        for _ in range(TOP_K):
            mx = jnp.max(logits, axis=-1, keepdims=True)
            ix = jnp.min(jnp.where(logits == mx, lane_e, float(n_exp)), axis=-1, keepdims=True)
            vals.append(mx)
            idxs.append(ix)
            logits = jnp.where(lane_e == ix, -jnp.inf, logits)
        exps = [jnp.exp(vk - vals[0]) for vk in vals]
        tot = exps[0] + exps[1] + exps[2] + exps[3]
        gate_o, idx_o = jnp.zeros((tm, 128), F32), jnp.zeros((tm, 128), F32)
        for kk in range(TOP_K):
            gate_o = jnp.where(lane_o == kk, exps[kk] / tot, gate_o)
            idx_o = jnp.where(lane_o == kk, idxs[kk], idx_o)
        gate_ref[rs, :] = gate_o
        idx_ref[rs, :] = idx_o.astype(jnp.int32)


def _merge(x, y_rwkv, y_attn, gates, w_up_r, w_up_a, w_out, norm2_g, w_router, b_router, tm):
    n, d = x.shape
    c = y_rwkv.shape[1]
    n_exp = w_router.shape[1]
    row = lambda cols, j=0: pl.BlockSpec((tm, cols), lambda i: (i, j))
    return pl.pallas_call(
        functools.partial(_merge_kernel, n_exp),
        grid=(n // tm,),
        in_specs=[row(d), row(c), row(c), row(d, 0), row(d, 1),
                  _resident((c, d)), _resident((c, d)), _resident((d, d)), _resident((1, d)),
                  _resident((d, n_exp)), _resident((1, n_exp))],
        out_specs=[row(d), pl.BlockSpec((tm * ROW_TILES, 128), lambda i: (i, 0)), row(128), row(128)],
        out_shape=[jax.ShapeDtypeStruct((n, d), F32), jax.ShapeDtypeStruct((n * ROW_TILES, 128), jnp.uint32),
                   jax.ShapeDtypeStruct((n, 128), F32), jax.ShapeDtypeStruct((n, 128), jnp.int32)],
        compiler_params=_cparams(("parallel",)),
        name="merge_router",
    )(x, y_rwkv, y_attn, gates, gates, w_up_r, w_up_a, w_out, norm2_g, w_router, b_router)


GATHER_UNROLL = 8


def _start_row_gather(src_hbm, idx_ref, buf, sem):
    rows = buf.shape[0] // ROW_TILES

    def start(g, carry):
        for u in range(GATHER_UNROLL):
            r = g * GATHER_UNROLL + u
            dst = buf.at[pl.ds(pl.multiple_of(r * ROW_TILES, ROW_TILES), ROW_TILES)]
            pltpu.make_async_copy(src_hbm.at[idx_ref[0, 0, r]], dst, sem).start(priority=u % 2)
        return carry

    lax.fori_loop(0, rows // GATHER_UNROLL, start, 0)


def _load_row_chunk(buf, slot, row0, rows, chunk):
    return buf[slot, pl.ds(row0 * ROW_TILES + chunk, rows, stride=ROW_TILES), :]


def _wait_row_gather(buf, sem):
    pltpu.make_async_copy(buf, buf, sem).wait()


def _dispatch_kernel(brows_ref, tok_ref, tok_next_ref, h_hbm, o_ref, buf, sems):
    s = pl.program_id(0)
    n_sub = ROW_BLOCK // ROW_SUB

    def has_rows(t):
        return lax.rem(t, n_sub) * ROW_SUB < brows_ref[lax.div(t, n_sub)]

    slot = s % 2

    @pl.when((s == 0) & has_rows(0))
    def _():
        _start_row_gather(h_hbm, tok_ref, buf.at[0], sems.at[0])

    @pl.when((s + 1 < pl.num_programs(0)) & has_rows(jnp.minimum(s + 1, pl.num_programs(0) - 1)))
    def _():
        _start_row_gather(h_hbm, tok_next_ref, buf.at[1 - slot], sems.at[1 - slot])

    @pl.when(has_rows(s))
    def _():
        _wait_row_gather(buf.at[slot], sems.at[slot])
        half = o_ref.shape[1] // 2
        for j in range(ROW_TILES):
            lo, hi = _unpack_bf16_pair(_load_row_chunk(buf, slot, 0, ROW_SUB, j))
            o_ref[:, j * 128:(j + 1) * 128] = lo.astype(o_ref.dtype)
            o_ref[:, half + j * 128:half + (j + 1) * 128] = hi.astype(o_ref.dtype)

    @pl.when(jnp.logical_not(has_rows(s)))
    def _():
        o_ref[...] = jnp.zeros_like(o_ref)


def _dispatch(h2p, row_tok, block_rows, n_blocks):
    d = 2 * ROW_TILES * 128
    steps = n_blocks * (ROW_BLOCK // ROW_SUB)
    grid_spec = pltpu.PrefetchScalarGridSpec(
        num_scalar_prefetch=1,
        grid=(steps,),
        in_specs=[pl.BlockSpec((1, 1, ROW_SUB), lambda s, br: (s, 0, 0), memory_space=pltpu.SMEM),
                  pl.BlockSpec((1, 1, ROW_SUB), lambda s, br: (jnp.minimum(s + 1, steps - 1), 0, 0),
                               memory_space=pltpu.SMEM),
                  pl.BlockSpec(memory_space=pl.ANY)],
        out_specs=pl.BlockSpec((ROW_SUB, d), lambda s, br: (s, 0)),
        scratch_shapes=[pltpu.VMEM((2, ROW_SUB * ROW_TILES, 128), jnp.uint32), pltpu.SemaphoreType.DMA((2,))],
    )
    tok = row_tok.reshape(steps, 1, ROW_SUB)
    return pl.pallas_call(
        _dispatch_kernel,
        grid_spec=grid_spec,
        out_shape=jax.ShapeDtypeStruct((n_blocks * ROW_BLOCK, d), BF16),
        compiler_params=_cparams(("arbitrary",)),
        name="moe_dispatch",
    )(block_rows, tok, tok, h2p)


def _ffn_kernel(nvalid_ref, bexp_ref, brows_ref, xs_ref, w1g_ref, w1l_ref, b1g_ref, b1l_ref, w2a_ref, w2b_ref,
                b2a_ref, b2b_ref, o_ref, act_ref):
    i = pl.program_id(0)
    j = pl.program_id(1)
    n_ff = act_ref.shape[0]
    rows_valid = brows_ref[i]
    n_sub = ROW_BLOCK // ROW_SUB

    full = rows_valid == ROW_BLOCK
    whole = slice(0, ROW_BLOCK)
    subs = [slice(sb * ROW_SUB, (sb + 1) * ROW_SUB) for sb in range(n_sub)]

    def hidden(rs, w1g, w1l):
        x = xs_ref[rs, :]
        hg = _mm(x, w1g) + b1g_ref[...]
        hl = _mm(x, w1l) + b1l_ref[...]
        glu = jnp.minimum(hg, SWIGLU_LIMIT)
        lin = jnp.clip(hl, -SWIGLU_LIMIT, SWIGLU_LIMIT)
        act_ref[j, rs, :] = (glu * _sigmoid(SWIGLU_ALPHA * glu) * (lin + 1.0)).astype(BF16)

    def project(rs, w2a, w2b):
        def half(w2, b2_ref):
            acc = _mm(act_ref[0, rs, :], w2[0:FF_TILE])
            for jf in range(1, n_ff):
                acc = acc + _mm(act_ref[jf, rs, :], w2[jf * FF_TILE:(jf + 1) * FF_TILE])
            return acc + b2_ref[...]

        write_out(rs, _pack_bf16_pair(half(w2a, b2a_ref), half(w2b, b2b_ref)))

    def write_out(rs, packed):
        for cc in range(OUT_TILE // 128):
            _store_row_chunk(o_ref, rs.start, (j - n_ff) * (OUT_TILE // 128) + cc, packed[:, cc * 128:(cc + 1) * 128])

    @pl.when((j < n_ff) & full)
    def _():
        hidden(whole, w1g_ref[...].astype(BF16), w1l_ref[...].astype(BF16))

    @pl.when((j < n_ff) & jnp.logical_not(full))
    def _():
        w1g = w1g_ref[...].astype(BF16)
        w1l = w1l_ref[...].astype(BF16)
        for sb, rs in enumerate(subs):
            pl.when(sb * ROW_SUB < rows_valid)(functools.partial(hidden, rs, w1g, w1l))

    @pl.when((j >= n_ff) & full)
    def _():
        project(whole, w2a_ref[...].astype(BF16), w2b_ref[...].astype(BF16))

    @pl.when((j >= n_ff) & jnp.logical_not(full))
    def _():
        w2a = w2a_ref[...].astype(BF16)
        w2b = w2b_ref[...].astype(BF16)
        for sb, rs in enumerate(subs):
            pl.when(sb * ROW_SUB < rows_valid)(functools.partial(project, rs, w2a, w2b))

            @pl.when(sb * ROW_SUB >= rows_valid)
            def _():
                write_out(rs, jnp.zeros((ROW_SUB, OUT_TILE), o_ref.dtype))


def _expert_ffn(xs, w1, b1, w2, b2, block_expert, block_rows, n_valid, n_blocks):
    d = xs.shape[1]
    assert d // 2 == ROW_TILES * 128
    n_exp, _, ff2 = w1.shape
    ff = ff2 // 2
    n_ff = ff // FF_TILE
    n_out = d // 2 // OUT_TILE

    def hid(i, j, nv):
        return jnp.where(i < nv[0], jnp.minimum(j, n_ff - 1), n_ff - 1)

    def w2_expert(i, j, be):
        return jnp.where(j < n_ff, be[jnp.maximum(i - 1, 0)], be[i])

    def w2_tile(i, j, nv):
        return jnp.where((i < nv[0]) & (j >= n_ff), j - n_ff, n_out - 1)

    grid_spec = pltpu.PrefetchScalarGridSpec(
        num_scalar_prefetch=3,
        grid=(n_blocks, n_ff + n_out),
        in_specs=[pl.BlockSpec((ROW_BLOCK, d), lambda i, j, nv, be, br: (i, 0)),
                  pl.BlockSpec((None, d, FF_TILE), lambda i, j, nv, be, br: (be[i], 0, hid(i, j, nv))),
                  pl.BlockSpec((None, d, FF_TILE), lambda i, j, nv, be, br: (be[i], 0, n_ff + hid(i, j, nv))),
                  pl.BlockSpec((None, 1, FF_TILE), lambda i, j, nv, be, br: (be[i], 0, hid(i, j, nv))),
                  pl.BlockSpec((None, 1, FF_TILE), lambda i, j, nv, be, br: (be[i], 0, n_ff + hid(i, j, nv))),
                  pl.BlockSpec((None, ff, OUT_TILE),
                               lambda i, j, nv, be, br: (w2_expert(i, j, be), 0, w2_tile(i, j, nv))),
                  pl.BlockSpec((None, ff, OUT_TILE),
                               lambda i, j, nv, be, br: (w2_expert(i, j, be), 0, n_out + w2_tile(i, j, nv))),
                  pl.BlockSpec((None, 1, OUT_TILE),
                               lambda i, j, nv, be, br: (w2_expert(i, j, be), 0, w2_tile(i, j, nv))),
                  pl.BlockSpec((None, 1, OUT_TILE),
                               lambda i, j, nv, be, br: (w2_expert(i, j, be), 0, n_out + w2_tile(i, j, nv)))],
        out_specs=pl.BlockSpec((ROW_BLOCK * ROW_TILES, 128), lambda i, j, nv, be, br: (i, 0)),
        scratch_shapes=[pltpu.VMEM((n_ff, ROW_BLOCK, FF_TILE), BF16)],
    )
    b1r = b1.reshape(n_exp, 1, ff2)
    b2r = b2.reshape(n_exp, 1, d)
    return pl.pallas_call(
        _ffn_kernel,
        grid_spec=grid_spec,
        out_shape=jax.ShapeDtypeStruct((n_blocks * ROW_BLOCK * ROW_TILES, 128), jnp.uint32),
        compiler_params=_cparams(("arbitrary", "arbitrary")),
        name="moe_ffn",
    )(n_valid, block_expert, block_rows, xs, w1, w1, b1r, b1r, w2, w2, b2r, b2r)


def _combine_kernel(pos_ref, pos_next_ref, x1_ref, gate_ref, nf_ref, ys_hbm, o_ref, buf, sems):
    s = pl.program_id(0)
    tc = x1_ref.shape[0]
    slot = s % 2

    @pl.when(s == 0)
    def _():
        _start_row_gather(ys_hbm, pos_ref, buf.at[0], sems.at[0])

    @pl.when(s + 1 < pl.num_programs(0))
    def _():
        _start_row_gather(ys_hbm, pos_next_ref, buf.at[1 - slot], sems.at[1 - slot])

    _wait_row_gather(buf.at[slot], sems.at[slot])
    half = x1_ref.shape[1] // 2
    gate = gate_ref[...]
    gates = [gate[:, k:k + 1] for k in range(TOP_K)]
    accs = []
    ssq = jnp.zeros((tc, 1), F32)
    for j in range(ROW_TILES):
        lo_cols = slice(j * 128, (j + 1) * 128)
        hi_cols = slice(half + j * 128, half + (j + 1) * 128)
        acc_lo = x1_ref[:, lo_cols]
        acc_hi = x1_ref[:, hi_cols]
        for k in range(TOP_K):
            lo, hi = _unpack_bf16_pair(_load_row_chunk(buf, slot, k * tc, tc, j))
            acc_lo = acc_lo + gates[k] * lo
            acc_hi = acc_hi + gates[k] * hi
        accs += [(lo_cols, acc_lo), (hi_cols, acc_hi)]
        ssq = ssq + jnp.sum(acc_lo * acc_lo, axis=-1, keepdims=True) + jnp.sum(acc_hi * acc_hi, axis=-1, keepdims=True)
    scale = lax.rsqrt(ssq * (1.0 / x1_ref.shape[1]) + NORM_EPS)
    for cols, acc in accs:
        o_ref[:, cols] = acc * scale * nf_ref[:, cols]


def _combine(x1, gate, pos, ys, normf_g, tc):
    n, d = x1.shape
    nt = n // tc
    return pl.pallas_call(
        _combine_kernel,
        grid=(nt,),
        in_specs=[pl.BlockSpec((1, 1, TOP_K * tc), lambda i: (i, 0, 0), memory_space=pltpu.SMEM),
                  pl.BlockSpec((1, 1, TOP_K * tc), lambda i: (jnp.minimum(i + 1, nt - 1), 0, 0),
                               memory_space=pltpu.SMEM),
                  pl.BlockSpec((tc, d), lambda i: (i, 0)), pl.BlockSpec((tc, 128), lambda i: (i, 0)),
                  _resident((1, d)), pl.BlockSpec(memory_space=pl.ANY)],
        out_specs=pl.BlockSpec((tc, d), lambda i: (i, 0)),
        out_shape=jax.ShapeDtypeStruct((n, d), F32),
        scratch_shapes=[pltpu.VMEM((2, TOP_K * tc * ROW_TILES, 128), jnp.uint32), pltpu.SemaphoreType.DMA((2,))],
        compiler_params=_cparams(("arbitrary",)),
        name="moe_combine",
    )(pos, pos, x1, gate, normf_g, ys)


def _route(top_idx, n_exp, tc):
    n = top_idx.shape[0]
    n_assign = n * TOP_K
    n_blocks = n_assign // ROW_BLOCK + n_exp
    tile = min(1024, n)
    hot = (top_idx[:, :, None] == jnp.arange(n_exp, dtype=jnp.int32)).any(axis=1).reshape(n // tile, tile, n_exp)
    earlier = jnp.asarray(np.tril(np.ones((tile, tile), np.float32), -1), BF16)
    within = jnp.einsum("ts,bse->bte", earlier, hot.astype(BF16), preferred_element_type=F32)
    tile_tot = jnp.sum(hot, axis=1, dtype=jnp.int32)
    tile_off = jnp.cumsum(tile_tot, axis=0) - tile_tot
    rank = (within.astype(jnp.int32) + tile_off[:, None, :]).reshape(n, n_exp)
    counts = jnp.sum(tile_tot, axis=0)
    padded = (counts + ROW_BLOCK - 1) // ROW_BLOCK * ROW_BLOCK
    pad_end = jnp.cumsum(padded)
    pad_start = pad_end - padded
    grp_start = jnp.cumsum(counts) - counts
    slot = jnp.take_along_axis(rank + pad_start[None, :], top_idx, axis=1).reshape(-1)
    block_start = jnp.arange(n_blocks, dtype=jnp.int32) * ROW_BLOCK
    block_expert = jnp.minimum(jnp.sum(pad_end[None, :] <= block_start[:, None], axis=1), n_exp - 1).astype(jnp.int32)
    n_valid = (pad_end[-1] // ROW_BLOCK).astype(jnp.int32).reshape(1)
    order = jnp.argsort(top_idx.reshape(-1), stable=True)
    row_e = jnp.repeat(block_expert, ROW_BLOCK)
    row_r = jnp.arange(n_blocks * ROW_BLOCK, dtype=jnp.int32) - pad_start[row_e]
    src = jnp.clip(grp_start[row_e] + row_r, 0, n_assign - 1)
    row_tok = jnp.where((row_r >= 0) & (row_r < counts[row_e]), order[src] // TOP_K, 0).astype(jnp.int32)
    pos = slot.reshape(n // tc, tc, TOP_K).transpose(0, 2, 1).reshape(n // tc, 1, TOP_K * tc)
    block_rows = jnp.clip(counts[block_expert] - (block_start - pad_start[block_expert]), 0, ROW_BLOCK)
    block_rows = jnp.where(jnp.arange(n_blocks) < n_valid[0], block_rows, 0).astype(jnp.int32)
    return row_tok, block_expert, block_rows, n_valid, pos, n_blocks


def _layer(x, norm1_g, w_in, mix_mu, w0, w_decay_up, a0, w_iclr_up, w_gate_up, k_k, k_a, r_k, lnx_g, lnx_b, b_qkv,
           sinks, w_up_rwkv, w_up_attn, w_out, norm2_g, w_router, b_router, w1, b1, w2, b2, normf_g):
    b, s, d = x.shape
    n = b * s
    c = w_up_rwkv.shape[0]
    lora = w_decay_up.shape[0]
    qc = w_up_attn.shape[0]
    kvc = KV_HEADS * HEAD
    rwkv_cols = 3 * c + 2 * lora + w_gate_up.shape[0]
    qkv_cols = qc + 2 * kvc
    row = lambda t: t.reshape(1, -1).astype(F32)
    xf = x.reshape(n, d)
    g1 = row(norm1_g)
    w_in_b = w_in.astype(BF16)

    qkv = _norm_proj(xf, g1, w_in_b[:, rwkv_cols:rwkv_cols + qkv_cols], row(b_qkv), BF16, 512)
    gates = _norm_proj(xf, g1, w_in_b[:, rwkv_cols + qkv_cols:], jnp.zeros((1, 2 * d), F32), BF16, 512)

    zl = jnp.zeros((lora, c), F32)
    w_lora = jnp.concatenate([jnp.concatenate([w_decay_up, zl], axis=1), jnp.concatenate([zl, w_iclr_up], axis=1)],
                             axis=0)
    r, lw, kf, v, kn, ba, g = _rwkv_prep(xf, g1, w_in_b[:, :rwkv_cols], s, c, row(mix_mu), w_lora, w_gate_up, row(w0),
                                         row(a0), row(k_k), row(k_a), 256)
    as3 = lambda t: t.reshape(b, s, c)
    y_rwkv = _rwkv_scan(as3(r), as3(lw), as3(kf), as3(v), as3(kn), as3(ba), as3(g), row(r_k), row(lnx_g),
                        row(lnx_b)).reshape(n, c)

    q = qkv[:, :qc].reshape(b, s, qc)
    ka = qkv[:, qc:qc + kvc].reshape(b, s, kvc)
    va = qkv[:, qc + kvc:].reshape(b, s, kvc)
    sinks_b = jnp.broadcast_to(sinks.astype(F32).reshape(-1, 1), (sinks.shape[0], 128))
    y_attn = _attention(q, ka, va, sinks_b).reshape(n, qc)

    x1, h2, gate, top_idx = _merge(xf, y_rwkv, y_attn, gates, w_up_rwkv.astype(BF16), w_up_attn.astype(BF16),
                                   w_out.astype(BF16), row(norm2_g), w_router, row(b_router), 256)

    tc = 256
    row_tok, block_expert, block_rows, n_valid, pos, n_blocks = _route(top_idx[:, :TOP_K], w_router.shape[1], tc)
    xs = _dispatch(h2.reshape(n, ROW_TILES, 128), row_tok, block_rows, n_blocks)
    ys = _expert_ffn(xs, w1, b1, w2, b2, block_expert, block_rows, n_valid, n_blocks)
    out = _combine(x1, gate, pos, ys.reshape(-1, ROW_TILES, 128), row(normf_g), tc)
    return out.reshape(b, s, d)


def kernel(x, norm1_g, w_in, mix_mu, w0, w_decay_up, a0, w_iclr_up, w_gate_up, k_k, k_a, r_k, lnx_g, lnx_b, b_qkv,
           sinks, w_up_rwkv, w_up_attn, w_out, norm2_g, w_router, b_router, w1, b1, w2, b2, normf_g):
    assert w_in.shape[0] == 1, "single-layer block"
    return _layer(x, norm1_g[0], w_in[0], mix_mu[0], w0[0], w_decay_up[0], a0[0], w_iclr_up[0], w_gate_up[0],
                  k_k[0], k_a[0], r_k[0], lnx_g[0], lnx_b[0], b_qkv[0], sinks[0], w_up_rwkv[0], w_up_attn[0],
                  w_out[0], norm2_g[0], w_router[0], b_router[0], w1[0], b1[0], w2[0], b2[0], normf_g)
```

```python
import functools

import jax
import jax.numpy as jnp
import numpy as np
from jax import lax
from jax.experimental import pallas as pl
from jax.experimental.pallas import tpu as pltpu

F32 = jnp.float32
BF16 = jnp.bfloat16

NORM_EPS = 1e-5
LNX_EPS = 64e-5
HEAD = 64
WINDOW = 128
KV_HEADS = 4
TOP_K = 4
SWIGLU_LIMIT = 7.0
SWIGLU_ALPHA = 1.702
DECAY_SCALE = float(np.exp(-0.5))

CHUNK = 64
ROW_BLOCK = 1152
ROW_SUB = 384
FF_TILE = 512
OUT_TILE = 256
VMEM_LIMIT = 56 * 1024 * 1024


def _cparams(sem):
    return pltpu.CompilerParams(dimension_semantics=sem, vmem_limit_bytes=VMEM_LIMIT)


def _resident(shape):
    nd = len(shape)
    return pl.BlockSpec(shape, lambda *_: (0,) * nd, pipeline_mode=pl.Buffered(1))


def _rmsnorm(x, g):
    return x * lax.rsqrt(jnp.mean(x * x, axis=-1, keepdims=True) + NORM_EPS) * g


def _sigmoid(x):
    return 1.0 / (1.0 + jnp.exp(-x))


def _pack_bf16_pair(lo, hi):
    lo_bits = lax.bitcast_convert_type(lo.astype(BF16).astype(F32), jnp.uint32)
    hi_bits = lax.bitcast_convert_type(hi.astype(BF16).astype(F32), jnp.uint32)
    return (hi_bits & jnp.uint32(0xFFFF0000)) | (lo_bits >> 16)


ROW_TILES = 8


def _store_row_chunk(ref, row0, chunk, value):
    ref[pl.ds(row0 * ROW_TILES + chunk, value.shape[0], stride=ROW_TILES), :] = value


def _store_row_tiles(ref, row0, packed):
    for j in range(ROW_TILES):
        _store_row_chunk(ref, row0, j, packed[:, j * 128:(j + 1) * 128])


def _unpack_bf16_pair(packed):
    lo = lax.bitcast_convert_type(packed << 16, F32)
    hi = lax.bitcast_convert_type(packed & jnp.uint32(0xFFFF0000), F32)
    return lo, hi


def _norm_proj_kernel(x_ref, g_ref, w_ref, b_ref, o_ref):
    h = _rmsnorm(x_ref[...], g_ref[...]).astype(BF16)
    z = jnp.dot(h, w_ref[...], preferred_element_type=F32) + b_ref[...]
    o_ref[...] = z.astype(o_ref.dtype)


def _norm_proj(x, g, w, b, out_dtype, tm):
    n, d = x.shape
    cols = w.shape[1]
    return pl.pallas_call(
        _norm_proj_kernel,
        grid=(n // tm,),
        in_specs=[pl.BlockSpec((tm, d), lambda i: (i, 0)), _resident((1, d)), _resident((d, cols)),
                  _resident((1, cols))],
        out_specs=pl.BlockSpec((tm, cols), lambda i: (i, 0)),
        out_shape=jax.ShapeDtypeStruct((n, cols), out_dtype),
        compiler_params=_cparams(("parallel",)),
        name="norm_proj",
    )(x, g, w, b)


def _head_sums(x):
    rows, c = x.shape
    lane = lax.broadcasted_iota(jnp.int32, (rows, 128), 1)
    low = lane < HEAD
    parts = []
    for gi in range(c // 128):
        xg = x[:, gi * 128:(gi + 1) * 128]
        s_lo = jnp.sum(jnp.where(low, xg, 0.0), axis=-1, keepdims=True)
        s_hi = jnp.sum(jnp.where(low, 0.0, xg), axis=-1, keepdims=True)
        parts.append(jnp.where(low, s_lo, s_hi))
    return jnp.concatenate(parts, axis=-1)


def _dot_bf16x3(a, w):
    a_hi = a.astype(BF16)
    a_lo = (a - a_hi.astype(F32)).astype(BF16)
    w_hi = w.astype(BF16)
    w_lo = (w - w_hi.astype(F32)).astype(BF16)
    return (jnp.dot(a_hi, w_hi, preferred_element_type=F32) + jnp.dot(a_lo, w_hi, preferred_element_type=F32)
            + jnp.dot(a_hi, w_lo, preferred_element_type=F32))


def _prep_kernel(seq_blocks, c, x_ref, n1_ref, win_ref, mu_ref, wlora_ref, wgate_ref, w0_ref, a0_ref, kk_ref, ka_ref,
                 r_ref, lw_ref, kf_ref, v_ref, kn_ref, ba_ref, g_ref, last_ref):
    i = pl.program_id(0)

    @pl.when(i == 0)
    def _():
        last_ref[...] = jnp.zeros_like(last_ref)

    h = _rmsnorm(x_ref[...], n1_ref[...]).astype(BF16)
    z = jnp.dot(h, win_ref[...], preferred_element_type=F32)
    tm = z.shape[0]
    prev = jnp.where(i % seq_blocks == 0, 0.0, last_ref[0:1, :])
    last_ref[0:1, :] = z[tm - 1:tm, :]
    row = lax.broadcasted_iota(jnp.int32, z.shape, 0)
    shifted = jnp.where(row == 0, prev, pltpu.roll(z, 1, 0))
    zs = z + (shifted - z) * mu_ref[...]
    r = zs[:, 0:c]
    k = zs[:, c:2 * c]
    v = zs[:, 2 * c:3 * c]
    zwa = zs[:, 3 * c:3 * c + 128]
    zg = zs[:, 3 * c + 128:3 * c + 256]
    lane = lax.broadcasted_iota(jnp.int32, (tm, 128), 1)
    lora_in = jnp.where(lane < 64, jnp.tanh(zwa), zwa)
    up = _dot_bf16x3(lora_in, wlora_ref[...])
    u = w0_ref[...] + up[:, 0:c]
    a = _sigmoid(a0_ref[...] + up[:, c:2 * c])
    g = _dot_bf16x3(_sigmoid(zg), wgate_ref[...])
    lw = -DECAY_SCALE * _sigmoid(u)
    kk = k * kk_ref[...]
    kn = kk / jnp.maximum(jnp.sqrt(_head_sums(kk * kk)), 1e-12)
    kf = k * (1.0 + (a - 1.0) * ka_ref[...])
    r_ref[...] = r
    lw_ref[...] = lw
    kf_ref[...] = kf
    v_ref[...] = v
    kn_ref[...] = kn
    ba_ref[...] = kn * a
    g_ref[...] = g


def _rwkv_prep(x, norm_g, w_in_rwkv, seq, c, mix_mu, w_lora, w_gate, w0, a0, k_k, k_a, tm):
    n, d = x.shape
    zc = w_in_rwkv.shape[1]
    row_spec = pl.BlockSpec((tm, c), lambda i: (i, 0))
    out = jax.ShapeDtypeStruct((n, c), F32)
    return pl.pallas_call(
        functools.partial(_prep_kernel, seq // tm, c),
        grid=(n // tm,),
        in_specs=[pl.BlockSpec((tm, d), lambda i: (i, 0)), _resident((1, d)), _resident((d, zc)),
                  _resident((1, zc)), _resident((128, 2 * c)), _resident((128, c)),
                  _resident((1, c)), _resident((1, c)), _resident((1, c)), _resident((1, c))],
        out_specs=[row_spec] * 7,
        out_shape=[out] * 7,
        scratch_shapes=[pltpu.VMEM((8, zc), F32)],
        compiler_params=_cparams(("arbitrary",)),
        name="rwkv_prep",
    )(x, norm_g, w_in_rwkv, mix_mu, w_lora, w_gate, w0, a0, k_k, k_a)


def _split3(x):
    h1 = x.astype(BF16)
    r1 = x - h1.astype(F32)
    h2 = r1.astype(BF16)
    h3 = (r1 - h2.astype(F32)).astype(BF16)
    return h1, h2, h3


def _mm(a, b):
    return jnp.dot(a, b, preferred_element_type=F32)


def _mm_nt(a, b):
    return lax.dot_general(a, b, (((1,), (1,)), ((), ())), preferred_element_type=F32)


def _mm_tn(a, b):
    return lax.dot_general(a, b, (((0,), (0,)), ((), ())), preferred_element_type=F32)


GROUP_HEADS = 4
GROUP_LANES = GROUP_HEADS * HEAD


def _block_diag(x, mask):
    return jnp.concatenate([x.astype(BF16)] * GROUP_HEADS, axis=0) * mask


def _scan_kernel(r_ref, lw_ref, kf_ref, v_ref, kn_ref, ba_ref, g_ref, rk_ref, lng_ref, lnb_ref, mask_ref, y_ref,
                 s_ref):
    @pl.when(pl.program_id(0) == 0)
    def _():
        s_ref[...] = jnp.zeros_like(s_ref)

    t = CHUNK
    nb = r_ref.shape[0]
    c = r_ref.shape[2]
    gl = GROUP_LANES
    n_groups = c // gl
    mask = mask_ref[...]
    ti = lax.broadcasted_iota(jnp.int32, (t, t), 0)
    si = lax.broadcasted_iota(jnp.int32, (t, t), 1)
    tri = jnp.where(si <= ti, 1.0, 0.0).astype(BF16)
    row = lax.broadcasted_iota(jnp.int32, (t, gl), 0)
    col = lax.broadcasted_iota(jnp.int32, (t, gl), 1) & (HEAD - 1)
    strict = col < row
    incl = col <= row
    eye = jnp.where(col == row, 1.0, 0.0)

    prep = []
    for b in range(nb):
        lw = lw_ref[b]
        cum = sum(jnp.dot(tri, part, preferred_element_type=F32) for part in _split3(lw))
        cum_end = cum[t - 1:t, :]
        e_pos = jnp.exp(cum)
        e_neg = jnp.exp(-cum)
        e_end = jnp.exp(cum_end - cum)
        r = r_ref[b]
        kf = kf_ref[b]
        ba = ba_ref[b]
        prep.append(dict(
            rt=r * e_pos,
            at=-kn_ref[b] * jnp.exp(cum - lw),
            bt=ba * e_neg,
            kt=kf * e_neg,
            bh=ba * e_end,
            kh=kf * e_end,
            v=v_ref[b],
            w_end=jnp.exp(cum_end),
            rkf=r * kf * rk_ref[...]))

    probs = [(b, gi) for b in range(nb) for gi in range(n_groups)]

    def part(b, gi, name):
        return prep[b][name][:, gi * gl:(gi + 1) * gl]

    a_ab, a_ak, a_rb, a_rk = [], [], [], []
    for b, gi in probs:
        lhs = jnp.concatenate([part(b, gi, "at"), part(b, gi, "rt")], axis=0).astype(BF16)
        rhs = jnp.concatenate([_block_diag(part(b, gi, "bt"), mask), _block_diag(part(b, gi, "kt"), mask)], axis=0)
        amat = _mm_nt(lhs, rhs)
        a_ab.append(jnp.where(strict, amat[0:t, 0:gl], 0.0))
        a_ak.append(jnp.where(strict, amat[0:t, gl:2 * gl], 0.0))
        a_rb.append(jnp.where(incl, amat[t:2 * t, 0:gl], 0.0))
        a_rk.append(jnp.where(incl, amat[t:2 * t, gl:2 * gl], 0.0))

    vbd = [_block_diag(part(b, gi, "v"), mask) for b, gi in probs]
    av = [_mm(a_ak[i].astype(BF16), vbd[i]) for i in range(len(probs))]
    minv = [eye + a for a in a_ab]
    power = [_mm(a.astype(BF16), _block_diag(a, mask)) for a in a_ab]
    span = 2
    while span < t:
        last = span * 2 >= t
        for i in range(len(probs)):
            pbd = _block_diag(power[i], mask)
            if last:
                minv[i] = minv[i] + _mm(minv[i].astype(BF16), pbd)
            else:
                both = _mm(jnp.concatenate([power[i], minv[i]], axis=0).astype(BF16), pbd)
                power[i] = both[0:t]
                minv[i] = minv[i] + both[t:2 * t]
        span *= 2

    s0 = [s_ref[b, gi] for b, gi in probs]
    sbd = [_block_diag(s, mask) for s in s0]
    ps = []
    for i, (b, gi) in enumerate(probs):
        lhs = jnp.concatenate([part(b, gi, "at"), part(b, gi, "rt")], axis=0).astype(BF16)
        ps.append(_mm_nt(lhs, sbd[i]))
    u = [_mm(minv[i].astype(BF16), _block_diag(ps[i][0:t] + av[i], mask)) for i in range(len(probs))]
    ys = []
    for i, (b, gi) in enumerate(probs):
        lhs = jnp.concatenate([a_rb[i], a_rk[i]], axis=1).astype(BF16)
        rhs = jnp.concatenate([_block_diag(u[i], mask), vbd[i]], axis=0)
        ys.append(ps[i][t:2 * t] + _mm(lhs, rhs))
        uv = jnp.concatenate([u[i], part(b, gi, "v")], axis=0).astype(BF16)
        bk = jnp.concatenate([part(b, gi, "bh"), part(b, gi, "kh")], axis=0).astype(BF16)
        full = _mm_tn(uv, bk) * mask.astype(F32)
        upd = full[0:HEAD]
        for hh in range(1, GROUP_HEADS):
            upd = upd + full[hh * HEAD:(hh + 1) * HEAD]
        s_ref[b, gi] = s0[i] * part(b, gi, "w_end") + upd

    for b in range(nb):
        y = jnp.concatenate([ys[b * n_groups + gi] for gi in range(n_groups)], axis=1)
        mu = _head_sums(y) * (1.0 / HEAD)
        yc = y - mu
        var = _head_sums(yc * yc) * (1.0 / HEAD)
        yn = yc * lax.rsqrt(var + LNX_EPS)
        bonus = _head_sums(prep[b]["rkf"]) * prep[b]["v"]
        y_ref[b] = ((yn * lng_ref[...] + lnb_ref[...] + bonus) * g_ref[b]).astype(y_ref.dtype)


def _rwkv_scan(r, lw, kf, v, kn, ba, g, r_k, lnx_g, lnx_b):
    b, s, c = r.shape
    blk = pl.BlockSpec((b, CHUNK, c), lambda ci: (0, ci, 0))
    hid = np.arange(GROUP_LANES) // HEAD
    mask = jnp.asarray(hid[:, None] == hid[None, :], BF16)
    return pl.pallas_call(
        _scan_kernel,
        grid=(s // CHUNK,),
        in_specs=[blk] * 7 + [_resident((1, c))] * 3 + [_resident((GROUP_LANES, GROUP_LANES))],
        out_specs=blk,
        out_shape=jax.ShapeDtypeStruct((b, s, c), BF16),
        scratch_shapes=[pltpu.VMEM((b, c // GROUP_LANES, HEAD, GROUP_LANES), F32)],
        compiler_params=_cparams(("arbitrary",)),
        name="rwkv_scan",
    )(r, lw, kf, v, kn, ba, g, r_k, lnx_g, lnx_b, mask)


def _attn_kernel(q_ref, kc_ref, kp_ref, vc_ref, vp_ref, sink_ref, o_ref):
    first = pl.program_id(1) == 0
    w = WINDOW
    group = q_ref.shape[-1] // HEAD // KV_HEADS
    row = lax.broadcasted_iota(jnp.int32, (group * w, 2 * w), 0)
    qi = row & (w - 1)
    kj = lax.broadcasted_iota(jnp.int32, (group * w, 2 * w), 1)
    lo = jnp.where(first, jnp.maximum(qi, w - 1), qi)
    valid = (kj > lo) & (kj <= qi + w)
    grow = lax.broadcasted_iota(jnp.int32, (group * w, 1), 0) // w
    q = q_ref[...]
    outs = []
    for hk in range(KV_HEADS):
        ksl = slice(hk * HEAD, (hk + 1) * HEAD)
        kcat = jnp.concatenate([kp_ref[:, ksl], kc_ref[:, ksl]], axis=0)
        vcat = jnp.concatenate([vp_ref[:, ksl], vc_ref[:, ksl]], axis=0)
        qg = jnp.concatenate([q[:, (hk * group + gi) * HEAD:(hk * group + gi + 1) * HEAD] for gi in range(group)],
                             axis=0)
        sink = jnp.zeros((group * w, 1), F32)
        for gi in range(group):
            sink = jnp.where(grow == gi, sink_ref[hk * group + gi:hk * group + gi + 1, 0:1], sink)
        s = _mm_nt(qg, kcat) * (HEAD ** -0.5)
        s = jnp.where(valid, s, -1e30)
        m = jnp.maximum(jnp.max(s, axis=-1, keepdims=True), sink)
        p = jnp.exp(s - m)
        pb = p.astype(BF16)
        psum = _mm(pb, jnp.ones((2 * w, HEAD), BF16))
        o = _mm(pb, vcat) / (psum + jnp.exp(sink - m))
        for gi in range(group):
            outs.append(o[gi * w:(gi + 1) * w, :])
    o_ref[...] = jnp.concatenate(outs, axis=-1).astype(o_ref.dtype)


def _attention(q, k, v, sinks_b):
    b, s, qc = q.shape
    kc = k.shape[-1]
    nb = s // WINDOW
    cur = lambda bi, i: (bi, i, 0)
    prev = lambda bi, i: (bi, jnp.maximum(i - 1, 0), 0)
    return pl.pallas_call(
        _attn_kernel,
        grid=(b, nb),
        in_specs=[pl.BlockSpec((None, WINDOW, qc), cur),
                  pl.BlockSpec((None, WINDOW, kc), cur), pl.BlockSpec((None, WINDOW, kc), prev),
                  pl.BlockSpec((None, WINDOW, kc), cur), pl.BlockSpec((None, WINDOW, kc), prev),
                  _resident(sinks_b.shape)],
        out_specs=pl.BlockSpec((None, WINDOW, qc), cur),
        out_shape=jax.ShapeDtypeStruct((b, s, qc), BF16),
        compiler_params=_cparams(("parallel", "parallel")),
        name="swa_attention",
    )(q, k, k, v, v, sinks_b)


MERGE_SPLIT = 2


def _merge_kernel(n_exp, x_ref, yr_ref, ya_ref, gr_ref, ga_ref, wur_ref, wua_ref, wo_ref, n2_ref, wr_ref, br_ref,
                  x1_ref, h2_ref, gate_ref, idx_ref):
    tm = x_ref.shape[0] // MERGE_SPLIT
    parts = [slice(p * tm, (p + 1) * tm) for p in range(MERGE_SPLIT)]
    ups = [(_mm(yr_ref[rs, :], wur_ref[...]), _mm(ya_ref[rs, :], wua_ref[...])) for rs in parts]
    merged = [_sigmoid(gr_ref[rs, :].astype(F32)) * ur + _sigmoid(ga_ref[rs, :].astype(F32)) * ua
              for rs, (ur, ua) in zip(parts, ups)]
    x1s = [x_ref[rs, :] + _mm(m.astype(BF16), wo_ref[...]) for rs, m in zip(parts, merged)]
    h2s = []
    for rs, x1 in zip(parts, x1s):
        x1_ref[rs, :] = x1
        h2 = _rmsnorm(x1, n2_ref[...])
        half = h2.shape[1] // 2
        _store_row_tiles(h2_ref, rs.start, _pack_bf16_pair(h2[:, :half], h2[:, half:]))
        h2s.append(h2)
    all_logits = [_dot_bf16x3(h2, wr_ref[...]) + br_ref[...] for h2 in h2s]
    lane_e = lax.broadcasted_iota(jnp.int32, (tm, n_exp), 1).astype(F32)
    lane_o = lax.broadcasted_iota(jnp.int32, (tm, 128), 1)
    for rs, logits in zip(parts, all_logits):
        vals, idxs = [], []
        for _ in range(TOP_K):
            mx = jnp.max(logits, axis=-1, keepdims=True)
            ix = jnp.min(jnp.where(logits == mx, lane_e, float(n_exp)), axis=-1, keepdims=True)
            vals.append(mx)
            idxs.append(ix)
            logits = jnp.where(lane_e == ix, -jnp.inf, logits)
        exps = [jnp.exp(vk - vals[0]) for vk in vals]
        tot = exps[0] + exps[1] + exps[2] + exps[3]
        gate_o, idx_o = jnp.zeros((tm, 128), F32), jnp.zeros((tm, 128), F32)
        for kk in range(TOP_K):
            gate_o = jnp.where(lane_o == kk, exps[kk] / tot, gate_o)
            idx_o = jnp.where(lane_o == kk, idxs[kk], idx_o)
        gate_ref[rs, :] = gate_o
        idx_ref[rs, :] = idx_o.astype(jnp.int32)


def _merge(x, y_rwkv, y_attn, gates, w_up_r, w_up_a, w_out, norm2_g, w_router, b_router, tm):
    n, d = x.shape
    c = y_rwkv.shape[1]
    n_exp = w_router.shape[1]
    row = lambda cols, j=0: pl.BlockSpec((tm, cols), lambda i: (i, j))
    return pl.pallas_call(
        functools.partial(_merge_kernel, n_exp),
        grid=(n // tm,),
        in_specs=[row(d), row(c), row(c), row(d, 0), row(d, 1),
                  _resident((c, d)), _resident((c, d)), _resident((d, d)), _resident((1, d)),
                  _resident((d, n_exp)), _resident((1, n_exp))],
        out_specs=[row(d), pl.BlockSpec((tm * ROW_TILES, 128), lambda i: (i, 0)), row(128), row(128)],
        out_shape=[jax.ShapeDtypeStruct((n, d), F32), jax.ShapeDtypeStruct((n * ROW_TILES, 128), jnp.uint32),
                   jax.ShapeDtypeStruct((n, 128), F32), jax.ShapeDtypeStruct((n, 128), jnp.int32)],
        compiler_params=_cparams(("parallel",)),
        name="merge_router",
    )(x, y_rwkv, y_attn, gates, gates, w_up_r, w_up_a, w_out, norm2_g, w_router, b_router)


GATHER_UNROLL = 16


def _start_row_gather(src_hbm, idx_ref, buf, sem):
    rows = buf.shape[0] // ROW_TILES

    def start(g, carry):
        for u in range(GATHER_UNROLL):
            r = g * GATHER_UNROLL + u
            dst = buf.at[pl.ds(pl.multiple_of(r * ROW_TILES, ROW_TILES), ROW_TILES)]
            pltpu.make_async_copy(src_hbm.at[idx_ref[0, 0, r]], dst, sem).start(priority=u % 2)
        return carry

    lax.fori_loop(0, rows // GATHER_UNROLL, start, 0)


def _load_row_chunk(buf, slot, row0, rows, chunk):
    return buf[slot, pl.ds(row0 * ROW_TILES + chunk, rows, stride=ROW_TILES), :]


def _wait_row_gather(buf, sem):
    pltpu.make_async_copy(buf, buf, sem).wait()


def _dispatch_kernel(brows_ref, tok_ref, tok_next_ref, h_hbm, o_ref, buf, sems):
    s = pl.program_id(0)
    n_sub = ROW_BLOCK // ROW_SUB

    def has_rows(t):
        return lax.rem(t, n_sub) * ROW_SUB < brows_ref[lax.div(t, n_sub)]

    slot = s % 2

    @pl.when((s == 0) & has_rows(0))
    def _():
        _start_row_gather(h_hbm, tok_ref, buf.at[0], sems.at[0])

    @pl.when((s + 1 < pl.num_programs(0)) & has_rows(jnp.minimum(s + 1, pl.num_programs(0) - 1)))
    def _():
        _start_row_gather(h_hbm, tok_next_ref, buf.at[1 - slot], sems.at[1 - slot])

    @pl.when(has_rows(s))
    def _():
        _wait_row_gather(buf.at[slot], sems.at[slot])
        half = o_ref.shape[1] // 2
        for j in range(ROW_TILES):
            lo, hi = _unpack_bf16_pair(_load_row_chunk(buf, slot, 0, ROW_SUB, j))
            o_ref[:, j * 128:(j + 1) * 128] = lo.astype(o_ref.dtype)
            o_ref[:, half + j * 128:half + (j + 1) * 128] = hi.astype(o_ref.dtype)

    @pl.when(jnp.logical_not(has_rows(s)))
    def _():
        o_ref[...] = jnp.zeros_like(o_ref)


def _dispatch(h2p, row_tok, block_rows, n_blocks):
    d = 2 * ROW_TILES * 128
    steps = n_blocks * (ROW_BLOCK // ROW_SUB)
    grid_spec = pltpu.PrefetchScalarGridSpec(
        num_scalar_prefetch=1,
        grid=(steps,),
        in_specs=[pl.BlockSpec((1, 1, ROW_SUB), lambda s, br: (s, 0, 0), memory_space=pltpu.SMEM),
                  pl.BlockSpec((1, 1, ROW_SUB), lambda s, br: (jnp.minimum(s + 1, steps - 1), 0, 0),
                               memory_space=pltpu.SMEM),
                  pl.BlockSpec(memory_space=pl.ANY)],
        out_specs=pl.BlockSpec((ROW_SUB, d), lambda s, br: (s, 0)),
        scratch_shapes=[pltpu.VMEM((2, ROW_SUB * ROW_TILES, 128), jnp.uint32), pltpu.SemaphoreType.DMA((2,))],
    )
    tok = row_tok.reshape(steps, 1, ROW_SUB)
    return pl.pallas_call(
        _dispatch_kernel,
        grid_spec=grid_spec,
        out_shape=jax.ShapeDtypeStruct((n_blocks * ROW_BLOCK, d), BF16),
        compiler_params=_cparams(("arbitrary",)),
        name="moe_dispatch",
    )(block_rows, tok, tok, h2p)


def _ffn_kernel(nvalid_ref, bexp_ref, brows_ref, xs_ref, w1g_ref, w1l_ref, b1g_ref, b1l_ref, w2a_ref, w2b_ref,
                b2a_ref, b2b_ref, o_ref, act_ref):
    i = pl.program_id(0)
    j = pl.program_id(1)
    n_ff = act_ref.shape[0]
    rows_valid = brows_ref[i]
    n_sub = ROW_BLOCK // ROW_SUB

    full = rows_valid == ROW_BLOCK
    whole = slice(0, ROW_BLOCK)
    subs = [slice(sb * ROW_SUB, (sb + 1) * ROW_SUB) for sb in range(n_sub)]

    def hidden(rs, w1g, w1l):
        x = xs_ref[rs, :]
        hg = _mm(x, w1g) + b1g_ref[...]
        hl = _mm(x, w1l) + b1l_ref[...]
        glu = jnp.minimum(hg, SWIGLU_LIMIT)
        lin = jnp.clip(hl, -SWIGLU_LIMIT, SWIGLU_LIMIT)
        act_ref[j, rs, :] = (glu * _sigmoid(SWIGLU_ALPHA * glu) * (lin + 1.0)).astype(BF16)

    def project(rs, w2a, w2b):
        def half(w2, b2_ref):
            acc = _mm(act_ref[0, rs, :], w2[0:FF_TILE])
            for jf in range(1, n_ff):
                acc = acc + _mm(act_ref[jf, rs, :], w2[jf * FF_TILE:(jf + 1) * FF_TILE])
            return acc + b2_ref[...]

        write_out(rs, _pack_bf16_pair(half(w2a, b2a_ref), half(w2b, b2b_ref)))

    def write_out(rs, packed):
        for cc in range(OUT_TILE // 128):
            _store_row_chunk(o_ref, rs.start, (j - n_ff) * (OUT_TILE // 128) + cc, packed[:, cc * 128:(cc + 1) * 128])

    @pl.when((j < n_ff) & full)
    def _():
        hidden(whole, w1g_ref[...].astype(BF16), w1l_ref[...].astype(BF16))

    @pl.when((j < n_ff) & jnp.logical_not(full))
    def _():
        w1g = w1g_ref[...].astype(BF16)
        w1l = w1l_ref[...].astype(BF16)
        for sb, rs in enumerate(subs):
            pl.when(sb * ROW_SUB < rows_valid)(functools.partial(hidden, rs, w1g, w1l))

    @pl.when((j >= n_ff) & full)
    def _():
        project(whole, w2a_ref[...].astype(BF16), w2b_ref[...].astype(BF16))

    @pl.when((j >= n_ff) & jnp.logical_not(full))
    def _():
        w2a = w2a_ref[...].astype(BF16)
        w2b = w2b_ref[...].astype(BF16)
        for sb, rs in enumerate(subs):
            pl.when(sb * ROW_SUB < rows_valid)(functools.partial(project, rs, w2a, w2b))

            @pl.when(sb * ROW_SUB >= rows_valid)
            def _():
                write_out(rs, jnp.zeros((ROW_SUB, OUT_TILE), o_ref.dtype))


def _expert_ffn(xs, w1, b1, w2, b2, block_expert, block_rows, n_valid, n_blocks):
    d = xs.shape[1]
    assert d // 2 == ROW_TILES * 128
    n_exp, _, ff2 = w1.shape
    ff = ff2 // 2
    n_ff = ff // FF_TILE
    n_out = d // 2 // OUT_TILE

    def hid(i, j, nv):
        return jnp.where(i < nv[0], jnp.minimum(j, n_ff - 1), n_ff - 1)

    def w2_expert(i, j, be):
        return jnp.where(j < n_ff, be[jnp.maximum(i - 1, 0)], be[i])

    def w2_tile(i, j, nv):
        return jnp.where((i < nv[0]) & (j >= n_ff), j - n_ff, n_out - 1)

    grid_spec = pltpu.PrefetchScalarGridSpec(
        num_scalar_prefetch=3,
        grid=(n_blocks, n_ff + n_out),
        in_specs=[pl.BlockSpec((ROW_BLOCK, d), lambda i, j, nv, be, br: (i, 0)),
                  pl.BlockSpec((None, d, FF_TILE), lambda i, j, nv, be, br: (be[i], 0, hid(i, j, nv))),
                  pl.BlockSpec((None, d, FF_TILE), lambda i, j, nv, be, br: (be[i], 0, n_ff + hid(i, j, nv))),
                  pl.BlockSpec((None, 1, FF_TILE), lambda i, j, nv, be, br: (be[i], 0, hid(i, j, nv))),
                  pl.BlockSpec((None, 1, FF_TILE), lambda i, j, nv, be, br: (be[i], 0, n_ff + hid(i, j, nv))),
                  pl.BlockSpec((None, ff, OUT_TILE),
                               lambda i, j, nv, be, br: (w2_expert(i, j, be), 0, w2_tile(i, j, nv))),
                  pl.BlockSpec((None, ff, OUT_TILE),
                               lambda i, j, nv, be, br: (w2_expert(i, j, be), 0, n_out + w2_tile(i, j, nv))),
                  pl.BlockSpec((None, 1, OUT_TILE),
                               lambda i, j, nv, be, br: (w2_expert(i, j, be), 0, w2_tile(i, j, nv))),
                  pl.BlockSpec((None, 1, OUT_TILE),
                               lambda i, j, nv, be, br: (w2_expert(i, j, be), 0, n_out + w2_tile(i, j, nv)))],
        out_specs=pl.BlockSpec((ROW_BLOCK * ROW_TILES, 128), lambda i, j, nv, be, br: (i, 0)),
        scratch_shapes=[pltpu.VMEM((n_ff, ROW_BLOCK, FF_TILE), BF16)],
    )
    b1r = b1.reshape(n_exp, 1, ff2)
    b2r = b2.reshape(n_exp, 1, d)
    return pl.pallas_call(
        _ffn_kernel,
        grid_spec=grid_spec,
        out_shape=jax.ShapeDtypeStruct((n_blocks * ROW_BLOCK * ROW_TILES, 128), jnp.uint32),
        compiler_params=_cparams(("arbitrary", "arbitrary")),
        name="moe_ffn",
    )(n_valid, block_expert, block_rows, xs, w1, w1, b1r, b1r, w2, w2, b2r, b2r)


def _combine_kernel(pos_ref, pos_next_ref, x1_ref, gate_ref, nf_ref, ys_hbm, o_ref, buf, sems):
    s = pl.program_id(0)
    tc = x1_ref.shape[0]
    slot = s % 2

    @pl.when(s == 0)
    def _():
        _start_row_gather(ys_hbm, pos_ref, buf.at[0], sems.at[0])

    @pl.when(s + 1 < pl.num_programs(0))
    def _():
        _start_row_gather(ys_hbm, pos_next_ref, buf.at[1 - slot], sems.at[1 - slot])

    _wait_row_gather(buf.at[slot], sems.at[slot])
    half = x1_ref.shape[1] // 2
    gate = gate_ref[...]
    gates = [gate[:, k:k + 1] for k in range(TOP_K)]
    accs = []
    ssq = jnp.zeros((tc, 1), F32)
    for j in range(ROW_TILES):
        lo_cols = slice(j * 128, (j + 1) * 128)
        hi_cols = slice(half + j * 128, half + (j + 1) * 128)
        acc_lo = x1_ref[:, lo_cols]
        acc_hi = x1_ref[:, hi_cols]
        for k in range(TOP_K):
            lo, hi = _unpack_bf16_pair(_load_row_chunk(buf, slot, k * tc, tc, j))
            acc_lo = acc_lo + gates[k] * lo
            acc_hi = acc_hi + gates[k] * hi
        accs += [(lo_cols, acc_lo), (hi_cols, acc_hi)]
        ssq = ssq + jnp.sum(acc_lo * acc_lo, axis=-1, keepdims=True) + jnp.sum(acc_hi * acc_hi, axis=-1, keepdims=True)
    scale = lax.rsqrt(ssq * (1.0 / x1_ref.shape[1]) + NORM_EPS)
    for cols, acc in accs:
        o_ref[:, cols] = acc * scale * nf_ref[:, cols]


def _combine(x1, gate, pos, ys, normf_g, tc):
    n, d = x1.shape
    nt = n // tc
    return pl.pallas_call(
        _combine_kernel,
        grid=(nt,),
        in_specs=[pl.BlockSpec((1, 1, TOP_K * tc), lambda i: (i, 0, 0), memory_space=pltpu.SMEM),
                  pl.BlockSpec((1, 1, TOP_K * tc), lambda i: (jnp.minimum(i + 1, nt - 1), 0, 0),
                               memory_space=pltpu.SMEM),
                  pl.BlockSpec((tc, d), lambda i: (i, 0)), pl.BlockSpec((tc, 128), lambda i: (i, 0)),
                  _resident((1, d)), pl.BlockSpec(memory_space=pl.ANY)],
        out_specs=pl.BlockSpec((tc, d), lambda i: (i, 0)),
        out_shape=jax.ShapeDtypeStruct((n, d), F32),
        scratch_shapes=[pltpu.VMEM((2, TOP_K * tc * ROW_TILES, 128), jnp.uint32), pltpu.SemaphoreType.DMA((2,))],
        compiler_params=_cparams(("arbitrary",)),
        name="moe_combine",
    )(pos, pos, x1, gate, normf_g, ys)


def _route(top_idx, n_exp, tc):
    n = top_idx.shape[0]
    n_assign = n * TOP_K
    n_blocks = n_assign // ROW_BLOCK + n_exp
    tile = min(1024, n)
    hot = (top_idx[:, :, None] == jnp.arange(n_exp, dtype=jnp.int32)).any(axis=1).reshape(n // tile, tile, n_exp)
    earlier = jnp.asarray(np.tril(np.ones((tile, tile), np.float32), -1), BF16)
    within = jnp.einsum("ts,bse->bte", earlier, hot.astype(BF16), preferred_element_type=F32)
    tile_tot = jnp.sum(hot, axis=1, dtype=jnp.int32)
    tile_off = jnp.cumsum(tile_tot, axis=0) - tile_tot
    rank = (within.astype(jnp.int32) + tile_off[:, None, :]).reshape(n, n_exp)
    counts = jnp.sum(tile_tot, axis=0)
    padded = (counts + ROW_BLOCK - 1) // ROW_BLOCK * ROW_BLOCK
    pad_end = jnp.cumsum(padded)
    pad_start = pad_end - padded
    grp_start = jnp.cumsum(counts) - counts
    slot = jnp.take_along_axis(rank + pad_start[None, :], top_idx, axis=1).reshape(-1)
    block_start = jnp.arange(n_blocks, dtype=jnp.int32) * ROW_BLOCK
    block_expert = jnp.minimum(jnp.sum(pad_end[None, :] <= block_start[:, None], axis=1), n_exp - 1).astype(jnp.int32)
    n_valid = (pad_end[-1] // ROW_BLOCK).astype(jnp.int32).reshape(1)
    order = jnp.argsort(top_idx.reshape(-1), stable=True)
    row_e = jnp.repeat(block_expert, ROW_BLOCK)
    row_r = jnp.arange(n_blocks * ROW_BLOCK, dtype=jnp.int32) - pad_start[row_e]
    src = jnp.clip(grp_start[row_e] + row_r, 0, n_assign - 1)
    row_tok = jnp.where((row_r >= 0) & (row_r < counts[row_e]), order[src] // TOP_K, 0).astype(jnp.int32)
    pos = slot.reshape(n // tc, tc, TOP_K).transpose(0, 2, 1).reshape(n // tc, 1, TOP_K * tc)
    block_rows = jnp.clip(counts[block_expert] - (block_start - pad_start[block_expert]), 0, ROW_BLOCK)
    block_rows = jnp.where(jnp.arange(n_blocks) < n_valid[0], block_rows, 0).astype(jnp.int32)
    return row_tok, block_expert, block_rows, n_valid, pos, n_blocks


def _layer(x, norm1_g, w_in, mix_mu, w0, w_decay_up, a0, w_iclr_up, w_gate_up, k_k, k_a, r_k, lnx_g, lnx_b, b_qkv,
           sinks, w_up_rwkv, w_up_attn, w_out, norm2_g, w_router, b_router, w1, b1, w2, b2, normf_g):
    b, s, d = x.shape
    n = b * s
    c = w_up_rwkv.shape[0]
    lora = w_decay_up.shape[0]
    qc = w_up_attn.shape[0]
    kvc = KV_HEADS * HEAD
    rwkv_cols = 3 * c + 2 * lora + w_gate_up.shape[0]
    qkv_cols = qc + 2 * kvc
    row = lambda t: t.reshape(1, -1).astype(F32)
    xf = x.reshape(n, d)
    g1 = row(norm1_g)
    w_in_b = w_in.astype(BF16)

    qkv = _norm_proj(xf, g1, w_in_b[:, rwkv_cols:rwkv_cols + qkv_cols], row(b_qkv), BF16, 512)
    gates = _norm_proj(xf, g1, w_in_b[:, rwkv_cols + qkv_cols:], jnp.zeros((1, 2 * d), F32), BF16, 512)

    zl = jnp.zeros((lora, c), F32)
    w_lora = jnp.concatenate([jnp.concatenate([w_decay_up, zl], axis=1), jnp.concatenate([zl, w_iclr_up], axis=1)],
                             axis=0)
    r, lw, kf, v, kn, ba, g = _rwkv_prep(xf, g1, w_in_b[:, :rwkv_cols], s, c, row(mix_mu), w_lora, w_gate_up, row(w0),
                                         row(a0), row(k_k), row(k_a), 256)
    as3 = lambda t: t.reshape(b, s, c)
    y_rwkv = _rwkv_scan(as3(r), as3(lw), as3(kf), as3(v), as3(kn), as3(ba), as3(g), row(r_k), row(lnx_g),
                        row(lnx_b)).reshape(n, c)

    q = qkv[:, :qc].reshape(b, s, qc)
    ka = qkv[:, qc:qc + kvc].reshape(b, s, kvc)
    va = qkv[:, qc + kvc:].reshape(b, s, kvc)
    sinks_b = jnp.broadcast_to(sinks.astype(F32).reshape(-1, 1), (sinks.shape[0], 128))
    y_attn = _attention(q, ka, va, sinks_b).reshape(n, qc)

    x1, h2, gate, top_idx = _merge(xf, y_rwkv, y_attn, gates, w_up_rwkv.astype(BF16), w_up_attn.astype(BF16),
                                   w_out.astype(BF16), row(norm2_g), w_router, row(b_router), 256)

    tc = 256
    row_tok, block_expert, block_rows, n_valid, pos, n_blocks = _route(top_idx[:, :TOP_K], w_router.shape[1], tc)
    xs = _dispatch(h2.reshape(n, ROW_TILES, 128), row_tok, block_rows, n_blocks)
    ys = _expert_ffn(xs, w1, b1, w2, b2, block_expert, block_rows, n_valid, n_blocks)
    out = _combine(x1, gate, pos, ys.reshape(-1, ROW_TILES, 128), row(normf_g), tc)
    return out.reshape(b, s, d)


def kernel(x, norm1_g, w_in, mix_mu, w0, w_decay_up, a0, w_iclr_up, w_gate_up, k_k, k_a, r_k, lnx_g, lnx_b, b_qkv,
           sinks, w_up_rwkv, w_up_attn, w_out, norm2_g, w_router, b_router, w1, b1, w2, b2, normf_g):
    assert w_in.shape[0] == 1, "single-layer block"
    return _layer(x, norm1_g[0], w_in[0], mix_mu[0], w0[0], w_decay_up[0], a0[0], w_iclr_up[0], w_gate_up[0],
                  k_k[0], k_a[0], r_k[0], lnx_g[0], lnx_b[0], b_qkv[0], sinks[0], w_up_rwkv[0], w_up_attn[0],
                  w_out[0], norm2_g[0], w_router[0], b_router[0], w1[0], b1[0], w2[0], b2[0], normf_g)
```

```python
import functools

import jax
import jax.numpy as jnp
import numpy as np
from jax import lax
from jax.experimental import pallas as pl
from jax.experimental.pallas import tpu as pltpu

F32 = jnp.float32
BF16 = jnp.bfloat16

NORM_EPS = 1e-5
LNX_EPS = 64e-5
HEAD = 64
WINDOW = 128
KV_HEADS = 4
TOP_K = 4
SWIGLU_LIMIT = 7.0
SWIGLU_ALPHA = 1.702
DECAY_SCALE = float(np.exp(-0.5))

CHUNK = 64
ROW_BLOCK = 1152
ROW_SUB = 384
FF_TILE = 512
OUT_TILE = 256
VMEM_LIMIT = 56 * 1024 * 1024


def _cparams(sem):
    return pltpu.CompilerParams(dimension_semantics=sem, vmem_limit_bytes=VMEM_LIMIT)


def _resident(shape):
    nd = len(shape)
    return pl.BlockSpec(shape, lambda *_: (0,) * nd, pipeline_mode=pl.Buffered(1))


def _rmsnorm(x, g):
    return x * lax.rsqrt(jnp.mean(x * x, axis=-1, keepdims=True) + NORM_EPS) * g


def _sigmoid(x):
    return 1.0 / (1.0 + jnp.exp(-x))


def _pack_bf16_pair(lo, hi):
    lo_bits = lax.bitcast_convert_type(lo.astype(BF16).astype(F32), jnp.uint32)
    hi_bits = lax.bitcast_convert_type(hi.astype(BF16).astype(F32), jnp.uint32)
    return (hi_bits & jnp.uint32(0xFFFF0000)) | (lo_bits >> 16)


ROW_TILES = 8


def _store_row_chunk(ref, row0, chunk, value):
    ref[pl.ds(row0 * ROW_TILES + chunk, value.shape[0], stride=ROW_TILES), :] = value


def _store_row_tiles(ref, row0, packed):
    for j in range(ROW_TILES):
        _store_row_chunk(ref, row0, j, packed[:, j * 128:(j + 1) * 128])


def _unpack_bf16_pair(packed):
    lo = lax.bitcast_convert_type(packed << 16, F32)
    hi = lax.bitcast_convert_type(packed & jnp.uint32(0xFFFF0000), F32)
    return lo, hi


def _norm_proj_kernel(x_ref, g_ref, w_ref, b_ref, o_ref):
    h = _rmsnorm(x_ref[...], g_ref[...]).astype(BF16)
    z = jnp.dot(h, w_ref[...], preferred_element_type=F32) + b_ref[...]
    o_ref[...] = z.astype(o_ref.dtype)


def _norm_proj(x, g, w, b, out_dtype, tm):
    n, d = x.shape
    cols = w.shape[1]
    return pl.pallas_call(
        _norm_proj_kernel,
        grid=(n // tm,),
        in_specs=[pl.BlockSpec((tm, d), lambda i: (i, 0)), _resident((1, d)), _resident((d, cols)),
                  _resident((1, cols))],
        out_specs=pl.BlockSpec((tm, cols), lambda i: (i, 0)),
        out_shape=jax.ShapeDtypeStruct((n, cols), out_dtype),
        compiler_params=_cparams(("parallel",)),
        name="norm_proj",
    )(x, g, w, b)


def _head_sums(x):
    rows, c = x.shape
    lane = lax.broadcasted_iota(jnp.int32, (rows, 128), 1)
    low = lane < HEAD
    parts = []
    for gi in range(c // 128):
        xg = x[:, gi * 128:(gi + 1) * 128]
        s_lo = jnp.sum(jnp.where(low, xg, 0.0), axis=-1, keepdims=True)
        s_hi = jnp.sum(jnp.where(low, 0.0, xg), axis=-1, keepdims=True)
        parts.append(jnp.where(low, s_lo, s_hi))
    return jnp.concatenate(parts, axis=-1)


def _dot_bf16x3(a, w):
    a_hi = a.astype(BF16)
    a_lo = (a - a_hi.astype(F32)).astype(BF16)
    w_hi = w.astype(BF16)
    w_lo = (w - w_hi.astype(F32)).astype(BF16)
    return (jnp.dot(a_hi, w_hi, preferred_element_type=F32) + jnp.dot(a_lo, w_hi, preferred_element_type=F32)
            + jnp.dot(a_hi, w_lo, preferred_element_type=F32))


def _prep_kernel(seq_blocks, c, x_ref, n1_ref, win_ref, mu_ref, wlora_ref, wgate_ref, w0_ref, a0_ref, kk_ref, ka_ref,
                 r_ref, lw_ref, kf_ref, v_ref, kn_ref, ba_ref, g_ref, last_ref):
    i = pl.program_id(0)

    @pl.when(i == 0)
    def _():
        last_ref[...] = jnp.zeros_like(last_ref)

    h = _rmsnorm(x_ref[...], n1_ref[...]).astype(BF16)
    z = jnp.dot(h, win_ref[...], preferred_element_type=F32)
    tm = z.shape[0]
    prev = jnp.where(i % seq_blocks == 0, 0.0, last_ref[0:1, :])
    last_ref[0:1, :] = z[tm - 1:tm, :]
    row = lax.broadcasted_iota(jnp.int32, z.shape, 0)
    shifted = jnp.where(row == 0, prev, pltpu.roll(z, 1, 0))
    zs = z + (shifted - z) * mu_ref[...]
    r = zs[:, 0:c]
    k = zs[:, c:2 * c]
    v = zs[:, 2 * c:3 * c]
    zwa = zs[:, 3 * c:3 * c + 128]
    zg = zs[:, 3 * c + 128:3 * c + 256]
    lane = lax.broadcasted_iota(jnp.int32, (tm, 128), 1)
    lora_in = jnp.where(lane < 64, jnp.tanh(zwa), zwa)
    up = _dot_bf16x3(lora_in, wlora_ref[...])
    u = w0_ref[...] + up[:, 0:c]
    a = _sigmoid(a0_ref[...] + up[:, c:2 * c])
    g = _dot_bf16x3(_sigmoid(zg), wgate_ref[...])
    lw = -DECAY_SCALE * _sigmoid(u)
    kk = k * kk_ref[...]
    kn = kk / jnp.maximum(jnp.sqrt(_head_sums(kk * kk)), 1e-12)
    kf = k * (1.0 + (a - 1.0) * ka_ref[...])
    r_ref[...] = r
    lw_ref[...] = lw
    kf_ref[...] = kf
    v_ref[...] = v
    kn_ref[...] = kn
    ba_ref[...] = kn * a
    g_ref[...] = g


def _rwkv_prep(x, norm_g, w_in_rwkv, seq, c, mix_mu, w_lora, w_gate, w0, a0, k_k, k_a, tm):
    n, d = x.shape
    zc = w_in_rwkv.shape[1]
    row_spec = pl.BlockSpec((tm, c), lambda i: (i, 0))
    out = jax.ShapeDtypeStruct((n, c), F32)
    return pl.pallas_call(
        functools.partial(_prep_kernel, seq // tm, c),
        grid=(n // tm,),
        in_specs=[pl.BlockSpec((tm, d), lambda i: (i, 0)), _resident((1, d)), _resident((d, zc)),
                  _resident((1, zc)), _resident((128, 2 * c)), _resident((128, c)),
                  _resident((1, c)), _resident((1, c)), _resident((1, c)), _resident((1, c))],
        out_specs=[row_spec] * 7,
        out_shape=[out] * 7,
        scratch_shapes=[pltpu.VMEM((8, zc), F32)],
        compiler_params=_cparams(("arbitrary",)),
        name="rwkv_prep",
    )(x, norm_g, w_in_rwkv, mix_mu, w_lora, w_gate, w0, a0, k_k, k_a)


def _split3(x):
    h1 = x.astype(BF16)
    r1 = x - h1.astype(F32)
    h2 = r1.astype(BF16)
    h3 = (r1 - h2.astype(F32)).astype(BF16)
    return h1, h2, h3


def _mm(a, b):
    return jnp.dot(a, b, preferred_element_type=F32)


def _mm_nt(a, b):
    return lax.dot_general(a, b, (((1,), (1,)), ((), ())), preferred_element_type=F32)


def _mm_tn(a, b):
    return lax.dot_general(a, b, (((0,), (0,)), ((), ())), preferred_element_type=F32)


GROUP_HEADS = 4
GROUP_LANES = GROUP_HEADS * HEAD


def _block_diag(x, mask):
    return jnp.concatenate([x.astype(BF16)] * GROUP_HEADS, axis=0) * mask


def _scan_kernel(r_ref, lw_ref, kf_ref, v_ref, kn_ref, ba_ref, g_ref, rk_ref, lng_ref, lnb_ref, mask_ref, y_ref,
                 s_ref):
    @pl.when(pl.program_id(0) == 0)
    def _():
        s_ref[...] = jnp.zeros_like(s_ref)

    t = CHUNK
    nb = r_ref.shape[0]
    c = r_ref.shape[2]
    gl = GROUP_LANES
    n_groups = c // gl
    mask = mask_ref[...]
    ti = lax.broadcasted_iota(jnp.int32, (t, t), 0)
    si = lax.broadcasted_iota(jnp.int32, (t, t), 1)
    tri = jnp.where(si <= ti, 1.0, 0.0).astype(BF16)
    row = lax.broadcasted_iota(jnp.int32, (t, gl), 0)
    col = lax.broadcasted_iota(jnp.int32, (t, gl), 1) & (HEAD - 1)
    strict = col < row
    incl = col <= row
    eye = jnp.where(col == row, 1.0, 0.0)

    prep = []
    for b in range(nb):
        lw = lw_ref[b]
        cum = sum(jnp.dot(tri, part, preferred_element_type=F32) for part in _split3(lw))
        cum_end = cum[t - 1:t, :]
        e_pos = jnp.exp(cum)
        e_neg = jnp.exp(-cum)
        e_end = jnp.exp(cum_end - cum)
        r = r_ref[b]
        kf = kf_ref[b]
        ba = ba_ref[b]
        prep.append(dict(
            rt=r * e_pos,
            at=-kn_ref[b] * jnp.exp(cum - lw),
            bt=ba * e_neg,
            kt=kf * e_neg,
            bh=ba * e_end,
            kh=kf * e_end,
            v=v_ref[b],
            w_end=jnp.exp(cum_end),
            rkf=r * kf * rk_ref[...]))

    probs = [(b, gi) for b in range(nb) for gi in range(n_groups)]

    def part(b, gi, name):
        return prep[b][name][:, gi * gl:(gi + 1) * gl]

    a_ab, a_ak, a_rb, a_rk = [], [], [], []
    for b, gi in probs:
        lhs = jnp.concatenate([part(b, gi, "at"), part(b, gi, "rt")], axis=0).astype(BF16)
        rhs = jnp.concatenate([_block_diag(part(b, gi, "bt"), mask), _block_diag(part(b, gi, "kt"), mask)], axis=0)
        amat = _mm_nt(lhs, rhs)
        a_ab.append(jnp.where(strict, amat[0:t, 0:gl], 0.0))
        a_ak.append(jnp.where(strict, amat[0:t, gl:2 * gl], 0.0))
        a_rb.append(jnp.where(incl, amat[t:2 * t, 0:gl], 0.0))
        a_rk.append(jnp.where(incl, amat[t:2 * t, gl:2 * gl], 0.0))

    vbd = [_block_diag(part(b, gi, "v"), mask) for b, gi in probs]
    av = [_mm(a_ak[i].astype(BF16), vbd[i]) for i in range(len(probs))]
    minv = [eye + a for a in a_ab]
    power = [_mm(a.astype(BF16), _block_diag(a, mask)) for a in a_ab]
    span = 2
    while span < t:
        last = span * 2 >= t
        for i in range(len(probs)):
            pbd = _block_diag(power[i], mask)
            if last:
                minv[i] = minv[i] + _mm(minv[i].astype(BF16), pbd)
            else:
                both = _mm(jnp.concatenate([power[i], minv[i]], axis=0).astype(BF16), pbd)
                power[i] = both[0:t]
                minv[i] = minv[i] + both[t:2 * t]
        span *= 2

    s0 = [s_ref[b, gi] for b, gi in probs]
    sbd = [_block_diag(s, mask) for s in s0]
    ps = []
    for i, (b, gi) in enumerate(probs):
        lhs = jnp.concatenate([part(b, gi, "at"), part(b, gi, "rt")], axis=0).astype(BF16)
        ps.append(_mm_nt(lhs, sbd[i]))
    u = [_mm(minv[i].astype(BF16), _block_diag(ps[i][0:t] + av[i], mask)) for i in range(len(probs))]
    ys = []
    for i, (b, gi) in enumerate(probs):
        lhs = jnp.concatenate([a_rb[i], a_rk[i]], axis=1).astype(BF16)
        rhs = jnp.concatenate([_block_diag(u[i], mask), vbd[i]], axis=0)
        ys.append(ps[i][t:2 * t] + _mm(lhs, rhs))
        uv = jnp.concatenate([u[i], part(b, gi, "v")], axis=0).astype(BF16)
        bk = jnp.concatenate([part(b, gi, "bh"), part(b, gi, "kh")], axis=0).astype(BF16)
        full = _mm_tn(uv, bk) * mask.astype(F32)
        upd = full[0:HEAD]
        for hh in range(1, GROUP_HEADS):
            upd = upd + full[hh * HEAD:(hh + 1) * HEAD]
        s_ref[b, gi] = s0[i] * part(b, gi, "w_end") + upd

    for b in range(nb):
        y = jnp.concatenate([ys[b * n_groups + gi] for gi in range(n_groups)], axis=1)
        mu = _head_sums(y) * (1.0 / HEAD)
        yc = y - mu
        var = _head_sums(yc * yc) * (1.0 / HEAD)
        yn = yc * lax.rsqrt(var + LNX_EPS)
        bonus = _head_sums(prep[b]["rkf"]) * prep[b]["v"]
        y_ref[b] = ((yn * lng_ref[...] + lnb_ref[...] + bonus) * g_ref[b]).astype(y_ref.dtype)


def _rwkv_scan(r, lw, kf, v, kn, ba, g, r_k, lnx_g, lnx_b):
    b, s, c = r.shape
    blk = pl.BlockSpec((b, CHUNK, c), lambda ci: (0, ci, 0))
    hid = np.arange(GROUP_LANES) // HEAD
    mask = jnp.asarray(hid[:, None] == hid[None, :], BF16)
    return pl.pallas_call(
        _scan_kernel,
        grid=(s // CHUNK,),
        in_specs=[blk] * 7 + [_resident((1, c))] * 3 + [_resident((GROUP_LANES, GROUP_LANES))],
        out_specs=blk,
        out_shape=jax.ShapeDtypeStruct((b, s, c), BF16),
        scratch_shapes=[pltpu.VMEM((b, c // GROUP_LANES, HEAD, GROUP_LANES), F32)],
        compiler_params=_cparams(("arbitrary",)),
        name="rwkv_scan",
    )(r, lw, kf, v, kn, ba, g, r_k, lnx_g, lnx_b, mask)


def _attn_kernel(q_ref, kc_ref, kp_ref, vc_ref, vp_ref, sink_ref, o_ref):
    first = pl.program_id(1) == 0
    w = WINDOW
    group = q_ref.shape[-1] // HEAD // KV_HEADS
    row = lax.broadcasted_iota(jnp.int32, (group * w, 2 * w), 0)
    qi = row & (w - 1)
    kj = lax.broadcasted_iota(jnp.int32, (group * w, 2 * w), 1)
    lo = jnp.where(first, jnp.maximum(qi, w - 1), qi)
    valid = (kj > lo) & (kj <= qi + w)
    grow = lax.broadcasted_iota(jnp.int32, (group * w, 1), 0) // w
    q = q_ref[...]
    outs = []
    for hk in range(KV_HEADS):
        ksl = slice(hk * HEAD, (hk + 1) * HEAD)
        kcat = jnp.concatenate([kp_ref[:, ksl], kc_ref[:, ksl]], axis=0)
        vcat = jnp.concatenate([vp_ref[:, ksl], vc_ref[:, ksl]], axis=0)
        qg = jnp.concatenate([q[:, (hk * group + gi) * HEAD:(hk * group + gi + 1) * HEAD] for gi in range(group)],
                             axis=0)
        sink = jnp.zeros((group * w, 1), F32)
        for gi in range(group):
            sink = jnp.where(grow == gi, sink_ref[hk * group + gi:hk * group + gi + 1, 0:1], sink)
        s = _mm_nt(qg, kcat) * (HEAD ** -0.5)
        s = jnp.where(valid, s, -1e30)
        m = jnp.maximum(jnp.max(s, axis=-1, keepdims=True), sink)
        p = jnp.exp(s - m)
        pb = p.astype(BF16)
        psum = _mm(pb, jnp.ones((2 * w, HEAD), BF16))
        o = _mm(pb, vcat) / (psum + jnp.exp(sink - m))
        for gi in range(group):
            outs.append(o[gi * w:(gi + 1) * w, :])
    o_ref[...] = jnp.concatenate(outs, axis=-1).astype(o_ref.dtype)


def _attention(q, k, v, sinks_b):
    b, s, qc = q.shape
    kc = k.shape[-1]
    nb = s // WINDOW
    cur = lambda bi, i: (bi, i, 0)
    prev = lambda bi, i: (bi, jnp.maximum(i - 1, 0), 0)
    return pl.pallas_call(
        _attn_kernel,
        grid=(b, nb),
        in_specs=[pl.BlockSpec((None, WINDOW, qc), cur),
                  pl.BlockSpec((None, WINDOW, kc), cur), pl.BlockSpec((None, WINDOW, kc), prev),
                  pl.BlockSpec((None, WINDOW, kc), cur), pl.BlockSpec((None, WINDOW, kc), prev),
                  _resident(sinks_b.shape)],
        out_specs=pl.BlockSpec((None, WINDOW, qc), cur),
        out_shape=jax.ShapeDtypeStruct((b, s, qc), BF16),
        compiler_params=_cparams(("parallel", "parallel")),
        name="swa_attention",
    )(q, k, k, v, v, sinks_b)


MERGE_SPLIT = 2


def _merge_kernel(n_exp, x_ref, yr_ref, ya_ref, gr_ref, ga_ref, wur_ref, wua_ref, wo_ref, n2_ref, wr_ref, br_ref,
                  x1_ref, h2_ref, gate_ref, idx_ref):
    tm = x_ref.shape[0] // MERGE_SPLIT
    parts = [slice(p * tm, (p + 1) * tm) for p in range(MERGE_SPLIT)]
    ups = [(_mm(yr_ref[rs, :], wur_ref[...]), _mm(ya_ref[rs, :], wua_ref[...])) for rs in parts]
    merged = [_sigmoid(gr_ref[rs, :].astype(F32)) * ur + _sigmoid(ga_ref[rs, :].astype(F32)) * ua
              for rs, (ur, ua) in zip(parts, ups)]
    x1s = [x_ref[rs, :] + _mm(m.astype(BF16), wo_ref[...]) for rs, m in zip(parts, merged)]
    h2s = []
    for rs, x1 in zip(parts, x1s):
        x1_ref[rs, :] = x1
        h2 = _rmsnorm(x1, n2_ref[...])
        half = h2.shape[1] // 2
        _store_row_tiles(h2_ref, rs.start, _pack_bf16_pair(h2[:, :half], h2[:, half:]))
        h2s.append(h2)
    all_logits = [_dot_bf16x3(h2, wr_ref[...]) + br_ref[...] for h2 in h2s]
    lane_e = lax.broadcasted_iota(jnp.int32, (tm, n_exp), 1).astype(F32)
    lane_o = lax.broadcasted_iota(jnp.int32, (tm, 128), 1)
    for rs, logits in zip(parts, all_logits):
        vals, idxs = [], []
        for _ in range(TOP_K):
            mx = jnp.max(logits, axis=-1, keepdims=True)
            ix = jnp.min(jnp.where(logits == mx, lane_e, float(n_exp)), axis=-1, keepdims=True)
            vals.append(mx)
            idxs.append(ix)
            logits = jnp.where(lane_e == ix, -jnp.inf, logits)
        exps = [jnp.exp(vk - vals[0]) for vk in vals]
        tot = exps[0] + exps[1] + exps[2] + exps[3]
        gate_o, idx_o = jnp.zeros((tm, 128), F32), jnp.zeros((tm, 128), F32)
        for kk in range(TOP_K):
            gate_o = jnp.where(lane_o == kk, exps[kk] / tot, gate_o)
            idx_o = jnp.where(lane_o == kk, idxs[kk], idx_o)
        gate_ref[rs, :] = gate_o
        idx_ref[rs, :] = idx_o.astype(jnp.int32)


def _merge(x, y_rwkv, y_attn, gates, w_up_r, w_up_a, w_out, norm2_g, w_router, b_router, tm):
    n, d = x.shape
    c = y_rwkv.shape[1]
    n_exp = w_router.shape[1]
    row = lambda cols, j=0: pl.BlockSpec((tm, cols), lambda i: (i, j))
    return pl.pallas_call(
        functools.partial(_merge_kernel, n_exp),
        grid=(n // tm,),
        in_specs=[row(d), row(c), row(c), row(d, 0), row(d, 1),
                  _resident((c, d)), _resident((c, d)), _resident((d, d)), _resident((1, d)),
                  _resident((d, n_exp)), _resident((1, n_exp))],
        out_specs=[row(d), pl.BlockSpec((tm * ROW_TILES, 128), lambda i: (i, 0)), row(128), row(128)],
        out_shape=[jax.ShapeDtypeStruct((n, d), F32), jax.ShapeDtypeStruct((n * ROW_TILES, 128), jnp.uint32),
                   jax.ShapeDtypeStruct((n, 128), F32), jax.ShapeDtypeStruct((n, 128), jnp.int32)],
        compiler_params=_cparams(("parallel",)),
        name="merge_router",
    )(x, y_rwkv, y_attn, gates, gates, w_up_r, w_up_a, w_out, norm2_g, w_router, b_router)


GATHER_UNROLL = 16


def _start_row_gather(src_hbm, idx_ref, buf, sem):
    rows = buf.shape[0] // ROW_TILES

    def start(g, carry):
        for u in range(GATHER_UNROLL):
            r = g * GATHER_UNROLL + u
            dst = buf.at[pl.ds(pl.multiple_of(r * ROW_TILES, ROW_TILES), ROW_TILES)]
            pltpu.make_async_copy(src_hbm.at[idx_ref[0, 0, r]], dst, sem).start(priority=u % 2)
        return carry

    lax.fori_loop(0, rows // GATHER_UNROLL, start, 0)


def _load_row_chunk(buf, slot, row0, rows, chunk):
    return buf[slot, pl.ds(row0 * ROW_TILES + chunk, rows, stride=ROW_TILES), :]


def _wait_row_gather(buf, sem):
    pltpu.make_async_copy(buf, buf, sem).wait()


def _dispatch_kernel(brows_ref, tok_ref, tok_next_ref, h_hbm, o_ref, buf, sems):
    s = pl.program_id(0)
    n_sub = ROW_BLOCK // ROW_SUB

    def has_rows(t):
        return lax.rem(t, n_sub) * ROW_SUB < brows_ref[lax.div(t, n_sub)]

    slot = s % 2

    @pl.when((s == 0) & has_rows(0))
    def _():
        _start_row_gather(h_hbm, tok_ref, buf.at[0], sems.at[0])

    @pl.when((s + 1 < pl.num_programs(0)) & has_rows(jnp.minimum(s + 1, pl.num_programs(0) - 1)))
    def _():
        _start_row_gather(h_hbm, tok_next_ref, buf.at[1 - slot], sems.at[1 - slot])

    @pl.when(has_rows(s))
    def _():
        _wait_row_gather(buf.at[slot], sems.at[slot])
        half = o_ref.shape[1] // 2
        for j in range(ROW_TILES):
            lo, hi = _unpack_bf16_pair(_load_row_chunk(buf, slot, 0, ROW_SUB, j))
            o_ref[:, j * 128:(j + 1) * 128] = lo.astype(o_ref.dtype)
            o_ref[:, half + j * 128:half + (j + 1) * 128] = hi.astype(o_ref.dtype)

    @pl.when(jnp.logical_not(has_rows(s)))
    def _():
        o_ref[...] = jnp.zeros_like(o_ref)


def _dispatch(h2p, row_tok, block_rows, n_blocks):
    d = 2 * ROW_TILES * 128
    steps = n_blocks * (ROW_BLOCK // ROW_SUB)
    grid_spec = pltpu.PrefetchScalarGridSpec(
        num_scalar_prefetch=1,
        grid=(steps,),
        in_specs=[pl.BlockSpec((1, 1, ROW_SUB), lambda s, br: (s, 0, 0), memory_space=pltpu.SMEM),
                  pl.BlockSpec((1, 1, ROW_SUB), lambda s, br: (jnp.minimum(s + 1, steps - 1), 0, 0),
                               memory_space=pltpu.SMEM),
                  pl.BlockSpec(memory_space=pl.ANY)],
        out_specs=pl.BlockSpec((ROW_SUB, d), lambda s, br: (s, 0)),
        scratch_shapes=[pltpu.VMEM((2, ROW_SUB * ROW_TILES, 128), jnp.uint32), pltpu.SemaphoreType.DMA((2,))],
    )
    tok = row_tok.reshape(steps, 1, ROW_SUB)
    return pl.pallas_call(
        _dispatch_kernel,
        grid_spec=grid_spec,
        out_shape=jax.ShapeDtypeStruct((n_blocks * ROW_BLOCK, d), BF16),
        compiler_params=_cparams(("arbitrary",)),
        name="moe_dispatch",
    )(block_rows, tok, tok, h2p)


def _ffn_kernel(nvalid_ref, bexp_ref, brows_ref, xs_ref, w1g_ref, w1l_ref, b1g_ref, b1l_ref, w2a_ref, w2b_ref,
                b2a_ref, b2b_ref, o_ref, act_ref):
    i = pl.program_id(0)
    j = pl.program_id(1)
    n_ff = act_ref.shape[0]
    rows_valid = brows_ref[i]
    n_sub = ROW_BLOCK // ROW_SUB

    full = rows_valid == ROW_BLOCK
    whole = slice(0, ROW_BLOCK)
    subs = [slice(sb * ROW_SUB, (sb + 1) * ROW_SUB) for sb in range(n_sub)]

    def hidden(rs, w1g, w1l):
        x = xs_ref[rs, :]
        hg = _mm(x, w1g) + b1g_ref[...]
        hl = _mm(x, w1l) + b1l_ref[...]
        glu = jnp.minimum(hg, SWIGLU_LIMIT)
        lin = jnp.clip(hl, -SWIGLU_LIMIT, SWIGLU_LIMIT)
        act_ref[j, rs, :] = (glu * _sigmoid(SWIGLU_ALPHA * glu) * (lin + 1.0)).astype(BF16)

    def project(rs, w2a, w2b):
        def half(w2, b2_ref):
            acc = _mm(act_ref[0, rs, :], w2[0:FF_TILE])
            for jf in range(1, n_ff):
                acc = acc + _mm(act_ref[jf, rs, :], w2[jf * FF_TILE:(jf + 1) * FF_TILE])
            return acc + b2_ref[...]

        write_out(rs, _pack_bf16_pair(half(w2a, b2a_ref), half(w2b, b2b_ref)))

    def write_out(rs, packed):
        for cc in range(OUT_TILE // 128):
            _store_row_chunk(o_ref, rs.start, (j - n_ff) * (OUT_TILE // 128) + cc, packed[:, cc * 128:(cc + 1) * 128])

    @pl.when((j < n_ff) & full)
    def _():
        hidden(whole, w1g_ref[...].astype(BF16), w1l_ref[...].astype(BF16))

    @pl.when((j < n_ff) & jnp.logical_not(full))
    def _():
        w1g = w1g_ref[...].astype(BF16)
        w1l = w1l_ref[...].astype(BF16)
        for sb, rs in enumerate(subs):
            pl.when(sb * ROW_SUB < rows_valid)(functools.partial(hidden, rs, w1g, w1l))

    @pl.when((j >= n_ff) & full)
    def _():
        project(whole, w2a_ref[...].astype(BF16), w2b_ref[...].astype(BF16))

    @pl.when((j >= n_ff) & jnp.logical_not(full))
    def _():
        w2a = w2a_ref[...].astype(BF16)
        w2b = w2b_ref[...].astype(BF16)
        for sb, rs in enumerate(subs):
            pl.when(sb * ROW_SUB < rows_valid)(functools.partial(project, rs, w2a, w2b))

            @pl.when(sb * ROW_SUB >= rows_valid)
            def _():
                write_out(rs, jnp.zeros((ROW_SUB, OUT_TILE), o_ref.dtype))


def _expert_ffn(xs, w1, b1, w2, b2, block_expert, block_rows, n_valid, n_blocks):
    d = xs.shape[1]
    assert d // 2 == ROW_TILES * 128
    n_exp, _, ff2 = w1.shape
    ff = ff2 // 2
    n_ff = ff // FF_TILE
    n_out = d // 2 // OUT_TILE

    def hid(i, j, nv):
        return jnp.where(i < nv[0], jnp.minimum(j, n_ff - 1), n_ff - 1)

    def rows_block(i, j):
        return jnp.where(j >= n_ff, jnp.minimum(i + 1, n_blocks - 1), i)

    def w2_expert(i, j, be):
        return jnp.where(j < n_ff, be[jnp.maximum(i - 1, 0)], be[i])

    def w2_tile(i, j, nv):
        return jnp.where((i < nv[0]) & (j >= n_ff), j - n_ff, n_out - 1)

    grid_spec = pltpu.PrefetchScalarGridSpec(
        num_scalar_prefetch=3,
        grid=(n_blocks, n_ff + n_out),
        in_specs=[pl.BlockSpec((ROW_BLOCK, d), lambda i, j, nv, be, br: (rows_block(i, j), 0)),
                  pl.BlockSpec((None, d, FF_TILE), lambda i, j, nv, be, br: (be[i], 0, hid(i, j, nv))),
                  pl.BlockSpec((None, d, FF_TILE), lambda i, j, nv, be, br: (be[i], 0, n_ff + hid(i, j, nv))),
                  pl.BlockSpec((None, 1, FF_TILE), lambda i, j, nv, be, br: (be[i], 0, hid(i, j, nv))),
                  pl.BlockSpec((None, 1, FF_TILE), lambda i, j, nv, be, br: (be[i], 0, n_ff + hid(i, j, nv))),
                  pl.BlockSpec((None, ff, OUT_TILE),
                               lambda i, j, nv, be, br: (w2_expert(i, j, be), 0, w2_tile(i, j, nv))),
                  pl.BlockSpec((None, ff, OUT_TILE),
                               lambda i, j, nv, be, br: (w2_expert(i, j, be), 0, n_out + w2_tile(i, j, nv))),
                  pl.BlockSpec((None, 1, OUT_TILE),
                               lambda i, j, nv, be, br: (w2_expert(i, j, be), 0, w2_tile(i, j, nv))),
                  pl.BlockSpec((None, 1, OUT_TILE),
                               lambda i, j, nv, be, br: (w2_expert(i, j, be), 0, n_out + w2_tile(i, j, nv)))],
        out_specs=pl.BlockSpec((ROW_BLOCK * ROW_TILES, 128), lambda i, j, nv, be, br: (i, 0)),
        scratch_shapes=[pltpu.VMEM((n_ff, ROW_BLOCK, FF_TILE), BF16)],
    )
    b1r = b1.reshape(n_exp, 1, ff2)
    b2r = b2.reshape(n_exp, 1, d)
    return pl.pallas_call(
        _ffn_kernel,
        grid_spec=grid_spec,
        out_shape=jax.ShapeDtypeStruct((n_blocks * ROW_BLOCK * ROW_TILES, 128), jnp.uint32),
        compiler_params=_cparams(("arbitrary", "arbitrary")),
        name="moe_ffn",
    )(n_valid, block_expert, block_rows, xs, w1, w1, b1r, b1r, w2, w2, b2r, b2r)


def _combine_kernel(pos_ref, pos_next_ref, x1_ref, gate_ref, nf_ref, ys_hbm, o_ref, buf, sems):
    s = pl.program_id(0)
    tc = x1_ref.shape[0]
    slot = s % 2

    @pl.when(s == 0)
    def _():
        _start_row_gather(ys_hbm, pos_ref, buf.at[0], sems.at[0])

    @pl.when(s + 1 < pl.num_programs(0))
    def _():
        _start_row_gather(ys_hbm, pos_next_ref, buf.at[1 - slot], sems.at[1 - slot])

    _wait_row_gather(buf.at[slot], sems.at[slot])
    half = x1_ref.shape[1] // 2
    gate = gate_ref[...]
    gates = [gate[:, k:k + 1] for k in range(TOP_K)]
    accs = []
    ssq = jnp.zeros((tc, 1), F32)
    for j in range(ROW_TILES):
        lo_cols = slice(j * 128, (j + 1) * 128)
        hi_cols = slice(half + j * 128, half + (j + 1) * 128)
        acc_lo = x1_ref[:, lo_cols]
        acc_hi = x1_ref[:, hi_cols]
        for k in range(TOP_K):
            lo, hi = _unpack_bf16_pair(_load_row_chunk(buf, slot, k * tc, tc, j))
            acc_lo = acc_lo + gates[k] * lo
            acc_hi = acc_hi + gates[k] * hi
        accs += [(lo_cols, acc_lo), (hi_cols, acc_hi)]
        ssq = ssq + jnp.sum(acc_lo * acc_lo, axis=-1, keepdims=True) + jnp.sum(acc_hi * acc_hi, axis=-1, keepdims=True)
    scale = lax.rsqrt(ssq * (1.0 / x1_ref.shape[1]) + NORM_EPS)
    for cols, acc in accs:
        o_ref[:, cols] = acc * scale * nf_ref[:, cols]


def _combine(x1, gate, pos, ys, normf_g, tc):
    n, d = x1.shape
    nt = n // tc
    return pl.pallas_call(
        _combine_kernel,
        grid=(nt,),
        in_specs=[pl.BlockSpec((1, 1, TOP_K * tc), lambda i: (i, 0, 0), memory_space=pltpu.SMEM),
                  pl.BlockSpec((1, 1, TOP_K * tc), lambda i: (jnp.minimum(i + 1, nt - 1), 0, 0),
                               memory_space=pltpu.SMEM),
                  pl.BlockSpec((tc, d), lambda i: (i, 0)), pl.BlockSpec((tc, 128), lambda i: (i, 0)),
                  _resident((1, d)), pl.BlockSpec(memory_space=pl.ANY)],
        out_specs=pl.BlockSpec((tc, d), lambda i: (i, 0)),
        out_shape=jax.ShapeDtypeStruct((n, d), F32),
        scratch_shapes=[pltpu.VMEM((2, TOP_K * tc * ROW_TILES, 128), jnp.uint32), pltpu.SemaphoreType.DMA((2,))],
        compiler_params=_cparams(("arbitrary",)),
        name="moe_combine",
    )(pos, pos, x1, gate, normf_g, ys)


def _route(top_idx, n_exp, tc):
    n = top_idx.shape[0]
    n_assign = n * TOP_K
    n_blocks = n_assign // ROW_BLOCK + n_exp
    tile = min(1024, n)
    hot = (top_idx[:, :, None] == jnp.arange(n_exp, dtype=jnp.int32)).any(axis=1).reshape(n // tile, tile, n_exp)
    earlier = jnp.asarray(np.tril(np.ones((tile, tile), np.float32), -1), BF16)
    within = jnp.einsum("ts,bse->bte", earlier, hot.astype(BF16), preferred_element_type=F32)
    tile_tot = jnp.sum(hot, axis=1, dtype=jnp.int32)
    tile_off = jnp.cumsum(tile_tot, axis=0) - tile_tot
    rank = (within.astype(jnp.int32) + tile_off[:, None, :]).reshape(n, n_exp)
    counts = jnp.sum(tile_tot, axis=0)
    padded = (counts + ROW_BLOCK - 1) // ROW_BLOCK * ROW_BLOCK
    pad_end = jnp.cumsum(padded)
    pad_start = pad_end - padded
    grp_start = jnp.cumsum(counts) - counts
    slot = jnp.take_along_axis(rank + pad_start[None, :], top_idx, axis=1).reshape(-1)
    block_start = jnp.arange(n_blocks, dtype=jnp.int32) * ROW_BLOCK
    block_expert = jnp.minimum(jnp.sum(pad_end[None, :] <= block_start[:, None], axis=1), n_exp - 1).astype(jnp.int32)
    n_valid = (pad_end[-1] // ROW_BLOCK).astype(jnp.int32).reshape(1)
    order = jnp.argsort(top_idx.reshape(-1), stable=True)
    row_e = jnp.repeat(block_expert, ROW_BLOCK)
    row_r = jnp.arange(n_blocks * ROW_BLOCK, dtype=jnp.int32) - pad_start[row_e]
    src = jnp.clip(grp_start[row_e] + row_r, 0, n_assign - 1)
    row_tok = jnp.where((row_r >= 0) & (row_r < counts[row_e]), order[src] // TOP_K, 0).astype(jnp.int32)
    pos = slot.reshape(n // tc, tc, TOP_K).transpose(0, 2, 1).reshape(n // tc, 1, TOP_K * tc)
    block_rows = jnp.clip(counts[block_expert] - (block_start - pad_start[block_expert]), 0, ROW_BLOCK)
    block_rows = jnp.where(jnp.arange(n_blocks) < n_valid[0], block_rows, 0).astype(jnp.int32)
    return row_tok, block_expert, block_rows, n_valid, pos, n_blocks


def _layer(x, norm1_g, w_in, mix_mu, w0, w_decay_up, a0, w_iclr_up, w_gate_up, k_k, k_a, r_k, lnx_g, lnx_b, b_qkv,
           sinks, w_up_rwkv, w_up_attn, w_out, norm2_g, w_router, b_router, w1, b1, w2, b2, normf_g):
    b, s, d = x.shape
    n = b * s
    c = w_up_rwkv.shape[0]
    lora = w_decay_up.shape[0]
    qc = w_up_attn.shape[0]
    kvc = KV_HEADS * HEAD
    rwkv_cols = 3 * c + 2 * lora + w_gate_up.shape[0]
    qkv_cols = qc + 2 * kvc
    row = lambda t: t.reshape(1, -1).astype(F32)
    xf = x.reshape(n, d)
    g1 = row(norm1_g)
    w_in_b = w_in.astype(BF16)

    qkv = _norm_proj(xf, g1, w_in_b[:, rwkv_cols:rwkv_cols + qkv_cols], row(b_qkv), BF16, 512)
    gates = _norm_proj(xf, g1, w_in_b[:, rwkv_cols + qkv_cols:], jnp.zeros((1, 2 * d), F32), BF16, 512)

    zl = jnp.zeros((lora, c), F32)
    w_lora = jnp.concatenate([jnp.concatenate([w_decay_up, zl], axis=1), jnp.concatenate([zl, w_iclr_up], axis=1)],
                             axis=0)
    r, lw, kf, v, kn, ba, g = _rwkv_prep(xf, g1, w_in_b[:, :rwkv_cols], s, c, row(mix_mu), w_lora, w_gate_up, row(w0),
                                         row(a0), row(k_k), row(k_a), 256)
    as3 = lambda t: t.reshape(b, s, c)
    y_rwkv = _rwkv_scan(as3(r), as3(lw), as3(kf), as3(v), as3(kn), as3(ba), as3(g), row(r_k), row(lnx_g),
                        row(lnx_b)).reshape(n, c)

    q = qkv[:, :qc].reshape(b, s, qc)
    ka = qkv[:, qc:qc + kvc].reshape(b, s, kvc)
    va = qkv[:, qc + kvc:].reshape(b, s, kvc)
    sinks_b = jnp.broadcast_to(sinks.astype(F32).reshape(-1, 1), (sinks.shape[0], 128))
    y_attn = _attention(q, ka, va, sinks_b).reshape(n, qc)

    x1, h2, gate, top_idx = _merge(xf, y_rwkv, y_attn, gates, w_up_rwkv.astype(BF16), w_up_attn.astype(BF16),
                                   w_out.astype(BF16), row(norm2_g), w_router, row(b_router), 256)

    tc = 256
    row_tok, block_expert, block_rows, n_valid, pos, n_blocks = _route(top_idx[:, :TOP_K], w_router.shape[1], tc)
    xs = _dispatch(h2.reshape(n, ROW_TILES, 128), row_tok, block_rows, n_blocks)
    ys = _expert_ffn(xs, w1, b1, w2, b2, block_expert, block_rows, n_valid, n_blocks)
    out = _combine(x1, gate, pos, ys.reshape(-1, ROW_TILES, 128), row(normf_g), tc)
    return out.reshape(b, s, d)


def kernel(x, norm1_g, w_in, mix_mu, w0, w_decay_up, a0, w_iclr_up, w_gate_up, k_k, k_a, r_k, lnx_g, lnx_b, b_qkv,
           sinks, w_up_rwkv, w_up_attn, w_out, norm2_g, w_router, b_router, w1, b1, w2, b2, normf_g):
    assert w_in.shape[0] == 1, "single-layer block"
    return _layer(x, norm1_g[0], w_in[0], mix_mu[0], w0[0], w_decay_up[0], a0[0], w_iclr_up[0], w_gate_up[0],
                  k_k[0], k_a[0], r_k[0], lnx_g[0], lnx_b[0], b_qkv[0], sinks[0], w_up_rwkv[0], w_up_attn[0],
                  w_out[0], norm2_g[0], w_router[0], b_router[0], w1[0], b1[0], w2[0], b2[0], normf_g)
```

```python
import functools

import jax
import jax.numpy as jnp
import numpy as np
from jax import lax
from jax.experimental import pallas as pl
from jax.experimental.pallas import tpu as pltpu

F32 = jnp.float32
BF16 = jnp.bfloat16

NORM_EPS = 1e-5
LNX_EPS = 64e-5
HEAD = 64
WINDOW = 128
KV_HEADS = 4
TOP_K = 4
SWIGLU_LIMIT = 7.0
SWIGLU_ALPHA = 1.702
DECAY_SCALE = float(np.exp(-0.5))

CHUNK = 64
ROW_BLOCK = 1152
ROW_SUB = 384
FF_TILE = 512
OUT_TILE = 256
VMEM_LIMIT = 56 * 1024 * 1024


def _cparams(sem):
    return pltpu.CompilerParams(dimension_semantics=sem, vmem_limit_bytes=VMEM_LIMIT)


def _resident(shape):
    nd = len(shape)
    return pl.BlockSpec(shape, lambda *_: (0,) * nd, pipeline_mode=pl.Buffered(1))


def _rmsnorm(x, g):
    return x * lax.rsqrt(jnp.mean(x * x, axis=-1, keepdims=True) + NORM_EPS) * g


def _sigmoid(x):
    return 1.0 / (1.0 + jnp.exp(-x))


def _pack_bf16_pair(lo, hi):
    lo_bits = lax.bitcast_convert_type(lo.astype(BF16).astype(F32), jnp.uint32)
    hi_bits = lax.bitcast_convert_type(hi.astype(BF16).astype(F32), jnp.uint32)
    return (hi_bits & jnp.uint32(0xFFFF0000)) | (lo_bits >> 16)


ROW_TILES = 8


def _store_row_chunk(ref, row0, chunk, value):
    ref[pl.ds(row0 * ROW_TILES + chunk, value.shape[0], stride=ROW_TILES), :] = value


def _store_row_tiles(ref, row0, packed):
    for j in range(ROW_TILES):
        _store_row_chunk(ref, row0, j, packed[:, j * 128:(j + 1) * 128])


def _unpack_bf16_pair(packed):
    lo = lax.bitcast_convert_type(packed << 16, F32)
    hi = lax.bitcast_convert_type(packed & jnp.uint32(0xFFFF0000), F32)
    return lo, hi


def _norm_proj_kernel(x_ref, g_ref, w_ref, b_ref, o_ref):
    h = _rmsnorm(x_ref[...], g_ref[...]).astype(BF16)
    z = jnp.dot(h, w_ref[...], preferred_element_type=F32) + b_ref[...]
    o_ref[...] = z.astype(o_ref.dtype)


def _norm_proj(x, g, w, b, out_dtype, tm):
    n, d = x.shape
    cols = w.shape[1]
    return pl.pallas_call(
        _norm_proj_kernel,
        grid=(n // tm,),
        in_specs=[pl.BlockSpec((tm, d), lambda i: (i, 0)), _resident((1, d)), _resident((d, cols)),
                  _resident((1, cols))],
        out_specs=pl.BlockSpec((tm, cols), lambda i: (i, 0)),
        out_shape=jax.ShapeDtypeStruct((n, cols), out_dtype),
        compiler_params=_cparams(("parallel",)),
        name="norm_proj",
    )(x, g, w, b)


def _head_sums(x):
    rows, c = x.shape
    lane = lax.broadcasted_iota(jnp.int32, (rows, 128), 1)
    low = lane < HEAD
    parts = []
    for gi in range(c // 128):
        xg = x[:, gi * 128:(gi + 1) * 128]
        s_lo = jnp.sum(jnp.where(low, xg, 0.0), axis=-1, keepdims=True)
        s_hi = jnp.sum(jnp.where(low, 0.0, xg), axis=-1, keepdims=True)
        parts.append(jnp.where(low, s_lo, s_hi))
    return jnp.concatenate(parts, axis=-1)


def _dot_bf16x3(a, w):
    a_hi = a.astype(BF16)
    a_lo = (a - a_hi.astype(F32)).astype(BF16)
    w_hi = w.astype(BF16)
    w_lo = (w - w_hi.astype(F32)).astype(BF16)
    return (jnp.dot(a_hi, w_hi, preferred_element_type=F32) + jnp.dot(a_lo, w_hi, preferred_element_type=F32)
            + jnp.dot(a_hi, w_lo, preferred_element_type=F32))


def _prep_kernel(seq_blocks, c, x_ref, n1_ref, win_ref, mu_ref, wlora_ref, wgate_ref, w0_ref, a0_ref, kk_ref, ka_ref,
                 r_ref, lw_ref, kf_ref, v_ref, kn_ref, ba_ref, g_ref, last_ref):
    i = pl.program_id(0)

    @pl.when(i == 0)
    def _():
        last_ref[...] = jnp.zeros_like(last_ref)

    h = _rmsnorm(x_ref[...], n1_ref[...]).astype(BF16)
    z = jnp.dot(h, win_ref[...], preferred_element_type=F32)
    tm = z.shape[0]
    prev = jnp.where(i % seq_blocks == 0, 0.0, last_ref[0:1, :])
    last_ref[0:1, :] = z[tm - 1:tm, :]
    row = lax.broadcasted_iota(jnp.int32, z.shape, 0)
    shifted = jnp.where(row == 0, prev, pltpu.roll(z, 1, 0))
    zs = z + (shifted - z) * mu_ref[...]
    r = zs[:, 0:c]
    k = zs[:, c:2 * c]
    v = zs[:, 2 * c:3 * c]
    zwa = zs[:, 3 * c:3 * c + 128]
    zg = zs[:, 3 * c + 128:3 * c + 256]
    lane = lax.broadcasted_iota(jnp.int32, (tm, 128), 1)
    lora_in = jnp.where(lane < 64, jnp.tanh(zwa), zwa)
    up = _dot_bf16x3(lora_in, wlora_ref[...])
    u = w0_ref[...] + up[:, 0:c]
    a = _sigmoid(a0_ref[...] + up[:, c:2 * c])
    g = _dot_bf16x3(_sigmoid(zg), wgate_ref[...])
    lw = -DECAY_SCALE * _sigmoid(u)
    kk = k * kk_ref[...]
    kn = kk / jnp.maximum(jnp.sqrt(_head_sums(kk * kk)), 1e-12)
    kf = k * (1.0 + (a - 1.0) * ka_ref[...])
    r_ref[...] = r
    lw_ref[...] = lw
    kf_ref[...] = kf
    v_ref[...] = v
    kn_ref[...] = kn
    ba_ref[...] = kn * a
    g_ref[...] = g


def _rwkv_prep(x, norm_g, w_in_rwkv, seq, c, mix_mu, w_lora, w_gate, w0, a0, k_k, k_a, tm):
    n, d = x.shape
    zc = w_in_rwkv.shape[1]
    row_spec = pl.BlockSpec((tm, c), lambda i: (i, 0))
    out = jax.ShapeDtypeStruct((n, c), F32)
    return pl.pallas_call(
        functools.partial(_prep_kernel, seq // tm, c),
        grid=(n // tm,),
        in_specs=[pl.BlockSpec((tm, d), lambda i: (i, 0)), _resident((1, d)), _resident((d, zc)),
                  _resident((1, zc)), _resident((128, 2 * c)), _resident((128, c)),
                  _resident((1, c)), _resident((1, c)), _resident((1, c)), _resident((1, c))],
        out_specs=[row_spec] * 7,
        out_shape=[out] * 7,
        scratch_shapes=[pltpu.VMEM((8, zc), F32)],
        compiler_params=_cparams(("arbitrary",)),
        name="rwkv_prep",
    )(x, norm_g, w_in_rwkv, mix_mu, w_lora, w_gate, w0, a0, k_k, k_a)


def _split3(x):
    h1 = x.astype(BF16)
    r1 = x - h1.astype(F32)
    h2 = r1.astype(BF16)
    h3 = (r1 - h2.astype(F32)).astype(BF16)
    return h1, h2, h3


def _mm(a, b):
    return jnp.dot(a, b, preferred_element_type=F32)


def _mm_nt(a, b):
    return lax.dot_general(a, b, (((1,), (1,)), ((), ())), preferred_element_type=F32)


def _mm_tn(a, b):
    return lax.dot_general(a, b, (((0,), (0,)), ((), ())), preferred_element_type=F32)


GROUP_HEADS = 4
GROUP_LANES = GROUP_HEADS * HEAD


def _block_diag(x, mask):
    return jnp.concatenate([x.astype(BF16)] * GROUP_HEADS, axis=0) * mask


def _scan_kernel(r_ref, lw_ref, kf_ref, v_ref, kn_ref, ba_ref, g_ref, rk_ref, lng_ref, lnb_ref, mask_ref, y_ref,
                 s_ref):
    @pl.when(pl.program_id(0) == 0)
    def _():
        s_ref[...] = jnp.zeros_like(s_ref)

    t = CHUNK
    nb = r_ref.shape[0]
    c = r_ref.shape[2]
    gl = GROUP_LANES
    n_groups = c // gl
    mask = mask_ref[...]
    ti = lax.broadcasted_iota(jnp.int32, (t, t), 0)
    si = lax.broadcasted_iota(jnp.int32, (t, t), 1)
    tri = jnp.where(si <= ti, 1.0, 0.0).astype(BF16)
    row = lax.broadcasted_iota(jnp.int32, (t, gl), 0)
    col = lax.broadcasted_iota(jnp.int32, (t, gl), 1) & (HEAD - 1)
    strict = col < row
    incl = col <= row
    eye = jnp.where(col == row, 1.0, 0.0)

    prep = []
    for b in range(nb):
        lw = lw_ref[b]
        cum = sum(jnp.dot(tri, part, preferred_element_type=F32) for part in _split3(lw))
        cum_end = cum[t - 1:t, :]
        e_pos = jnp.exp(cum)
        e_neg = jnp.exp(-cum)
        e_end = jnp.exp(cum_end - cum)
        r = r_ref[b]
        kf = kf_ref[b]
        ba = ba_ref[b]
        prep.append(dict(
            rt=r * e_pos,
            at=-kn_ref[b] * jnp.exp(cum - lw),
            bt=ba * e_neg,
            kt=kf * e_neg,
            bh=ba * e_end,
            kh=kf * e_end,
            v=v_ref[b],
            w_end=jnp.exp(cum_end),
            rkf=r * kf * rk_ref[...]))

    probs = [(b, gi) for b in range(nb) for gi in range(n_groups)]

    def part(b, gi, name):
        return prep[b][name][:, gi * gl:(gi + 1) * gl]

    a_ab, a_ak, a_rb, a_rk = [], [], [], []
    for b, gi in probs:
        lhs = jnp.concatenate([part(b, gi, "at"), part(b, gi, "rt")], axis=0).astype(BF16)
        rhs = jnp.concatenate([_block_diag(part(b, gi, "bt"), mask), _block_diag(part(b, gi, "kt"), mask)], axis=0)
        amat = _mm_nt(lhs, rhs)
        a_ab.append(jnp.where(strict, amat[0:t, 0:gl], 0.0))
        a_ak.append(jnp.where(strict, amat[0:t, gl:2 * gl], 0.0))
        a_rb.append(jnp.where(incl, amat[t:2 * t, 0:gl], 0.0))
        a_rk.append(jnp.where(incl, amat[t:2 * t, gl:2 * gl], 0.0))

    vbd = [_block_diag(part(b, gi, "v"), mask) for b, gi in probs]
    av = [_mm(a_ak[i].astype(BF16), vbd[i]) for i in range(len(probs))]
    minv = [eye + a for a in a_ab]
    power = [_mm(a.astype(BF16), _block_diag(a, mask)) for a in a_ab]
    span = 2
    while span < t:
        last = span * 2 >= t
        for i in range(len(probs)):
            pbd = _block_diag(power[i], mask)
            if last:
                minv[i] = minv[i] + _mm(minv[i].astype(BF16), pbd)
            else:
                both = _mm(jnp.concatenate([power[i], minv[i]], axis=0).astype(BF16), pbd)
                power[i] = both[0:t]
                minv[i] = minv[i] + both[t:2 * t]
        span *= 2

    s0 = [s_ref[b, gi] for b, gi in probs]
    sbd = [_block_diag(s, mask) for s in s0]
    ps = []
    for i, (b, gi) in enumerate(probs):
        lhs = jnp.concatenate([part(b, gi, "at"), part(b, gi, "rt")], axis=0).astype(BF16)
        ps.append(_mm_nt(lhs, sbd[i]))
    u = [_mm(minv[i].astype(BF16), _block_diag(ps[i][0:t] + av[i], mask)) for i in range(len(probs))]
    ys = []
    for i, (b, gi) in enumerate(probs):
        lhs = jnp.concatenate([a_rb[i], a_rk[i]], axis=1).astype(BF16)
        rhs = jnp.concatenate([_block_diag(u[i], mask), vbd[i]], axis=0)
        ys.append(ps[i][t:2 * t] + _mm(lhs, rhs))
        uv = jnp.concatenate([u[i], part(b, gi, "v")], axis=0).astype(BF16)
        bk = jnp.concatenate([part(b, gi, "bh"), part(b, gi, "kh")], axis=0).astype(BF16)
        full = _mm_tn(uv, bk) * mask.astype(F32)
        upd = full[0:HEAD]
        for hh in range(1, GROUP_HEADS):
            upd = upd + full[hh * HEAD:(hh + 1) * HEAD]
        s_ref[b, gi] = s0[i] * part(b, gi, "w_end") + upd

    for b in range(nb):
        y = jnp.concatenate([ys[b * n_groups + gi] for gi in range(n_groups)], axis=1)
        mu = _head_sums(y) * (1.0 / HEAD)
        yc = y - mu
        var = _head_sums(yc * yc) * (1.0 / HEAD)
        yn = yc * lax.rsqrt(var + LNX_EPS)
        bonus = _head_sums(prep[b]["rkf"]) * prep[b]["v"]
        y_ref[b] = ((yn * lng_ref[...] + lnb_ref[...] + bonus) * g_ref[b]).astype(y_ref.dtype)


def _rwkv_scan(r, lw, kf, v, kn, ba, g, r_k, lnx_g, lnx_b):
    b, s, c = r.shape
    blk = pl.BlockSpec((b, CHUNK, c), lambda ci: (0, ci, 0))
    hid = np.arange(GROUP_LANES) // HEAD
    mask = jnp.asarray(hid[:, None] == hid[None, :], BF16)
    return pl.pallas_call(
        _scan_kernel,
        grid=(s // CHUNK,),
        in_specs=[blk] * 7 + [_resident((1, c))] * 3 + [_resident((GROUP_LANES, GROUP_LANES))],
        out_specs=blk,
        out_shape=jax.ShapeDtypeStruct((b, s, c), BF16),
        scratch_shapes=[pltpu.VMEM((b, c // GROUP_LANES, HEAD, GROUP_LANES), F32)],
        compiler_params=_cparams(("arbitrary",)),
        name="rwkv_scan",
    )(r, lw, kf, v, kn, ba, g, r_k, lnx_g, lnx_b, mask)


def _attn_kernel(q_ref, kc_ref, kp_ref, vc_ref, vp_ref, sink_ref, o_ref):
    first = pl.program_id(1) == 0
    w = WINDOW
    group = q_ref.shape[-1] // HEAD // KV_HEADS
    row = lax.broadcasted_iota(jnp.int32, (group * w, 2 * w), 0)
    qi = row & (w - 1)
    kj = lax.broadcasted_iota(jnp.int32, (group * w, 2 * w), 1)
    lo = jnp.where(first, jnp.maximum(qi, w - 1), qi)
    valid = (kj > lo) & (kj <= qi + w)
    grow = lax.broadcasted_iota(jnp.int32, (group * w, 1), 0) // w
    q = q_ref[...]
    outs = []
    for hk in range(KV_HEADS):
        ksl = slice(hk * HEAD, (hk + 1) * HEAD)
        kcat = jnp.concatenate([kp_ref[:, ksl], kc_ref[:, ksl]], axis=0)
        vcat = jnp.concatenate([vp_ref[:, ksl], vc_ref[:, ksl]], axis=0)
        qg = jnp.concatenate([q[:, (hk * group + gi) * HEAD:(hk * group + gi + 1) * HEAD] for gi in range(group)],
                             axis=0)
        sink = jnp.zeros((group * w, 1), F32)
        for gi in range(group):
            sink = jnp.where(grow == gi, sink_ref[hk * group + gi:hk * group + gi + 1, 0:1], sink)
        s = _mm_nt(qg, kcat) * (HEAD ** -0.5)
        s = jnp.where(valid, s, -1e30)
        m = jnp.maximum(jnp.max(s, axis=-1, keepdims=True), sink)
        p = jnp.exp(s - m)
        pb = p.astype(BF16)
        psum = _mm(pb, jnp.ones((2 * w, HEAD), BF16))
        o = _mm(pb, vcat) / (psum + jnp.exp(sink - m))
        for gi in range(group):
            outs.append(o[gi * w:(gi + 1) * w, :])
    o_ref[...] = jnp.concatenate(outs, axis=-1).astype(o_ref.dtype)


def _attention(q, k, v, sinks_b):
    b, s, qc = q.shape
    kc = k.shape[-1]
    nb = s // WINDOW
    cur = lambda bi, i: (bi, i, 0)
    prev = lambda bi, i: (bi, jnp.maximum(i - 1, 0), 0)
    return pl.pallas_call(
        _attn_kernel,
        grid=(b, nb),
        in_specs=[pl.BlockSpec((None, WINDOW, qc), cur),
                  pl.BlockSpec((None, WINDOW, kc), cur), pl.BlockSpec((None, WINDOW, kc), prev),
                  pl.BlockSpec((None, WINDOW, kc), cur), pl.BlockSpec((None, WINDOW, kc), prev),
                  _resident(sinks_b.shape)],
        out_specs=pl.BlockSpec((None, WINDOW, qc), cur),
        out_shape=jax.ShapeDtypeStruct((b, s, qc), BF16),
        compiler_params=_cparams(("parallel", "parallel")),
        name="swa_attention",
    )(q, k, k, v, v, sinks_b)


MERGE_SPLIT = 2


def _merge_kernel(n_exp, x_ref, yr_ref, ya_ref, gr_ref, ga_ref, wur_ref, wua_ref, wo_ref, n2_ref, wr_ref, br_ref,
                  x1_ref, h2_ref, gate_ref, idx_ref):
    tm = x_ref.shape[0] // MERGE_SPLIT
    parts = [slice(p * tm, (p + 1) * tm) for p in range(MERGE_SPLIT)]
    ups = [(_mm(yr_ref[rs, :], wur_ref[...]), _mm(ya_ref[rs, :], wua_ref[...])) for rs in parts]
    merged = [_sigmoid(gr_ref[rs, :].astype(F32)) * ur + _sigmoid(ga_ref[rs, :].astype(F32)) * ua
              for rs, (ur, ua) in zip(parts, ups)]
    x1s = [x_ref[rs, :] + _mm(m.astype(BF16), wo_ref[...]) for rs, m in zip(parts, merged)]
    h2s = []
    for rs, x1 in zip(parts, x1s):
        x1_ref[rs, :] = x1
        h2 = _rmsnorm(x1, n2_ref[...])
        half = h2.shape[1] // 2
        _store_row_tiles(h2_ref, rs.start, _pack_bf16_pair(h2[:, :half], h2[:, half:]))
        h2s.append(h2)
    all_logits = [_dot_bf16x3(h2, wr_ref[...]) + br_ref[...] for h2 in h2s]
    lane_e = lax.broadcasted_iota(jnp.int32, (tm, n_exp), 1).astype(F32)
    lane_o = lax.broadcasted_iota(jnp.int32, (tm, 128), 1)
    for rs, logits in zip(parts, all_logits):
        vals, idxs = [], []
        for _ in range(TOP_K):
            mx = jnp.max(logits, axis=-1, keepdims=True)
            ix = jnp.min(jnp.where(logits == mx, lane_e, float(n_exp)), axis=-1, keepdims=True)
            vals.append(mx)
            idxs.append(ix)
            logits = jnp.where(lane_e == ix, -jnp.inf, logits)
        exps = [jnp.exp(vk - vals[0]) for vk in vals]
        tot = exps[0] + exps[1] + exps[2] + exps[3]
        gate_o, idx_o = jnp.zeros((tm, 128), F32), jnp.zeros((tm, 128), F32)
        for kk in range(TOP_K):
            gate_o = jnp.where(lane_o == kk, exps[kk] / tot, gate_o)
            idx_o = jnp.where(lane_o == kk, idxs[kk], idx_o)
        gate_ref[rs, :] = gate_o
        idx_ref[rs, :] = idx_o.astype(jnp.int32)


def _merge(x, y_rwkv, y_attn, gates, w_up_r, w_up_a, w_out, norm2_g, w_router, b_router, tm):
    n, d = x.shape
    c = y_rwkv.shape[1]
    n_exp = w_router.shape[1]
    row = lambda cols, j=0: pl.BlockSpec((tm, cols), lambda i: (i, j))
    return pl.pallas_call(
        functools.partial(_merge_kernel, n_exp),
        grid=(n // tm,),
        in_specs=[row(d), row(c), row(c), row(d, 0), row(d, 1),
                  _resident((c, d)), _resident((c, d)), _resident((d, d)), _resident((1, d)),
                  _resident((d, n_exp)), _resident((1, n_exp))],
        out_specs=[row(d), pl.BlockSpec((tm * ROW_TILES, 128), lambda i: (i, 0)), row(128), row(128)],
        out_shape=[jax.ShapeDtypeStruct((n, d), F32), jax.ShapeDtypeStruct((n * ROW_TILES, 128), jnp.uint32),
                   jax.ShapeDtypeStruct((n, 128), F32), jax.ShapeDtypeStruct((n, 128), jnp.int32)],
        compiler_params=_cparams(("parallel",)),
        name="merge_router",
    )(x, y_rwkv, y_attn, gates, gates, w_up_r, w_up_a, w_out, norm2_g, w_router, b_router)


GATHER_UNROLL = 16


def _start_row_gather(src_hbm, idx_ref, buf, sem):
    rows = buf.shape[0] // ROW_TILES

    def start(g, carry):
        for u in range(GATHER_UNROLL):
            r = g * GATHER_UNROLL + u
            dst = buf.at[pl.ds(pl.multiple_of(r * ROW_TILES, ROW_TILES), ROW_TILES)]
            pltpu.make_async_copy(src_hbm.at[idx_ref[0, 0, r]], dst, sem).start(priority=u % 2)
        return carry

    lax.fori_loop(0, rows // GATHER_UNROLL, start, 0)


def _load_row_chunk(buf, slot, row0, rows, chunk):
    return buf[slot, pl.ds(row0 * ROW_TILES + chunk, rows, stride=ROW_TILES), :]


def _wait_row_gather(buf, sem):
    pltpu.make_async_copy(buf, buf, sem).wait()


def _dispatch_kernel(brows_ref, tok_ref, tok_next_ref, h_hbm, o_ref, buf, sems):
    s = pl.program_id(0)
    n_sub = ROW_BLOCK // ROW_SUB

    def has_rows(t):
        return lax.rem(t, n_sub) * ROW_SUB < brows_ref[lax.div(t, n_sub)]

    slot = s % 2

    @pl.when((s == 0) & has_rows(0))
    def _():
        _start_row_gather(h_hbm, tok_ref, buf.at[0], sems.at[0])

    @pl.when((s + 1 < pl.num_programs(0)) & has_rows(jnp.minimum(s + 1, pl.num_programs(0) - 1)))
    def _():
        _start_row_gather(h_hbm, tok_next_ref, buf.at[1 - slot], sems.at[1 - slot])

    @pl.when(has_rows(s))
    def _():
        _wait_row_gather(buf.at[slot], sems.at[slot])
        half = o_ref.shape[1] // 2
        for j in range(ROW_TILES):
            lo, hi = _unpack_bf16_pair(_load_row_chunk(buf, slot, 0, ROW_SUB, j))
            o_ref[:, j * 128:(j + 1) * 128] = lo.astype(o_ref.dtype)
            o_ref[:, half + j * 128:half + (j + 1) * 128] = hi.astype(o_ref.dtype)

    @pl.when(jnp.logical_not(has_rows(s)))
    def _():
        o_ref[...] = jnp.zeros_like(o_ref)


def _dispatch(h2p, row_tok, block_rows, n_blocks):
    d = 2 * ROW_TILES * 128
    steps = n_blocks * (ROW_BLOCK // ROW_SUB)
    grid_spec = pltpu.PrefetchScalarGridSpec(
        num_scalar_prefetch=1,
        grid=(steps,),
        in_specs=[pl.BlockSpec((1, 1, ROW_SUB), lambda s, br: (s, 0, 0), memory_space=pltpu.SMEM),
                  pl.BlockSpec((1, 1, ROW_SUB), lambda s, br: (jnp.minimum(s + 1, steps - 1), 0, 0),
                               memory_space=pltpu.SMEM),
                  pl.BlockSpec(memory_space=pl.ANY)],
        out_specs=pl.BlockSpec((ROW_SUB, d), lambda s, br: (s, 0)),
        scratch_shapes=[pltpu.VMEM((2, ROW_SUB * ROW_TILES, 128), jnp.uint32), pltpu.SemaphoreType.DMA((2,))],
    )
    tok = row_tok.reshape(steps, 1, ROW_SUB)
    return pl.pallas_call(
        _dispatch_kernel,
        grid_spec=grid_spec,
        out_shape=jax.ShapeDtypeStruct((n_blocks * ROW_BLOCK, d), BF16),
        compiler_params=_cparams(("arbitrary",)),
        name="moe_dispatch",
    )(block_rows, tok, tok, h2p)


def _ffn_kernel(nvalid_ref, bexp_ref, brows_ref, xs_ref, w1g_ref, w1l_ref, b1g_ref, b1l_ref, w2a_ref, w2b_ref,
                b2a_ref, b2b_ref, o_ref, act_ref):
    i = pl.program_id(0)
    j = pl.program_id(1)
    n_ff = act_ref.shape[0]
    rows_valid = brows_ref[i]
    n_sub = ROW_BLOCK // ROW_SUB

    full = rows_valid == ROW_BLOCK
    whole = slice(0, ROW_BLOCK)
    subs = [slice(sb * ROW_SUB, (sb + 1) * ROW_SUB) for sb in range(n_sub)]

    def hidden(rs, w1g, w1l):
        x = xs_ref[rs, :]
        hg = _mm(x, w1g) + b1g_ref[...]
        hl = _mm(x, w1l) + b1l_ref[...]
        glu = jnp.minimum(hg, SWIGLU_LIMIT)
        lin = jnp.clip(hl, -SWIGLU_LIMIT, SWIGLU_LIMIT)
        act_ref[j, rs, :] = (glu * _sigmoid(SWIGLU_ALPHA * glu) * (lin + 1.0)).astype(BF16)

    def project(rs, w2a, w2b):
        def half(w2, b2_ref):
            acc = _mm(act_ref[0, rs, :], w2[0:FF_TILE])
            for jf in range(1, n_ff):
                acc = acc + _mm(act_ref[jf, rs, :], w2[jf * FF_TILE:(jf + 1) * FF_TILE])
            return acc + b2_ref[...]

        write_out(rs, _pack_bf16_pair(half(w2a, b2a_ref), half(w2b, b2b_ref)))

    def write_out(rs, packed):
        for cc in range(OUT_TILE // 128):
            _store_row_chunk(o_ref, rs.start, (j - n_ff) * (OUT_TILE // 128) + cc, packed[:, cc * 128:(cc + 1) * 128])

    @pl.when((j < n_ff) & full)
    def _():
        hidden(whole, w1g_ref[...].astype(BF16), w1l_ref[...].astype(BF16))

    @pl.when((j < n_ff) & jnp.logical_not(full))
    def _():
        w1g = w1g_ref[...].astype(BF16)
        w1l = w1l_ref[...].astype(BF16)
        for sb, rs in enumerate(subs):
            pl.when(sb * ROW_SUB < rows_valid)(functools.partial(hidden, rs, w1g, w1l))

    @pl.when((j >= n_ff) & full)
    def _():
        project(whole, w2a_ref[...].astype(BF16), w2b_ref[...].astype(BF16))

    @pl.when((j >= n_ff) & jnp.logical_not(full))
    def _():
        w2a = w2a_ref[...].astype(BF16)
        w2b = w2b_ref[...].astype(BF16)
        for sb, rs in enumerate(subs):
            pl.when(sb * ROW_SUB < rows_valid)(functools.partial(project, rs, w2a, w2b))

            @pl.when(sb * ROW_SUB >= rows_valid)
            def _():
                write_out(rs, jnp.zeros((ROW_SUB, OUT_TILE), o_ref.dtype))


def _expert_ffn(xs, w1, b1, w2, b2, block_expert, block_rows, n_valid, n_blocks):
    d = xs.shape[1]
    assert d // 2 == ROW_TILES * 128
    n_exp, _, ff2 = w1.shape
    ff = ff2 // 2
    n_ff = ff // FF_TILE
    n_out = d // 2 // OUT_TILE

    def hid(i, j, nv):
        return jnp.where(i < nv[0], jnp.minimum(j, n_ff - 1), n_ff - 1)

    def rows_block(i, j):
        return jnp.where(j >= n_ff, jnp.minimum(i + 1, n_blocks - 1), i)

    def w2_expert(i, j, be):
        return jnp.where(j < n_ff, be[jnp.maximum(i - 1, 0)], be[i])

    def w2_tile(i, j, nv):
        return jnp.where((i < nv[0]) & (j >= n_ff), j - n_ff, n_out - 1)

    grid_spec = pltpu.PrefetchScalarGridSpec(
        num_scalar_prefetch=3,
        grid=(n_blocks, n_ff + n_out),
        in_specs=[pl.BlockSpec((ROW_BLOCK, d), lambda i, j, nv, be, br: (rows_block(i, j), 0)),
                  pl.BlockSpec((None, d, FF_TILE), lambda i, j, nv, be, br: (be[i], 0, hid(i, j, nv))),
                  pl.BlockSpec((None, d, FF_TILE), lambda i, j, nv, be, br: (be[i], 0, n_ff + hid(i, j, nv))),
                  pl.BlockSpec((None, 1, FF_TILE), lambda i, j, nv, be, br: (be[i], 0, hid(i, j, nv))),
                  pl.BlockSpec((None, 1, FF_TILE), lambda i, j, nv, be, br: (be[i], 0, n_ff + hid(i, j, nv))),
                  pl.BlockSpec((None, ff, OUT_TILE),
                               lambda i, j, nv, be, br: (w2_expert(i, j, be), 0, w2_tile(i, j, nv))),
                  pl.BlockSpec((None, ff, OUT_TILE),
                               lambda i, j, nv, be, br: (w2_expert(i, j, be), 0, n_out + w2_tile(i, j, nv))),
                  pl.BlockSpec((None, 1, OUT_TILE),
                               lambda i, j, nv, be, br: (w2_expert(i, j, be), 0, w2_tile(i, j, nv))),
                  pl.BlockSpec((None, 1, OUT_TILE),
                               lambda i, j, nv, be, br: (w2_expert(i, j, be), 0, n_out + w2_tile(i, j, nv)))],
        out_specs=pl.BlockSpec((ROW_BLOCK * ROW_TILES, 128), lambda i, j, nv, be, br: (i, 0)),
        scratch_shapes=[pltpu.VMEM((n_ff, ROW_BLOCK, FF_TILE), BF16)],
    )
    b1r = b1.reshape(n_exp, 1, ff2)
    b2r = b2.reshape(n_exp, 1, d)
    return pl.pallas_call(
        _ffn_kernel,
        grid_spec=grid_spec,
        out_shape=jax.ShapeDtypeStruct((n_blocks * ROW_BLOCK * ROW_TILES, 128), jnp.uint32),
        compiler_params=_cparams(("arbitrary", "arbitrary")),
        name="moe_ffn",
    )(n_valid, block_expert, block_rows, xs, w1, w1, b1r, b1r, w2, w2, b2r, b2r)


def _combine_kernel(pos_ref, pos_next_ref, x1_ref, gate_ref, nf_ref, ys_hbm, o_ref, buf, sems):
    s = pl.program_id(0)
    tc = x1_ref.shape[0]
    slot = s % 2
    n_rows = TOP_K * tc
    per_load = n_rows // (ROW_TILES * TOP_K)
    nxt_buf = buf.at[1 - slot]
    nxt_sem = sems.at[1 - slot]

    def start_next(first):
        for r in range(first, first + per_load):
            pltpu.make_async_copy(ys_hbm.at[pos_next_ref[0, 0, r]], nxt_buf.at[pl.ds(r * ROW_TILES, ROW_TILES)],
                                  nxt_sem).start(priority=r % 2)

    @pl.when(s == 0)
    def _():
        _start_row_gather(ys_hbm, pos_ref, buf.at[0], sems.at[0])

    _wait_row_gather(buf.at[slot], sems.at[slot])
    half = x1_ref.shape[1] // 2
    gate = gate_ref[...]
    gates = [gate[:, k:k + 1] for k in range(TOP_K)]
    accs = []
    ssq = jnp.zeros((tc, 1), F32)
    for j in range(ROW_TILES):
        lo_cols = slice(j * 128, (j + 1) * 128)
        hi_cols = slice(half + j * 128, half + (j + 1) * 128)
        acc_lo = x1_ref[:, lo_cols]
        acc_hi = x1_ref[:, hi_cols]
        for k in range(TOP_K):
            start_next((j * TOP_K + k) * per_load)
            lo, hi = _unpack_bf16_pair(_load_row_chunk(buf, slot, k * tc, tc, j))
            acc_lo = acc_lo + gates[k] * lo
            acc_hi = acc_hi + gates[k] * hi
        accs += [(lo_cols, acc_lo), (hi_cols, acc_hi)]
        ssq = ssq + jnp.sum(acc_lo * acc_lo, axis=-1, keepdims=True) + jnp.sum(acc_hi * acc_hi, axis=-1, keepdims=True)
    scale = lax.rsqrt(ssq * (1.0 / x1_ref.shape[1]) + NORM_EPS)
    for cols, acc in accs:
        o_ref[:, cols] = acc * scale * nf_ref[:, cols]

    @pl.when(s + 1 == pl.num_programs(0))
    def _():
        _wait_row_gather(nxt_buf, nxt_sem)


def _combine(x1, gate, pos, ys, normf_g, tc):
    n, d = x1.shape
    nt = n // tc
    return pl.pallas_call(
        _combine_kernel,
        grid=(nt,),
        in_specs=[pl.BlockSpec((1, 1, TOP_K * tc), lambda i: (i, 0, 0), memory_space=pltpu.SMEM),
                  pl.BlockSpec((1, 1, TOP_K * tc), lambda i: (jnp.minimum(i + 1, nt - 1), 0, 0),
                               memory_space=pltpu.SMEM),
                  pl.BlockSpec((tc, d), lambda i: (i, 0)), pl.BlockSpec((tc, 128), lambda i: (i, 0)),
                  _resident((1, d)), pl.BlockSpec(memory_space=pl.ANY)],
        out_specs=pl.BlockSpec((tc, d), lambda i: (i, 0)),
        out_shape=jax.ShapeDtypeStruct((n, d), F32),
        scratch_shapes=[pltpu.VMEM((2, TOP_K * tc * ROW_TILES, 128), jnp.uint32), pltpu.SemaphoreType.DMA((2,))],
        compiler_params=_cparams(("arbitrary",)),
        name="moe_combine",
    )(pos, pos, x1, gate, normf_g, ys)


def _route(top_idx, n_exp, tc):
    n = top_idx.shape[0]
    n_assign = n * TOP_K
    n_blocks = n_assign // ROW_BLOCK + n_exp
    tile = min(1024, n)
    hot = (top_idx[:, :, None] == jnp.arange(n_exp, dtype=jnp.int32)).any(axis=1).reshape(n // tile, tile, n_exp)
    earlier = jnp.asarray(np.tril(np.ones((tile, tile), np.float32), -1), BF16)
    within = jnp.einsum("ts,bse->bte", earlier, hot.astype(BF16), preferred_element_type=F32)
    tile_tot = jnp.sum(hot, axis=1, dtype=jnp.int32)
    tile_off = jnp.cumsum(tile_tot, axis=0) - tile_tot
    rank = (within.astype(jnp.int32) + tile_off[:, None, :]).reshape(n, n_exp)
    counts = jnp.sum(tile_tot, axis=0)
    padded = (counts + ROW_BLOCK - 1) // ROW_BLOCK * ROW_BLOCK
    pad_end = jnp.cumsum(padded)
    pad_start = pad_end - padded
    grp_start = jnp.cumsum(counts) - counts
    slot = jnp.take_along_axis(rank + pad_start[None, :], top_idx, axis=1).reshape(-1)
    block_start = jnp.arange(n_blocks, dtype=jnp.int32) * ROW_BLOCK
    block_expert = jnp.minimum(jnp.sum(pad_end[None, :] <= block_start[:, None], axis=1), n_exp - 1).astype(jnp.int32)
    n_valid = (pad_end[-1] // ROW_BLOCK).astype(jnp.int32).reshape(1)
    order = jnp.argsort(top_idx.reshape(-1), stable=True)
    row_e = jnp.repeat(block_expert, ROW_BLOCK)
    row_r = jnp.arange(n_blocks * ROW_BLOCK, dtype=jnp.int32) - pad_start[row_e]
    src = jnp.clip(grp_start[row_e] + row_r, 0, n_assign - 1)
    row_tok = jnp.where((row_r >= 0) & (row_r < counts[row_e]), order[src] // TOP_K, 0).astype(jnp.int32)
    pos = slot.reshape(n // tc, tc, TOP_K).transpose(0, 2, 1).reshape(n // tc, 1, TOP_K * tc)
    block_rows = jnp.clip(counts[block_expert] - (block_start - pad_start[block_expert]), 0, ROW_BLOCK)
    block_rows = jnp.where(jnp.arange(n_blocks) < n_valid[0], block_rows, 0).astype(jnp.int32)
    return row_tok, block_expert, block_rows, n_valid, pos, n_blocks


def _layer(x, norm1_g, w_in, mix_mu, w0, w_decay_up, a0, w_iclr_up, w_gate_up, k_k, k_a, r_k, lnx_g, lnx_b, b_qkv,
           sinks, w_up_rwkv, w_up_attn, w_out, norm2_g, w_router, b_router, w1, b1, w2, b2, normf_g):
    b, s, d = x.shape
    n = b * s
    c = w_up_rwkv.shape[0]
    lora = w_decay_up.shape[0]
    qc = w_up_attn.shape[0]
    kvc = KV_HEADS * HEAD
    rwkv_cols = 3 * c + 2 * lora + w_gate_up.shape[0]
    qkv_cols = qc + 2 * kvc
    row = lambda t: t.reshape(1, -1).astype(F32)
    xf = x.reshape(n, d)
    g1 = row(norm1_g)
    w_in_b = w_in.astype(BF16)

    qkv = _norm_proj(xf, g1, w_in_b[:, rwkv_cols:rwkv_cols + qkv_cols], row(b_qkv), BF16, 512)
    gates = _norm_proj(xf, g1, w_in_b[:, rwkv_cols + qkv_cols:], jnp.zeros((1, 2 * d), F32), BF16, 512)

    zl = jnp.zeros((lora, c), F32)
    w_lora = jnp.concatenate([jnp.concatenate([w_decay_up, zl], axis=1), jnp.concatenate([zl, w_iclr_up], axis=1)],
                             axis=0)
    r, lw, kf, v, kn, ba, g = _rwkv_prep(xf, g1, w_in_b[:, :rwkv_cols], s, c, row(mix_mu), w_lora, w_gate_up, row(w0),
                                         row(a0), row(k_k), row(k_a), 256)
    as3 = lambda t: t.reshape(b, s, c)
    y_rwkv = _rwkv_scan(as3(r), as3(lw), as3(kf), as3(v), as3(kn), as3(ba), as3(g), row(r_k), row(lnx_g),
                        row(lnx_b)).reshape(n, c)

    q = qkv[:, :qc].reshape(b, s, qc)
    ka = qkv[:, qc:qc + kvc].reshape(b, s, kvc)
    va = qkv[:, qc + kvc:].reshape(b, s, kvc)
    sinks_b = jnp.broadcast_to(sinks.astype(F32).reshape(-1, 1), (sinks.shape[0], 128))
    y_attn = _attention(q, ka, va, sinks_b).reshape(n, qc)

    x1, h2, gate, top_idx = _merge(xf, y_rwkv, y_attn, gates, w_up_rwkv.astype(BF16), w_up_attn.astype(BF16),
                                   w_out.astype(BF16), row(norm2_g), w_router, row(b_router), 256)

    tc = 256
    row_tok, block_expert, block_rows, n_valid, pos, n_blocks = _route(top_idx[:, :TOP_K], w_router.shape[1], tc)
    xs = _dispatch(h2.reshape(n, ROW_TILES, 128), row_tok, block_rows, n_blocks)
    ys = _expert_ffn(xs, w1, b1, w2, b2, block_expert, block_rows, n_valid, n_blocks)
    out = _combine(x1, gate, pos, ys.reshape(-1, ROW_TILES, 128), row(normf_g), tc)
    return out.reshape(b, s, d)


def kernel(x, norm1_g, w_in, mix_mu, w0, w_decay_up, a0, w_iclr_up, w_gate_up, k_k, k_a, r_k, lnx_g, lnx_b, b_qkv,
           sinks, w_up_rwkv, w_up_attn, w_out, norm2_g, w_router, b_router, w1, b1, w2, b2, normf_g):
    assert w_in.shape[0] == 1, "single-layer block"
    return _layer(x, norm1_g[0], w_in[0], mix_mu[0], w0[0], w_decay_up[0], a0[0], w_iclr_up[0], w_gate_up[0],
                  k_k[0], k_a[0], r_k[0], lnx_g[0], lnx_b[0], b_qkv[0], sinks[0], w_up_rwkv[0], w_up_attn[0],
                  w_out[0], norm2_g[0], w_router[0], b_router[0], w1[0], b1[0], w2[0], b2[0], normf_g)
```
